```python
import math
import jax
import jax.numpy as jnp
from jax import lax
import numpy as np

D_MODEL = 1024
BATCH = 4
SEQ = 8192
DEPTH = 2
DEC_BATCH = 32
DEC_SEQ = 4
PAST_LEN = 16384
PAGE_SIZE = 128

N_EVEN = (DEPTH + 1) // 2
N_ODD = DEPTH // 2

A_GROUPS = ((128, 1), (512, 4), (2048, 16))
N_BRANCH = len(A_GROUPS)
A_HEAD_DIM = 64
A_HEADS = D_MODEL // 128
A_WIDTH = A_HEADS * A_HEAD_DIM
BAND_BLOCK = 128
NUM_BUCKETS = 32
REL_MAX_DIST = 2048

POOL_WINDOWS = (2, 4, 8, 16)
POOL_GROUPS = len(POOL_WINDOWS)
POOL_GROUP_DIM = D_MODEL // 8
POOL_WIDTH = POOL_GROUPS * POOL_GROUP_DIM
POOL_BUF = max(POOL_WINDOWS) - 1

MIX_WIDTH = A_WIDTH + POOL_WIDTH
QKV_WIDTH = 3 * N_BRANCH * A_WIDTH
IN0_WIDTH = QKV_WIDTH + POOL_WIDTH

CONV_CH = D_MODEL
CONV_K = 31

D_FF = 256 * ((8 * D_MODEL // 3 + 255) // 256)
N_EXPERTS = 8
TOP_K = 2
D_FF_EXPERT = 7 * D_MODEL // 2

RMS_EPS = 1e-6
LN_EPS = 1e-5

kernel_name = 'hybrid_dilated_pool_conformer_moe_step'


def rms_norm(x, g):
    xf = x.astype(jnp.float32)
    y = xf * lax.rsqrt(jnp.mean(xf * xf, axis=-1, keepdims=True) + RMS_EPS)
    return (y * g.astype(jnp.float32)).astype(x.dtype)


def t5_bucket(dist):
    max_exact = NUM_BUCKETS // 2
    d = jnp.maximum(dist, 1).astype(jnp.float32)
    large = max_exact + (jnp.log(d / max_exact) / math.log(REL_MAX_DIST / max_exact)
                         * (NUM_BUCKETS - max_exact)).astype(jnp.int32)
    return jnp.where(dist < max_exact, dist, jnp.minimum(large, NUM_BUCKETS - 1))


def band_mask_bias(rel_bias_g, window, dil):
    slots = window // dil
    qi = jnp.arange(BAND_BLOCK)[:, None]
    kj = jnp.arange(2 * BAND_BLOCK)[None, :]
    rel = qi + BAND_BLOCK - kj
    in_band = (rel >= 0) & (rel <= slots)
    bias = rel_bias_g[t5_bucket(jnp.clip(rel, 0, slots) * dil)]
    return in_band, jnp.transpose(bias, (2, 0, 1))


def dilated_band_attention(q, k, v, rel_bias_g, window, dil):
    b, s, h, dh = q.shape
    m = s // dil
    nb = -(-m // BAND_BLOCK)
    mp = nb * BAND_BLOCK

    def to_blocks(t):
        t = t.reshape(b, m, dil, h, dh).transpose(0, 2, 1, 3, 4).reshape(b * dil, m, h, dh)
        t = jnp.pad(t, ((0, 0), (0, mp - m), (0, 0), (0, 0)))
        return t.reshape(b * dil, nb, BAND_BLOCK, h, dh)

    def with_prev(t):
        prev = jnp.pad(t, ((0, 0), (1, 0), (0, 0), (0, 0), (0, 0)))[:, :-1]
        return jnp.concatenate([prev, t], axis=2)

    qb = to_blocks(q).astype(jnp.float32)
    kw = with_prev(to_blocks(k)).astype(jnp.float32)
    vw = with_prev(to_blocks(v)).astype(jnp.float32)
    in_band, bias = band_mask_bias(rel_bias_g, window, dil)
    key_idx = (jnp.arange(nb)[:, None] - 1) * BAND_BLOCK + jnp.arange(2 * BAND_BLOCK)[None, :]
    valid = in_band[None] & (key_idx >= 0)[:, None, :]
    scores = jnp.einsum('xnqhd,xnkhd->xnhqk', qb, kw) * (A_HEAD_DIM ** -0.5) + bias[None, None]
    scores = jnp.where(valid[None, :, None], scores, -jnp.inf)
    lse = jax.nn.logsumexp(scores, axis=-1)
    p = jnp.exp(scores - lse[..., None])
    o = jnp.einsum('xnhqk,xnkhd->xnqhd', p, vw)
    o = o.reshape(b, dil, mp, h, dh)[:, :, :m].transpose(0, 2, 1, 3, 4).reshape(b, s, h, dh)
    lse = lse.transpose(0, 1, 3, 2).reshape(b, dil, mp, h)[:, :, :m].transpose(0, 2, 1, 3).reshape(b, s, h)
    return o, lse


def dilated_window_step(q, k_new, v_new, kv_buf, rel_bias_g, window, dil):
    n, t, h, dh = q.shape
    buf_len = kv_buf.shape[1]
    slots = window // dil
    kv_all = jnp.concatenate([kv_buf.astype(k_new.dtype), jnp.stack([k_new, v_new], axis=2)], axis=1)
    slot = jnp.arange(slots + 1)
    idx = buf_len + jnp.arange(t)[:, None] - slot[None, :] * dil
    valid = idx >= 0
    kvg = kv_all[:, jnp.maximum(idx, 0)].astype(jnp.float32)
    bias = rel_bias_g[t5_bucket(slot * dil)]
    scores = (jnp.einsum('nthd,ntshd->nhts', q.astype(jnp.float32), kvg[:, :, :, 0]) * (A_HEAD_DIM ** -0.5)
              + bias.T[None, :, None, :])
    scores = jnp.where(valid[None, None], scores, -jnp.inf)
    lse = jax.nn.logsumexp(scores, axis=-1)
    p = jnp.exp(scores - lse[..., None])
    o = jnp.einsum('nhts,ntshd->nthd', p, kvg[:, :, :, 1])
    return o, lse.transpose(0, 2, 1), kv_all[:, kv_all.shape[1] - buf_len:]


def merge_branches(outs, lses):
    o = jnp.stack(outs, axis=2)
    wts = jax.nn.softmax(jnp.stack(lses, axis=2), axis=2)
    return jnp.sum(o * wts[..., None], axis=2)


def multiscale_pool(u, w_pool, pool_scale):
    b, s, _ = u.shape
    uf = u.astype(jnp.float32).reshape(b, s, POOL_GROUPS, POOL_GROUP_DIM)
    cs = jnp.cumsum(uf, axis=1)
    pos = jnp.arange(s)
    means = []
    for g, w in enumerate(POOL_WINDOWS):
        c = cs[:, :, g]
        lagged = jnp.pad(c, ((0, 0), (w, 0), (0, 0)))[:, :s]
        cnt = jnp.minimum(pos + 1, w).astype(jnp.float32)[None, :, None]
        means.append((c - lagged) / cnt)
    mix = jnp.stack(means, axis=2) - uf
    y = jnp.einsum('bsgc,gcd->bsgd', mix, w_pool.astype(jnp.float32))
    y = y * pool_scale.astype(jnp.float32).reshape(POOL_GROUPS, POOL_GROUP_DIM)
    return y.reshape(b, s, POOL_WIDTH).astype(u.dtype)


def project_in(hn, w_in):
    proj = hn @ w_in
    b, s = proj.shape[0], proj.shape[1]
    qkv = proj[..., :QKV_WIDTH].reshape(b, s, 3, N_BRANCH, A_HEADS, A_HEAD_DIM)
    return qkv, proj[..., QKV_WIDTH:]


def mixer_ab_prompt(hn, w_in, rel_bias, w_pool, pool_scale, w_out):
    b, s, _ = hn.shape
    qkv, u = project_in(hn, w_in)
    rb = rel_bias.astype(jnp.float32).reshape(NUM_BUCKETS, N_BRANCH, A_HEADS)
    outs, lses, kv_states = [], [], []
    for g, (window, dil) in enumerate(A_GROUPS):
        q, k, v = qkv[:, :, 0, g], qkv[:, :, 1, g], qkv[:, :, 2, g]
        o, l = dilated_band_attention(q, k, v, rb[:, g], window, dil)
        outs.append(o)
        lses.append(l)
        keep = min(window, s)
        kv_states.append(jnp.stack([k[:, s - keep:], v[:, s - keep:]], axis=2))
    a_out = merge_branches(outs, lses).reshape(b, s, A_WIDTH).astype(hn.dtype)
    p_out = multiscale_pool(u, w_pool, pool_scale)
    y = jnp.concatenate([a_out, p_out], axis=-1) @ w_out
    return y, kv_states, u[:, s - POOL_BUF:]


def mixer_ab_sample(hn, kv_bufs, pool_buf, w_in, rel_bias, w_pool, pool_scale, w_out):
    n, t, _ = hn.shape
    qkv, u = project_in(hn, w_in)
    rb = rel_bias.astype(jnp.float32).reshape(NUM_BUCKETS, N_BRANCH, A_HEADS)
    outs, lses, new_bufs = [], [], []
    for g, (window, dil) in enumerate(A_GROUPS):
        q, k, v = qkv[:, :, 0, g], qkv[:, :, 1, g], qkv[:, :, 2, g]
        o, l, nbuf = dilated_window_step(q, k, v, kv_bufs[g], rb[:, g], window, dil)
        outs.append(o)
        lses.append(l)
        new_bufs.append(nbuf)
    a_out = merge_branches(outs, lses).reshape(n, t, A_WIDTH).astype(hn.dtype)
    ext = jnp.concatenate([pool_buf.astype(u.dtype), u], axis=1)
    p_out = multiscale_pool(ext, w_pool, pool_scale)[:, POOL_BUF:]
    y = jnp.concatenate([a_out, p_out], axis=-1) @ w_out
    return y, new_bufs, ext[:, ext.shape[1] - POOL_BUF:]


def conv_module(hn, buf, w_pw1, b_pw1, w_dw, b_dw, ln_g, ln_b, w_pw2, b_pw2):
    a = hn @ w_pw1 + b_pw1
    glu = a[..., :CONV_CH] * jax.nn.sigmoid(a[..., CONV_CH:])
    ext = jnp.concatenate([buf.astype(glu.dtype), glu], axis=1)
    z = lax.conv_general_dilated(ext, w_dw[:, None, :].astype(glu.dtype), window_strides=(1,),
                                 padding='VALID', dimension_numbers=('NWC', 'WIO', 'NWC'),
                                 feature_group_count=CONV_CH) + b_dw
    zf = z.astype(jnp.float32)
    mu = jnp.mean(zf, axis=-1, keepdims=True)
    var = jnp.mean(jnp.square(zf - mu), axis=-1, keepdims=True)
    zf = (zf - mu) * lax.rsqrt(var + LN_EPS) * ln_g.astype(jnp.float32) + ln_b.astype(jnp.float32)
    z = jax.nn.silu(zf).astype(hn.dtype)
    return z @ w_pw2 + b_pw2, ext[:, ext.shape[1] - (CONV_K - 1):]


def swiglu(h, w_gate, w_up, w_down):
    return (jax.nn.silu(h @ w_gate) * (h @ w_up)) @ w_down


def moe_swiglu(h, w_router, we_gate, we_up, we_down):
    logits = jnp.einsum('bsd,de->bse', h.astype(jnp.float32), w_router.astype(jnp.float32))
    top_v, top_i = lax.top_k(logits, TOP_K)
    gates = jax.nn.softmax(top_v, axis=-1)
    dense_gate = jnp.sum(jax.nn.one_hot(top_i, N_EXPERTS, dtype=jnp.float32) * gates[..., None], axis=-2)
    out = jnp.zeros(h.shape, jnp.float32)
    for e in range(N_EXPERTS):
        out = out + dense_gate[..., e:e + 1] * swiglu(h, we_gate[e], we_up[e], we_down[e]).astype(jnp.float32)
    return out.astype(h.dtype)


def setup_inputs(seed: int = 0) -> dict:
    key = jax.random.key(seed)
    keys = iter(jax.random.split(key, 40))

    def rnd(shape, scale):
        return jax.random.normal(next(keys), shape, jnp.float32) * scale

    ne, no = N_EVEN, N_ODD
    inp = {}
    inp['x_prompt'] = rnd((BATCH, SEQ, D_MODEL), 1.0)
    inp['x_sample'] = rnd((DEC_BATCH, DEC_SEQ, D_MODEL), 1.0)
    inp['cache_kv_w128'] = rnd((ne, DEC_BATCH, min(A_GROUPS[0][0], PAST_LEN), 2, A_HEADS, A_HEAD_DIM), 1.0)
    inp['cache_kv_w512'] = rnd((ne, DEC_BATCH, min(A_GROUPS[1][0], PAST_LEN), 2, A_HEADS, A_HEAD_DIM), 1.0)
    inp['cache_kv_w2048'] = rnd((ne, DEC_BATCH, min(A_GROUPS[2][0], PAST_LEN), 2, A_HEADS, A_HEAD_DIM), 1.0)
    inp['state_pool'] = rnd((ne, DEC_BATCH, POOL_BUF, POOL_WIDTH), 1.0)
    inp['state_conv'] = rnd((no, DEC_BATCH, CONV_K - 1, CONV_CH), 0.5)
    inp['norm_mix0'] = 1.0 + rnd((ne, D_MODEL), 0.02)
    inp['w_in0'] = rnd((ne, D_MODEL, IN0_WIDTH), D_MODEL ** -0.5)
    inp['rel_bias'] = rnd((NUM_BUCKETS, N_BRANCH * A_HEADS), 0.5)
    inp['w_pool'] = rnd((ne, POOL_GROUPS, POOL_GROUP_DIM, POOL_GROUP_DIM), POOL_GROUP_DIM ** -0.5)
    inp['pool_scale'] = 1.0 + rnd((ne, POOL_WIDTH), 0.02)
    inp['w_out0'] = rnd((ne, MIX_WIDTH, D_MODEL), MIX_WIDTH ** -0.5)
    inp['norm_ffn0'] = 1.0 + rnd((ne, D_MODEL), 0.02)
    inp['w_ff_gate'] = rnd((ne, D_MODEL, D_FF), D_MODEL ** -0.5)
    inp['w_ff_up'] = rnd((ne, D_MODEL, D_FF), D_MODEL ** -0.5)
    inp['w_ff_down'] = rnd((ne, D_FF, D_MODEL), D_FF ** -0.5)
    inp['norm_mix1'] = 1.0 + rnd((no, D_MODEL), 0.02)
    inp['w_pw1'] = rnd((no, D_MODEL, 2 * CONV_CH), D_MODEL ** -0.5)
    inp['b_pw1'] = rnd((no, 2 * CONV_CH), 0.02)
    inp['w_dw'] = rnd((no, CONV_K, CONV_CH), CONV_K ** -0.5)
    inp['b_dw'] = rnd((no, CONV_CH), 0.02)
    inp['ln_g'] = 1.0 + rnd((no, CONV_CH), 0.02)
    inp['ln_b'] = rnd((no, CONV_CH), 0.02)
    inp['w_pw2'] = rnd((no, CONV_CH, D_MODEL), CONV_CH ** -0.5)
    inp['b_pw2'] = rnd((no, D_MODEL), 0.02)
    inp['norm_ffn1'] = 1.0 + rnd((no, D_MODEL), 0.02)
    inp['w_router'] = rnd((no, D_MODEL, N_EXPERTS), D_MODEL ** -0.5)
    inp['we_gate'] = rnd((no, N_EXPERTS, D_MODEL, D_FF_EXPERT), D_MODEL ** -0.5)
    inp['we_up'] = rnd((no, N_EXPERTS, D_MODEL, D_FF_EXPERT), D_MODEL ** -0.5)
    inp['we_down'] = rnd((no, N_EXPERTS, D_FF_EXPERT, D_MODEL), D_FF_EXPERT ** -0.5)
    inp['norm_final'] = 1.0 + rnd((D_MODEL,), 0.02)
    return inp


def reference(x_prompt, x_sample, cache_kv_w128, cache_kv_w512, cache_kv_w2048, state_pool, state_conv,
              norm_mix0, w_in0, rel_bias, w_pool, pool_scale, w_out0,
              norm_ffn0, w_ff_gate, w_ff_up, w_ff_down,
              norm_mix1, w_pw1, b_pw1, w_dw, b_dw, ln_g, ln_b, w_pw2, b_pw2,
              norm_ffn1, w_router, we_gate, we_up, we_down, norm_final):
    h_p, h_s = x_prompt, x_sample
    kv_in = (cache_kv_w128, cache_kv_w512, cache_kv_w2048)
    kv_p = [[] for _ in A_GROUPS]
    kv_s = [[] for _ in A_GROUPS]
    pool_p, pool_s, conv_p, conv_s = [], [], [], []
    for layer in range(DEPTH):
        i = layer // 2
        if layer % 2 == 0:
            y, kvs, ps = mixer_ab_prompt(rms_norm(h_p, norm_mix0[i]), w_in0[i], rel_bias,
                                         w_pool[i], pool_scale[i], w_out0[i])
            h_p = h_p + y
            y, kvn, pn = mixer_ab_sample(rms_norm(h_s, norm_mix0[i]), [c[i] for c in kv_in], state_pool[i],
                                         w_in0[i], rel_bias, w_pool[i], pool_scale[i], w_out0[i])
            h_s = h_s + y
            for g in range(N_BRANCH):
                kv_p[g].append(kvs[g])
                kv_s[g].append(kvn[g])
            pool_p.append(ps)
            pool_s.append(pn)
            h_p = h_p + swiglu(rms_norm(h_p, norm_ffn0[i]), w_ff_gate[i], w_ff_up[i], w_ff_down[i])
            h_s = h_s + swiglu(rms_norm(h_s, norm_ffn0[i]), w_ff_gate[i], w_ff_up[i], w_ff_down[i])
        else:
            conv_w = (w_pw1[i], b_pw1[i], w_dw[i], b_dw[i], ln_g[i], ln_b[i], w_pw2[i], b_pw2[i])
            zero_buf = jnp.zeros((h_p.shape[0], CONV_K - 1, CONV_CH), h_p.dtype)
            y, cp = conv_module(rms_norm(h_p, norm_mix1[i]), zero_buf, *conv_w)
            h_p = h_p + y
            y, cn = conv_module(rms_norm(h_s, norm_mix1[i]), state_conv[i], *conv_w)
            h_s = h_s + y
            conv_p.append(cp)
            conv_s.append(cn)
            h_p = h_p + moe_swiglu(rms_norm(h_p, norm_ffn1[i]), w_router[i], we_gate[i], we_up[i], we_down[i])
            h_s = h_s + moe_swiglu(rms_norm(h_s, norm_ffn1[i]), w_router[i], we_gate[i], we_up[i], we_down[i])
    y_prompt = rms_norm(h_p, norm_final)
    y_sample = rms_norm(h_s, norm_final)
    kv_w128_prompt = jnp.stack(kv_p[0])
    kv_w512_prompt = jnp.stack(kv_p[1])
    kv_w2048_prompt = jnp.stack(kv_p[2])
    pool_prompt = jnp.stack(pool_p)
    conv_prompt = jnp.stack(conv_p)
    kv_w128_sample = jnp.stack(kv_s[0])
    kv_w512_sample = jnp.stack(kv_s[1])
    kv_w2048_sample = jnp.stack(kv_s[2])
    pool_sample = jnp.stack(pool_s)
    conv_sample = jnp.stack(conv_s)
    return (y_prompt, y_sample, kv_w128_prompt, kv_w512_prompt, kv_w2048_prompt, pool_prompt, conv_prompt,
            kv_w128_sample, kv_w512_sample, kv_w2048_sample, pool_sample, conv_sample)
```

```python
import functools
import math

import jax
import jax.numpy as jnp
from jax import lax
from jax.experimental import pallas as pl
from jax.experimental.pallas import tpu as pltpu

F32 = jnp.float32
BF16 = jnp.bfloat16

A_GROUPS = ((128, 1), (512, 4), (2048, 16))
N_BRANCH = 3
A_HEADS = 8
A_HEAD_DIM = 64
A_WIDTH = 512
BAND = 128
NUM_BUCKETS = 32
REL_MAX_DIST = 2048
POOL_WINDOWS = (2, 4, 8, 16)
POOL_GROUP_DIM = 128
POOL_WIDTH = 512
POOL_BUF = 15
CONV_K = 31
N_EXPERTS = 8
RMS_EPS = 1e-6
LN_EPS = 1e-5
NEG = -1e30

VMEM_LIMIT_BYTES = 56 * 1024 * 1024
LANES = 128
CHUNK = 2048


def _params(*sem):
    return pltpu.CompilerParams(dimension_semantics=sem, vmem_limit_bytes=VMEM_LIMIT_BYTES)


def _rmsnorm(x, g):
    return x * lax.rsqrt(jnp.mean(x * x, axis=-1, keepdims=True) + RMS_EPS) * g


def _sigmoid(x):
    return 1.0 / (1.0 + jnp.exp(-x))


def _round_bf16(x):
    return x.astype(BF16).astype(F32)


def _rms_matmul_kernel(x_ref, g_ref, w_ref, b_ref, o_ref, hn_ref):
    @pl.when(pl.program_id(1) == 0)
    def _():
        hn_ref[...] = _rmsnorm(x_ref[...], g_ref[...]).astype(BF16)

    acc = jnp.dot(hn_ref[...], w_ref[...], preferred_element_type=F32)
    o_ref[...] = (acc + b_ref[...]).astype(o_ref.dtype)


def rms_matmul(x, g, w, b, tm, tn, out_dtype):
    m, k = x.shape
    n = w.shape[1]
    assert m % tm == 0 and n % tn == 0
    return pl.pallas_call(
        _rms_matmul_kernel,
        grid=(m // tm, n // tn),
        in_specs=[
            pl.BlockSpec((tm, k), lambda i, j: (i, 0)),
            pl.BlockSpec((1, k), lambda i, j: (0, 0)),
            pl.BlockSpec((k, tn), lambda i, j: (0, j)),
            pl.BlockSpec((1, tn), lambda i, j: (0, j)),
        ],
        out_specs=pl.BlockSpec((tm, tn), lambda i, j: (i, j)),
        out_shape=jax.ShapeDtypeStruct((m, n), out_dtype),
        scratch_shapes=[pltpu.VMEM((tm, k), BF16)],
        compiler_params=_params("parallel", "arbitrary"),
        name="rms_matmul",
    )(x, g, w, b)


def _rms_glu_kernel(x_ref, g_ref, wv_ref, wg_ref, bv_ref, bg_ref, o_ref, hn_ref):
    @pl.when(pl.program_id(1) == 0)
    def _():
        hn_ref[...] = _rmsnorm(x_ref[...], g_ref[...]).astype(BF16)

    hn = hn_ref[...]
    val = jnp.dot(hn, wv_ref[...], preferred_element_type=F32) + bv_ref[...]
    gate = jnp.dot(hn, wg_ref[...], preferred_element_type=F32) + bg_ref[...]
    o_ref[...] = val * _sigmoid(gate)


def rms_glu(x, g, w, b, tm, tn):
    m, k = x.shape
    c = w.shape[1] // 2
    nb = c // tn
    return pl.pallas_call(
        _rms_glu_kernel,
        grid=(m // tm, nb),
        in_specs=[
            pl.BlockSpec((tm, k), lambda i, j: (i, 0)),
            pl.BlockSpec((1, k), lambda i, j: (0, 0)),
            pl.BlockSpec((k, tn), lambda i, j: (0, j)),
            pl.BlockSpec((k, tn), lambda i, j: (0, j + nb)),
            pl.BlockSpec((1, tn), lambda i, j: (0, j)),
            pl.BlockSpec((1, tn), lambda i, j: (0, j + nb)),
        ],
        out_specs=pl.BlockSpec((tm, tn), lambda i, j: (i, j)),
        out_shape=jax.ShapeDtypeStruct((m, c), F32),
        scratch_shapes=[pltpu.VMEM((tm, k), BF16)],
        compiler_params=_params("parallel", "arbitrary"),
        name="rms_glu",
    )(x, g, w, w, b, b)


def _qkv_perm_kernel(x_ref, g_ref, w_ref, o_ref, hn_ref, slab_ref):
    j = pl.program_id(1)

    @pl.when(j == 0)
    def _():
        hn_ref[...] = _rmsnorm(x_ref[...], g_ref[...]).astype(BF16)

    res = jnp.dot(hn_ref[...], w_ref[...], preferred_element_type=F32)
    n_slab = res.shape[1] // LANES

    @pl.when(j < 3)
    def _():
        o_ref[...] = res.astype(BF16)

    for s in range(n_slab):
        slab_ref[s] = res[:, s * LANES:(s + 1) * LANES]

    for branch in (1, 2):
        dil = A_GROUPS[branch][1]
        rows = CHUNK // dil

        @pl.when(j // 3 == branch)
        def _():
            for r in range(dil):
                for s in range(n_slab):
                    v = slab_ref[s, pl.ds(r, rows, stride=dil), :]
                    o_ref[r * rows:(r + 1) * rows, s * LANES:(s + 1) * LANES] = v.astype(BF16)


def qkv_perm(x, g, w):
    m, k = x.shape
    n = w.shape[1]
    tn = A_WIDTH
    return pl.pallas_call(
        _qkv_perm_kernel,
        grid=(m // CHUNK, n // tn),
        in_specs=[
            pl.BlockSpec((CHUNK, k), lambda i, j: (i, 0)),
            pl.BlockSpec((1, k), lambda i, j: (0, 0)),
            pl.BlockSpec((k, tn), lambda i, j: (0, j)),
        ],
        out_specs=pl.BlockSpec((CHUNK, tn), lambda i, j: (i, j)),
        out_shape=jax.ShapeDtypeStruct((m, n), BF16),
        scratch_shapes=[pltpu.VMEM((CHUNK, k), BF16), pltpu.VMEM((tn // LANES, CHUNK, LANES), F32)],
        compiler_params=_params("parallel", "arbitrary"),
        name="qkv_perm",
    )(x, g, w)


def _band_attn_kernel(*refs):
    qkv_refs = refs[:15]
    bias_ref, o_ref, acc_ref, lse_ref = refs[15:]
    c = pl.program_id(1)
    lane = lax.broadcasted_iota(jnp.int32, (BAND, LANES), 1)
    head_of_lane = lane // A_HEAD_DIM
    n_units = CHUNK // BAND

    for branch, (_, dil) in enumerate(A_GROUPS):
        q_ref, kprev_ref, kcur_ref, vprev_ref, vcur_ref = qkv_refs[5 * branch:5 * branch + 5]
        nblk = n_units // dil

        def unit(u, carry, q_ref=q_ref, kprev_ref=kprev_ref, kcur_ref=kcur_ref,
                 vprev_ref=vprev_ref, vcur_ref=vcur_ref, nblk=nblk, dil=dil, branch=branch):
            blk = u % nblk
            res = u // nblk
            first = blk == 0
            row = pl.multiple_of(u * BAND, BAND)
            prow = pl.multiple_of(jnp.where(first, u + nblk - 1, u - 1) * BAND, BAND)
            q = q_ref[pl.ds(row, BAND), :]
            kc = kcur_ref[pl.ds(row, BAND), :]
            vc = vcur_ref[pl.ds(row, BAND), :]
            kp = jnp.where(first, kprev_ref[pl.ds(prow, BAND), :], kcur_ref[pl.ds(prow, BAND), :])
            vp = jnp.where(first, vprev_ref[pl.ds(prow, BAND), :], vcur_ref[pl.ds(prow, BAND), :])
            pen = jnp.where(jnp.logical_and(first, c == 0), NEG, 0.0).astype(F32)

            o_full = jnp.zeros((BAND, LANES), F32)
            lse_full = jnp.zeros((BAND, LANES), F32)
            for hh in range(2):
                mine = head_of_lane == hh
                qh = jnp.where(mine, q, jnp.zeros_like(q)) * jnp.asarray(A_HEAD_DIM ** -0.5, BF16)
                dn = (((1,), (1,)), ((), ()))
                s_p = lax.dot_general(qh, kp, dn, preferred_element_type=F32) + bias_ref[branch, hh, 0] + pen
                s_c = lax.dot_general(qh, kc, dn, preferred_element_type=F32) + bias_ref[branch, hh, 1]
                mx = jnp.maximum(jnp.max(s_p, axis=-1, keepdims=True), jnp.max(s_c, axis=-1, keepdims=True))
                p_p = jnp.exp(s_p - mx)
                p_c = jnp.exp(s_c - mx)
                den = jnp.sum(p_p, axis=-1, keepdims=True) + jnp.sum(p_c, axis=-1, keepdims=True)
                inv = 1.0 / den
                o = (jnp.dot((p_p * inv).astype(BF16), vp, preferred_element_type=F32)
                     + jnp.dot((p_c * inv).astype(BF16), vc, preferred_element_type=F32))
                o_full = jnp.where(mine, o, o_full)
                lse_full = jnp.where(mine, mx + jnp.log(den), lse_full)

            start = res + dil * BAND * blk
            if dil == 1:
                rows = pl.ds(pl.multiple_of(start, BAND), BAND)
            else:
                rows = pl.ds(start, BAND, stride=dil)
            acc_ref[branch, rows, :] = o_full
            lse_ref[branch, rows, :] = lse_full
            return carry

        lax.fori_loop(0, n_units, unit, 0)

    l0, l1, l2 = lse_ref[0], lse_ref[1], lse_ref[2]
    mx = jnp.maximum(jnp.maximum(l0, l1), l2)
    w0, w1, w2 = jnp.exp(l0 - mx), jnp.exp(l1 - mx), jnp.exp(l2 - mx)
    inv = 1.0 / (w0 + w1 + w2)
    o_ref[...] = (acc_ref[0] * (w0 * inv) + acc_ref[1] * (w1 * inv) + acc_ref[2] * (w2 * inv)).astype(o_ref.dtype)


def band_attention(qkv, bias_tbl, batch, seq):
    n_chunk = seq // CHUNK
    n_pair = A_WIDTH // LANES
    blk = (CHUNK, LANES)
    in_specs = []
    for branch in range(N_BRANCH):
        qcol, kcol, vcol = [(3 * branch + kind) * n_pair for kind in range(3)]

        def cur(col):
            return lambda b, c, hp, col=col: (b * n_chunk + c, col + hp)

        def prev(col):
            return lambda b, c, hp, col=col: (b * n_chunk + jnp.maximum(c - 1, 0), col + hp)

        in_specs += [pl.BlockSpec(blk, cur(qcol)), pl.BlockSpec(blk, prev(kcol)), pl.BlockSpec(blk, cur(kcol)),
                     pl.BlockSpec(blk, prev(vcol)), pl.BlockSpec(blk, cur(vcol))]
    in_specs.append(pl.BlockSpec((N_BRANCH, 2, 2, BAND, BAND), lambda b, c, hp: (0, hp, 0, 0, 0)))
    return pl.pallas_call(
        _band_attn_kernel,
        grid=(batch, n_chunk, n_pair),
        in_specs=in_specs,
        out_specs=pl.BlockSpec(blk, lambda b, c, hp: (b * n_chunk + c, hp)),
        out_shape=jax.ShapeDtypeStruct((batch * seq, A_WIDTH), BF16),
        scratch_shapes=[pltpu.VMEM((N_BRANCH, CHUNK, LANES), F32)] * 2,
        compiler_params=_params("parallel", "parallel", "arbitrary"),
        name="band_attention",
    )(*([qkv] * 15), bias_tbl)


def _mix_out_kernel(a_ref, u_ref, halo_ref, x_ref, wpool_ref, scale_ref, woa_ref, wop_ref, o_ref, ext_ref,
                    *, tiles_per_seq):
    i = pl.program_id(0)
    tm = a_ref.shape[0]
    halo = halo_ref.shape[0]
    seq_tile = i % tiles_per_seq
    ext_ref[0:halo, :] = jnp.where(seq_tile == 0, 0.0, halo_ref[...])
    ext_ref[halo:halo + tm, :] = u_ref[...]
    pos = seq_tile * tm + lax.broadcasted_iota(jnp.int32, (tm, POOL_GROUP_DIM), 0)
    acc = x_ref[...] + jnp.dot(a_ref[...], woa_ref[...], preferred_element_type=F32)
    for grp, win in enumerate(POOL_WINDOWS):
        cols = slice(grp * POOL_GROUP_DIM, (grp + 1) * POOL_GROUP_DIM)
        tot = ext_ref[halo:halo + tm, cols]
        for back in range(1, win):
            tot = tot + ext_ref[halo - back:halo - back + tm, cols]
        cnt = jnp.minimum(pos + 1, win).astype(F32)
        mix = tot / cnt - ext_ref[halo:halo + tm, cols]
        y = jnp.dot(mix.astype(BF16), wpool_ref[grp], preferred_element_type=F32) * scale_ref[:, cols]
        acc = acc + jnp.dot(y.astype(BF16), wop_ref[cols, :], preferred_element_type=F32)
    o_ref[...] = acc


def mix_out(a, u, x, w_pool, pool_scale, w_out_a, w_out_p, tm, seq):
    m, d = x.shape
    halo = 16
    assert seq % tm == 0 and tm % halo == 0
    kern = functools.partial(_mix_out_kernel, tiles_per_seq=seq // tm)
    return pl.pallas_call(
        kern,
        grid=(m // tm,),
        in_specs=[
            pl.BlockSpec((tm, A_WIDTH), lambda i: (i, 0)),
            pl.BlockSpec((tm, POOL_WIDTH), lambda i: (i, 0)),
            pl.BlockSpec((halo, POOL_WIDTH), lambda i: (jnp.maximum(i * (tm // halo) - 1, 0), 0)),
            pl.BlockSpec((tm, d), lambda i: (i, 0)),
            pl.BlockSpec((len(POOL_WINDOWS), POOL_GROUP_DIM, POOL_GROUP_DIM), lambda i: (0, 0, 0)),
            pl.BlockSpec((1, POOL_WIDTH), lambda i: (0, 0)),
            pl.BlockSpec((A_WIDTH, d), lambda i: (0, 0)),
            pl.BlockSpec((POOL_WIDTH, d), lambda i: (0, 0)),
        ],
        out_specs=pl.BlockSpec((tm, d), lambda i: (i, 0)),
        out_shape=jax.ShapeDtypeStruct((m, d), F32),
        scratch_shapes=[pltpu.VMEM((tm + halo, POOL_WIDTH), F32)],
        compiler_params=_params("parallel"),
        name="mix_out",
    )(a, u, u, x, w_pool, pool_scale, w_out_a, w_out_p)


def _mix_out_sample_kernel(a_ref, mix_ref, x_ref, wpool_ref, scale_ref, woa_ref, wop_ref, o_ref):
    acc = x_ref[...] + jnp.dot(a_ref[...].astype(BF16), woa_ref[...], preferred_element_type=F32)
    for grp in range(len(POOL_WINDOWS)):
        cols = slice(grp * POOL_GROUP_DIM, (grp + 1) * POOL_GROUP_DIM)
        y = jnp.dot(mix_ref[:, cols].astype(BF16), wpool_ref[grp], preferred_element_type=F32) * scale_ref[:, cols]
        acc = acc + jnp.dot(y.astype(BF16), wop_ref[cols, :], preferred_element_type=F32)
    o_ref[...] = acc


def mix_out_sample(a, mix, x, w_pool, pool_scale, w_out_a, w_out_p):
    m, d = x.shape
    return pl.pallas_call(
        _mix_out_sample_kernel,
        out_shape=jax.ShapeDtypeStruct((m, d), F32),
        compiler_params=pltpu.CompilerParams(vmem_limit_bytes=VMEM_LIMIT_BYTES),
        name="mix_out_sample",
    )(a, mix, x, w_pool, pool_scale, w_out_a, w_out_p)


def _top2_gates(logits):
    lane = lax.broadcasted_iota(jnp.int32, logits.shape, 1)
    lg = jnp.where(lane < N_EXPERTS, logits, -jnp.inf)
    m1 = jnp.max(lg, axis=-1, keepdims=True)
    i1 = jnp.min(jnp.where(lg == m1, lane, LANES), axis=-1, keepdims=True)
    rest = jnp.where(lane == i1, -jnp.inf, lg)
    m2 = jnp.max(rest, axis=-1, keepdims=True)
    i2 = jnp.min(jnp.where(rest == m2, lane, LANES), axis=-1, keepdims=True)
    e2 = jnp.exp(m2 - m1)
    g1 = 1.0 / (1.0 + e2)
    g2 = e2 / (1.0 + e2)
    return jnp.where(lane == i1, g1, 0.0) + jnp.where(lane == i2, g2, 0.0)


def _swiglu_kernel(*refs, n_expert, final_norm):
    x_ref, g_ref = refs[0], refs[1]
    k = 2
    if n_expert > 1:
        wr_ref = refs[k]
        k += 1
    wg_ref, wu_ref, wd_ref = refs[k:k + 3]
    k += 3
    if final_norm:
        gf_ref = refs[k]
        k += 1
    o_ref, hn_ref, acc_ref = refs[k:k + 3]
    if n_expert > 1:
        gate_ref = refs[k + 3]
    e = pl.program_id(1)
    j = pl.program_id(2)

    @pl.when(jnp.logical_and(e == 0, j == 0))
    def _():
        hn = _rmsnorm(x_ref[...], g_ref[...])
        hn_ref[...] = hn.astype(BF16)
        acc_ref[...] = jnp.zeros_like(acc_ref)
        if n_expert > 1:
            logits = jnp.dot(hn.astype(BF16), wr_ref[...], preferred_element_type=F32)
            gate_ref[...] = _top2_gates(logits)

    hn = hn_ref[...]
    a = jnp.dot(hn, wg_ref[...], preferred_element_type=F32)
    b = jnp.dot(hn, wu_ref[...], preferred_element_type=F32)
    h = (a * _sigmoid(a) * b).astype(BF16)
    y = jnp.dot(h, wd_ref[...], preferred_element_type=F32)
    if n_expert > 1:
        lane = lax.broadcasted_iota(jnp.int32, gate_ref.shape, 1)
        gate = jnp.sum(jnp.where(lane == e, gate_ref[...], 0.0), axis=-1, keepdims=True)
        y = gate * y
    acc_ref[...] += y

    @pl.when(jnp.logical_and(e == n_expert - 1, j == pl.num_programs(2) - 1))
    def _():
        out = x_ref[...] + acc_ref[...]
        if final_norm:
            out = _rmsnorm(out, gf_ref[...])
        o_ref[...] = out


def swiglu_block(x, g, w_gate, w_up, w_down, tm, tf, w_router=None, g_final=None):
    m, d = x.shape
    n_expert, _, f = w_gate.shape
    assert m % tm == 0 and f % tf == 0
    args = [x, g]
    in_specs = [pl.BlockSpec((tm, d), lambda i, e, j: (i, 0)), pl.BlockSpec((1, d), lambda i, e, j: (0, 0))]
    if n_expert > 1:
        args.append(w_router)
        in_specs.append(pl.BlockSpec((d, LANES), lambda i, e, j: (0, 0)))
    args += [w_gate, w_up, w_down]
    in_specs += [
        pl.BlockSpec((None, d, tf), lambda i, e, j: (e, 0, j)),
        pl.BlockSpec((None, d, tf), lambda i, e, j: (e, 0, j)),
        pl.BlockSpec((None, tf, d), lambda i, e, j: (e, j, 0)),
    ]
    if g_final is not None:
        args.append(g_final)
        in_specs.append(pl.BlockSpec((1, d), lambda i, e, j: (0, 0)))
    scratch = [pltpu.VMEM((tm, d), BF16), pltpu.VMEM((tm, d), F32)]
    if n_expert > 1:
        scratch.append(pltpu.VMEM((tm, LANES), F32))
    kern = functools.partial(_swiglu_kernel, n_expert=n_expert, final_norm=g_final is not None)
    return pl.pallas_call(
        kern,
        grid=(m // tm, n_expert, f // tf),
        in_specs=in_specs,
        out_specs=pl.BlockSpec((tm, d), lambda i, e, j: (i, 0)),
        out_shape=jax.ShapeDtypeStruct((m, d), F32),
        scratch_shapes=scratch,
        compiler_params=_params("parallel", "arbitrary", "arbitrary"),
        name="swiglu_block",
    )(*args)


def _ln_silu_pw2(z, lng_ref, lnb_ref, w2_ref, b2_ref):
    mu = jnp.mean(z, axis=-1, keepdims=True)
    zc = z - mu
    var = jnp.mean(zc * zc, axis=-1, keepdims=True)
    zn = zc * lax.rsqrt(var + LN_EPS) * lng_ref[...] + lnb_ref[...]
    act = (zn * _sigmoid(zn)).astype(BF16)
    return jnp.dot(act, w2_ref[...], preferred_element_type=F32) + b2_ref[...]


def _conv_kernel(glu_ref, halo_ref, x_ref, wdw_ref, bdw_ref, lng_ref, lnb_ref, w2_ref, b2_ref, o_ref,
                 ext_ref, z_ref, *, tiles_per_seq, row_chunk):
    i = pl.program_id(0)
    tm = glu_ref.shape[0]
    halo = halo_ref.shape[0]
    n_slab = glu_ref.shape[1] // LANES
    at_start = i % tiles_per_seq == 0
    for cb in range(n_slab):
        cols = slice(cb * LANES, (cb + 1) * LANES)
        ext_ref[cb, 0:halo, :] = jnp.where(at_start, 0.0, _round_bf16(halo_ref[:, cols]))
        ext_ref[cb, halo:halo + tm, :] = _round_bf16(glu_ref[:, cols])
    base = halo - (CONV_K - 1)

    def slab(cb, carry):
        for rc in range(tm // row_chunk):
            r0 = rc * row_chunk
            acc = jnp.zeros((row_chunk, LANES), F32) + bdw_ref[cb]
            for tap in range(CONV_K):
                acc = acc + ext_ref[cb, r0 + base + tap:r0 + base + tap + row_chunk, :] * wdw_ref[cb, tap:tap + 1, :]
            z_ref[cb, r0:r0 + row_chunk, :] = acc
        return carry

    lax.fori_loop(0, n_slab, slab, 0)
    z = jnp.concatenate([z_ref[cb] for cb in range(n_slab)], axis=-1)
    o_ref[...] = x_ref[...] + _ln_silu_pw2(z, lng_ref, lnb_ref, w2_ref, b2_ref)


def conv_block(glu, x, w_dw, b_dw, ln_g, ln_b, w_pw2, b_pw2, tm, seq):
    m, d = x.shape
    halo = 32
    n_slab = d // LANES
    assert seq % tm == 0 and tm % halo == 0
    kern = functools.partial(_conv_kernel, tiles_per_seq=seq // tm, row_chunk=64)
    vec = pl.BlockSpec((1, d), lambda i: (0, 0))
    w_dw_slab = jnp.transpose(w_dw.reshape(w_dw.shape[0], n_slab, LANES), (1, 0, 2))
    b_dw_slab = b_dw.reshape(n_slab, 1, LANES)
    return pl.pallas_call(
        kern,
        grid=(m // tm,),
        in_specs=[
            pl.BlockSpec((tm, d), lambda i: (i, 0)),
            pl.BlockSpec((halo, d), lambda i: (jnp.maximum(i * (tm // halo) - 1, 0), 0)),
            pl.BlockSpec((tm, d), lambda i: (i, 0)),
            pl.BlockSpec(w_dw_slab.shape, lambda i: (0, 0, 0)),
            pl.BlockSpec(b_dw_slab.shape, lambda i: (0, 0, 0)),
            vec, vec,
            pl.BlockSpec((d, d), lambda i: (0, 0)),
            vec,
        ],
        out_specs=pl.BlockSpec((tm, d), lambda i: (i, 0)),
        out_shape=jax.ShapeDtypeStruct((m, d), F32),
        scratch_shapes=[pltpu.VMEM((n_slab, tm + halo, LANES), F32), pltpu.VMEM((n_slab, tm, LANES), F32)],
        compiler_params=_params("parallel"),
        name="conv_block",
    )(glu, glu, x, w_dw_slab, b_dw_slab, ln_g, ln_b, w_pw2, b_pw2)


def _conv_sample_kernel(ext_ref, x_ref, wdw_ref, bdw_ref, lng_ref, lnb_ref, w2_ref, b2_ref, o_ref, z_ref,
                        *, steps):
    n = ext_ref.shape[0]
    z_ref[...] = jnp.zeros_like(z_ref)

    def one(s, carry):
        acc = jnp.zeros((steps, z_ref.shape[2]), F32) + bdw_ref[...]
        for tap in range(CONV_K):
            acc = acc + _round_bf16(ext_ref[s, tap:tap + steps, :]) * wdw_ref[tap:tap + 1, :]
        z_ref[s, 0:steps, :] = acc
        return carry

    lax.fori_loop(0, n, one, 0)
    z = z_ref[...].reshape(n * z_ref.shape[1], z_ref.shape[2])
    y = _ln_silu_pw2(z, lng_ref, lnb_ref, w2_ref, b2_ref)
    o_ref[...] = x_ref[...] + y.reshape(o_ref.shape)


def conv_block_sample(ext, x, w_dw, b_dw, ln_g, ln_b, w_pw2, b_pw2, steps):
    n, _, d = ext.shape
    kern = functools.partial(_conv_sample_kernel, steps=steps)
    return pl.pallas_call(
        kern,
        out_shape=jax.ShapeDtypeStruct((n, 8, d), F32),
        scratch_shapes=[pltpu.VMEM((n, 8, d), F32)],
        compiler_params=pltpu.CompilerParams(vmem_limit_bytes=VMEM_LIMIT_BYTES),
        name="conv_block_sample",
    )(ext, x, w_dw, b_dw, ln_g, ln_b, w_pw2, b_pw2)


def _sample_attn_kernel(q_ref, kvn_ref, c0_ref, c1_ref, c2_ref, b0_ref, b12_ref, bn_ref, ext_ref,
                        a_ref, mix_ref, *, steps):
    caches = (c0_ref, c1_ref, c2_ref)
    for t in range(steps):
        outs, lses = [], []
        for branch in range(N_BRANCH):
            q = _round_bf16(q_ref[0, t, branch]) * (A_HEAD_DIM ** -0.5)
            c_ref = caches[branch]
            res = min(t, c_ref.shape[2] - 1)
            kc = _round_bf16(c_ref[0, :, res, 0])
            vc = _round_bf16(c_ref[0, :, res, 1])
            kn = _round_bf16(kvn_ref[0, :, 0, branch])
            vn = _round_bf16(kvn_ref[0, :, 1, branch])
            bias = b0_ref[t] if branch == 0 else b12_ref[branch - 1]
            s_c = jnp.sum(kc * q[None], axis=-1, keepdims=True) + bias
            s_n = jnp.sum(kn * q[None], axis=-1, keepdims=True) + bn_ref[t, branch]
            mx = jnp.maximum(jnp.max(s_c, axis=0), jnp.max(s_n, axis=0))
            den = jnp.sum(jnp.exp(s_c - mx[None]), axis=0) + jnp.sum(jnp.exp(s_n - mx[None]), axis=0)
            lse = mx + jnp.log(den)
            p_c = _round_bf16(jnp.exp(s_c - lse[None]))
            p_n = _round_bf16(jnp.exp(s_n - lse[None]))
            outs.append(jnp.sum(p_c * vc, axis=0) + jnp.sum(p_n * vn, axis=0))
            lses.append(lse)
        mx = jnp.maximum(jnp.maximum(lses[0], lses[1]), lses[2])
        wts = [jnp.exp(lse - mx) for lse in lses]
        inv = 1.0 / (wts[0] + wts[1] + wts[2])
        a_ref[0, t] = outs[0] * (wts[0] * inv) + outs[1] * (wts[1] * inv) + outs[2] * (wts[2] * inv)

    base = ext_ref.shape[1] - steps
    for t in range(steps):
        for grp, win in enumerate(POOL_WINDOWS):
            cols = slice(grp * POOL_GROUP_DIM, (grp + 1) * POOL_GROUP_DIM)
            last = base + t
            tot = jnp.sum(ext_ref[0, last - win + 1:last + 1, cols], axis=0, keepdims=True)
            mix_ref[0, t:t + 1, cols] = tot / float(win) - ext_ref[0, last:last + 1, cols]


def sample_attention(q, kv_new, caches, b0, b12, bn, ext_pool):
    n, steps = q.shape[0], q.shape[1]
    hd = (A_HEADS, A_HEAD_DIM)
    in_specs = [
        pl.BlockSpec((1, steps, N_BRANCH) + hd, lambda i: (i, 0, 0, 0, 0)),
        pl.BlockSpec((1, steps, 2, N_BRANCH) + hd, lambda i: (i, 0, 0, 0, 0, 0)),
    ]
    for c in caches:
        res = min(c.shape[2], steps)
        in_specs.append(pl.BlockSpec((1, BAND, res, 2) + hd, lambda i: (i, 0, 0, 0, 0, 0)))
    in_specs += [
        pl.BlockSpec(b0.shape, lambda i: (0, 0, 0, 0)),
        pl.BlockSpec(b12.shape, lambda i: (0, 0, 0, 0)),
        pl.BlockSpec(bn.shape, lambda i: (0, 0, 0, 0, 0)),
        pl.BlockSpec((1,) + ext_pool.shape[1:], lambda i: (i, 0, 0)),
    ]
    kern = functools.partial(_sample_attn_kernel, steps=steps)
    return pl.pallas_call(
        kern,
        grid=(n,),
        in_specs=in_specs,
        out_specs=[
            pl.BlockSpec((1, steps) + hd, lambda i: (i, 0, 0, 0)),
            pl.BlockSpec((1, steps, POOL_WIDTH), lambda i: (i, 0, 0)),
        ],
        out_shape=[
            jax.ShapeDtypeStruct((n, steps) + hd, F32),
            jax.ShapeDtypeStruct((n, steps, POOL_WIDTH), F32),
        ],
        compiler_params=_params("parallel"),
        name="sample_attention",
    )(q, kv_new, *caches, b0, b12, bn, ext_pool)


def _t5_bucket(dist):
    max_exact = NUM_BUCKETS // 2
    d = jnp.maximum(dist, 1).astype(F32)
    large = max_exact + (jnp.log(d / max_exact) / math.log(REL_MAX_DIST / max_exact)
                         * (NUM_BUCKETS - max_exact)).astype(jnp.int32)
    return jnp.where(dist < max_exact, dist, jnp.minimum(large, NUM_BUCKETS - 1))


def _slot_bias(rel_bias):
    rb = rel_bias.astype(F32).reshape(NUM_BUCKETS, N_BRANCH, A_HEADS)
    slot = jnp.arange(BAND + 1)
    return jnp.stack([rb[_t5_bucket(slot * dil), g] for g, (_, dil) in enumerate(A_GROUPS)])


def _prompt_bias_tables(rel_bias):
    sb = _slot_bias(rel_bias)
    qi = jnp.arange(BAND)[:, None]
    kj = jnp.arange(BAND)[None, :]
    out = []
    for rel in (qi + BAND - kj, qi - kj):
        ok = (rel >= 0) & (rel <= BAND)
        tbl = sb[:, jnp.clip(rel, 0, BAND)]
        out.append(jnp.where(ok[None, :, :, None], tbl, NEG))
    return jnp.transpose(jnp.stack(out, axis=1), (0, 4, 1, 2, 3))


def _sample_bias_tables(rel_bias, steps):
    sb = _slot_bias(rel_bias)
    shape = (BAND, A_HEADS, A_HEAD_DIM)
    row = jnp.arange(BAND)
    b0 = []
    for t in range(steps):
        slot = BAND + t - row
        tbl = jnp.where((slot <= BAND)[:, None], sb[0, jnp.clip(slot, 0, BAND)], NEG)
        b0.append(jnp.broadcast_to(tbl[:, :, None], shape))
    b0 = jnp.stack(b0)
    b12 = jnp.stack([jnp.broadcast_to(sb[g, BAND - row][:, :, None], shape) for g in (1, 2)])
    bn = []
    for t in range(steps):
        per_branch = []
        for g, (_, dil) in enumerate(A_GROUPS):
            diff = t - jnp.arange(steps)
            ok = (diff >= 0) & (diff % dil == 0)
            tbl = jnp.where(ok[:, None], sb[g, jnp.clip(diff // dil, 0, BAND)], NEG)
            per_branch.append(jnp.broadcast_to(tbl[:, :, None], (steps, A_HEADS, A_HEAD_DIM)))
        bn.append(jnp.stack(per_branch))
    return b0, b12, jnp.stack(bn)


def kernel(x_prompt, x_sample, cache_kv_w128, cache_kv_w512, cache_kv_w2048, state_pool, state_conv, norm_mix0, w_in0, rel_bias, w_pool, pool_scale, w_out0, norm_ffn0, w_ff_gate, w_ff_up, w_ff_down, norm_mix1, w_pw1, b_pw1, w_dw, b_dw, ln_g, ln_b, w_pw2, b_pw2, norm_ffn1, w_router, we_gate, we_up, we_down, norm_final):
    batch, seq, d = x_prompt.shape
    n_s, steps, _ = x_sample.shape
    caches_in = (cache_kv_w128, cache_kv_w512, cache_kv_w2048)
    assert norm_mix0.shape[0] == 1 and norm_mix1.shape[0] == 1, "two layers: one mixer of each kind"
    assert seq % CHUNK == 0 and steps <= min(dil for _, dil in A_GROUPS[1:])
    for c, (window, _) in zip(caches_in, A_GROUPS):
        assert c.shape[2] == window, "cache must hold a full window"
    qkv_w = N_BRANCH * 3 * A_WIDTH

    w_in = w_in0[0]
    w_qkv = w_in[:, :qkv_w].reshape(d, 3, N_BRANCH, A_WIDTH)
    w_qkv_perm = jnp.transpose(w_qkv, (0, 2, 1, 3)).reshape(d, qkv_w).astype(BF16)
    w_kv = w_in[:, A_WIDTH * N_BRANCH:qkv_w].astype(BF16)
    w_u = w_in[:, qkv_w:].astype(BF16)
    w_in_bf = w_in.astype(BF16)
    g_mix0, g_ffn0 = norm_mix0, norm_ffn0
    g_mix1, g_ffn1 = norm_mix1, norm_ffn1
    g_final = norm_final.reshape(1, d)
    wpool_bf = w_pool[0].astype(BF16)
    w_out_a = w_out0[0, :A_WIDTH].astype(BF16)
    w_out_p = w_out0[0, A_WIDTH:].astype(BF16)
    wff = (w_ff_gate.astype(BF16), w_ff_up.astype(BF16), w_ff_down.astype(BF16))
    w_pw1_bf = w_pw1[0].astype(BF16)
    w_pw2_bf = w_pw2[0].astype(BF16)
    w_dw_pad = jnp.pad(w_dw[0], ((0, 32 - CONV_K), (0, 0)))
    wex = (we_gate[0].astype(BF16), we_up[0].astype(BF16), we_down[0].astype(BF16))
    w_router_pad = jnp.pad(w_router[0], ((0, 0), (0, LANES - N_EXPERTS))).astype(BF16)
    zeros = lambda n: jnp.zeros((1, n), F32)

    xp = x_prompt.reshape(batch * seq, d)
    qkv = qkv_perm(xp, g_mix0, w_qkv_perm)
    u_p = rms_matmul(xp, g_mix0, w_u, zeros(POOL_WIDTH), 1024, POOL_WIDTH, F32)
    a_p = band_attention(qkv, _prompt_bias_tables(rel_bias), batch, seq)
    h_p = mix_out(a_p, u_p, xp, wpool_bf, pool_scale, w_out_a, w_out_p, 512, seq)
    h_p = swiglu_block(h_p, g_ffn0, *wff, tm=512, tf=1408)
    glu_p = rms_glu(h_p, g_mix1, w_pw1_bf, b_pw1, 1024, 512)
    h_p = conv_block(glu_p, h_p, w_dw_pad, b_dw, ln_g, ln_b, w_pw2_bf, b_pw2, 256, seq)
    y_p = swiglu_block(h_p, g_ffn1, *wex, tm=1024, tf=896, w_router=w_router_pad, g_final=g_final)

    keep = A_GROUPS[-1][0]
    x_tail = x_prompt[:, seq - keep:].reshape(batch * keep, d)
    kv_tail = rms_matmul(x_tail, g_mix0, w_kv, zeros(w_kv.shape[1]), 1024, 512, F32)
    kv_tail = kv_tail.reshape(batch, keep, 2, N_BRANCH, A_HEADS, A_HEAD_DIM)
    kv_p = [kv_tail[:, keep - window:, :, g][None] for g, (window, _) in enumerate(A_GROUPS)]
    u_p3 = u_p.reshape(batch, seq, POOL_WIDTH)
    pool_p = u_p3[:, seq - POOL_BUF:][None]
    conv_p = glu_p.reshape(batch, seq, d)[:, seq - (CONV_K - 1):][None]

    m_s = n_s * steps
    xs = x_sample.reshape(m_s, d)
    proj = rms_matmul(xs, g_mix0, w_in_bf, zeros(w_in_bf.shape[1]), m_s, 512, F32)
    qkv_s = proj[:, :qkv_w].reshape(n_s, steps, 3, N_BRANCH, A_HEADS, A_HEAD_DIM)
    u_s = proj[:, qkv_w:].reshape(n_s, steps, POOL_WIDTH)
    q_s = qkv_s[:, :, 0]
    kv_new = qkv_s[:, :, 1:]
    cache_views = [c[0].reshape(n_s, BAND, dil, 2, A_HEADS, A_HEAD_DIM) for c, (_, dil) in zip(caches_in, A_GROUPS)]
    ext_pool = jnp.concatenate([state_pool[0], u_s], axis=1)
    b0, b12, bn = _sample_bias_tables(rel_bias, steps)
    a_s, mix_s = sample_attention(q_s, kv_new, cache_views, b0, b12, bn, ext_pool)
    h_s = mix_out_sample(a_s.reshape(m_s, A_WIDTH), mix_s.reshape(m_s, POOL_WIDTH), xs,
                         wpool_bf, pool_scale, w_out_a, w_out_p)
    h_s = swiglu_block(h_s, g_ffn0, *wff, tm=m_s, tf=1408)
    glu_s = rms_glu(h_s, g_mix1, w_pw1_bf, b_pw1, m_s, 512)
    ext_conv = jnp.concatenate([state_conv[0], glu_s.reshape(n_s, steps, d)], axis=1)
    ext_conv_pad = jnp.pad(ext_conv, ((0, 0), (0, 40 - ext_conv.shape[1]), (0, 0)))
    h_s3 = jnp.pad(h_s.reshape(n_s, steps, d), ((0, 0), (0, 8 - steps), (0, 0)))
    h_s = conv_block_sample(ext_conv_pad, h_s3, w_dw_pad, b_dw, ln_g, ln_b, w_pw2_bf, b_pw2, steps)
    h_s = h_s[:, :steps].reshape(m_s, d)
    y_s = swiglu_block(h_s, g_ffn1, *wex, tm=m_s, tf=896, w_router=w_router_pad, g_final=g_final)

    kv_s = []
    for g, c in enumerate(caches_in):
        new = kv_new[:, :, :, g]
        kv_s.append(jnp.concatenate([c[0, :, steps:], new], axis=1)[None])
    pool_s = ext_pool[:, steps:][None]
    conv_s = ext_conv[:, steps:][None]

    return (y_p.reshape(batch, seq, d), y_s.reshape(n_s, steps, d), kv_p[0], kv_p[1], kv_p[2], pool_p, conv_p,
            kv_s[0], kv_s[1], kv_s[2], pool_s, conv_s)
```

```python
import functools
import math

import jax
import jax.numpy as jnp
from jax import lax
from jax.experimental import pallas as pl
from jax.experimental.pallas import tpu as pltpu

F32 = jnp.float32
BF16 = jnp.bfloat16

A_GROUPS = ((128, 1), (512, 4), (2048, 16))
N_BRANCH = 3
A_HEADS = 8
A_HEAD_DIM = 64
A_WIDTH = 512
BAND = 128
NUM_BUCKETS = 32
REL_MAX_DIST = 2048
POOL_WINDOWS = (2, 4, 8, 16)
POOL_GROUP_DIM = 128
POOL_WIDTH = 512
POOL_BUF = 15
CONV_K = 31
N_EXPERTS = 8
RMS_EPS = 1e-6
LN_EPS = 1e-5
NEG = -1e30

VMEM_LIMIT_BYTES = 56 * 1024 * 1024
LANES = 128
CHUNK = 2048
GROUP = 4


def _params(*sem):
    return pltpu.CompilerParams(dimension_semantics=sem, vmem_limit_bytes=VMEM_LIMIT_BYTES)


def _rmsnorm(x, g):
    return x * lax.rsqrt(jnp.mean(x * x, axis=-1, keepdims=True) + RMS_EPS) * g


def _sigmoid(x):
    return 1.0 / (1.0 + jnp.exp(-x))


def _round_bf16(x):
    return x.astype(BF16).astype(F32)


def _rms_matmul_kernel(x_ref, g_ref, w_ref, b_ref, o_ref, hn_ref):
    @pl.when(pl.program_id(1) == 0)
    def _():
        hn_ref[...] = _rmsnorm(x_ref[...], g_ref[...]).astype(BF16)

    acc = jnp.dot(hn_ref[...], w_ref[...], preferred_element_type=F32)
    o_ref[...] = (acc + b_ref[...]).astype(o_ref.dtype)


def rms_matmul(x, g, w, b, tm, tn, out_dtype):
    m, k = x.shape
    n = w.shape[1]
    assert m % tm == 0 and n % tn == 0
    return pl.pallas_call(
        _rms_matmul_kernel,
        grid=(m // tm, n // tn),
        in_specs=[
            pl.BlockSpec((tm, k), lambda i, j: (i, 0)),
            pl.BlockSpec((1, k), lambda i, j: (0, 0)),
            pl.BlockSpec((k, tn), lambda i, j: (0, j)),
            pl.BlockSpec((1, tn), lambda i, j: (0, j)),
        ],
        out_specs=pl.BlockSpec((tm, tn), lambda i, j: (i, j)),
        out_shape=jax.ShapeDtypeStruct((m, n), out_dtype),
        scratch_shapes=[pltpu.VMEM((tm, k), BF16)],
        compiler_params=_params("parallel", "arbitrary"),
        name="rms_matmul",
    )(x, g, w, b)


def _rms_glu_kernel(x_ref, g_ref, wv_ref, wg_ref, bv_ref, bg_ref, o_ref, hn_ref):
    @pl.when(pl.program_id(1) == 0)
    def _():
        hn_ref[...] = _rmsnorm(x_ref[...], g_ref[...]).astype(BF16)

    hn = hn_ref[...]
    val = jnp.dot(hn, wv_ref[...], preferred_element_type=F32) + bv_ref[...]
    gate = jnp.dot(hn, wg_ref[...], preferred_element_type=F32) + bg_ref[...]
    o_ref[...] = val * _sigmoid(gate)


def rms_glu(x, g, w, b, tm, tn):
    m, k = x.shape
    c = w.shape[1] // 2
    nb = c // tn
    return pl.pallas_call(
        _rms_glu_kernel,
        grid=(m // tm, nb),
        in_specs=[
            pl.BlockSpec((tm, k), lambda i, j: (i, 0)),
            pl.BlockSpec((1, k), lambda i, j: (0, 0)),
            pl.BlockSpec((k, tn), lambda i, j: (0, j)),
            pl.BlockSpec((k, tn), lambda i, j: (0, j + nb)),
            pl.BlockSpec((1, tn), lambda i, j: (0, j)),
            pl.BlockSpec((1, tn), lambda i, j: (0, j + nb)),
        ],
        out_specs=pl.BlockSpec((tm, tn), lambda i, j: (i, j)),
        out_shape=jax.ShapeDtypeStruct((m, c), F32),
        scratch_shapes=[pltpu.VMEM((tm, k), BF16)],
        compiler_params=_params("parallel", "arbitrary"),
        name="rms_glu",
    )(x, g, w, w, b, b)


def _qkv_perm_kernel(x_ref, g_ref, w_ref, o_ref, hn_ref, slab_ref):
    j = pl.program_id(1)

    @pl.when(j == 0)
    def _():
        hn_ref[...] = _rmsnorm(x_ref[...], g_ref[...]).astype(BF16)

    res = jnp.dot(hn_ref[...], w_ref[...], preferred_element_type=F32)
    n_slab = res.shape[1] // LANES

    @pl.when(j < 3)
    def _():
        o_ref[...] = res.astype(BF16)

    for s in range(n_slab):
        slab_ref[s] = res[:, s * LANES:(s + 1) * LANES]

    for branch in (1, 2):
        dil = A_GROUPS[branch][1]
        rows = CHUNK // dil

        @pl.when(j // 3 == branch)
        def _():
            for r in range(dil):
                for s in range(n_slab):
                    v = slab_ref[s, pl.ds(r, rows, stride=dil), :]
                    o_ref[r * rows:(r + 1) * rows, s * LANES:(s + 1) * LANES] = v.astype(BF16)


def qkv_perm(x, g, w):
    m, k = x.shape
    n = w.shape[1]
    tn = A_WIDTH
    return pl.pallas_call(
        _qkv_perm_kernel,
        grid=(m // CHUNK, n // tn),
        in_specs=[
            pl.BlockSpec((CHUNK, k), lambda i, j: (i, 0)),
            pl.BlockSpec((1, k), lambda i, j: (0, 0)),
            pl.BlockSpec((k, tn), lambda i, j: (0, j)),
        ],
        out_specs=pl.BlockSpec((CHUNK, tn), lambda i, j: (i, j)),
        out_shape=jax.ShapeDtypeStruct((m, n), BF16),
        scratch_shapes=[pltpu.VMEM((CHUNK, k), BF16), pltpu.VMEM((tn // LANES, CHUNK, LANES), F32)],
        compiler_params=_params("parallel", "arbitrary"),
        name="qkv_perm",
    )(x, g, w)


def _band_attn_kernel(*refs):
    qkv_refs = refs[:15]
    bias_ref, o_ref, acc_ref, lse_ref, kp_ref, vp_ref = refs[15:]
    c = pl.program_id(1)
    n_units = CHUNK // BAND
    grp_rows = GROUP * BAND
    head_of_lane = lax.broadcasted_iota(jnp.int32, (1, 1, LANES), 2) // A_HEAD_DIM
    key_is_prev = lax.broadcasted_iota(jnp.int32, (1, 1, 2 * BAND), 2) < BAND

    for branch, (_, dil) in enumerate(A_GROUPS):
        q_ref, kprev_ref, kcur_ref, vprev_ref, vcur_ref = qkv_refs[5 * branch:5 * branch + 5]
        nblk = n_units // dil
        span = nblk * BAND
        if nblk == 1:
            kp_src, vp_src = kprev_ref, vprev_ref
        else:
            kp_src, vp_src = kp_ref.at[branch], vp_ref.at[branch]
            for res in range(dil):
                lo = res * span
                for dst, prev, cur in ((kp_src, kprev_ref, kcur_ref), (vp_src, vprev_ref, vcur_ref)):
                    dst[lo:lo + BAND, :] = prev[lo + span - BAND:lo + span, :]
                    dst[lo + BAND:lo + span, :] = cur[lo:lo + span - BAND, :]

        def group(gi, carry, q_ref=q_ref, kcur_ref=kcur_ref, vcur_ref=vcur_ref, kp_src=kp_src, vp_src=vp_src,
                  nblk=nblk, dil=dil, branch=branch):
            row = pl.multiple_of(gi * grp_rows, grp_rows)

            def blocks(ref):
                return ref[pl.ds(row, grp_rows), :].reshape(GROUP, BAND, LANES)

            q = blocks(q_ref) * jnp.asarray(A_HEAD_DIM ** -0.5, BF16)
            kcat = jnp.concatenate([blocks(kp_src), blocks(kcur_ref)], axis=1)
            vcat = jnp.concatenate([blocks(vp_src), blocks(vcur_ref)], axis=1)
            unit = gi * GROUP + lax.broadcasted_iota(jnp.int32, (GROUP, 1, 1), 0)
            no_prev = jnp.logical_and(unit % nblk == 0, c == 0)
            pen = jnp.where(jnp.logical_and(no_prev, key_is_prev), NEG, 0.0)

            o = jnp.zeros((GROUP, BAND, LANES), F32)
            lse = jnp.zeros((GROUP, BAND, LANES), F32)
            for hh in range(2):
                mine = head_of_lane == hh
                kh = jnp.where(mine, kcat, jnp.zeros_like(kcat))
                vh = jnp.where(mine, vcat, jnp.zeros_like(vcat))
                s = jnp.einsum("uqd,ukd->uqk", q, kh, preferred_element_type=F32) + bias_ref[branch, hh][None] + pen
                mx = jnp.max(s, axis=-1, keepdims=True)
                p = jnp.exp(s - mx)
                den = jnp.sum(p, axis=-1, keepdims=True)
                pn = (p * (1.0 / den)).astype(BF16)
                o = o + jnp.einsum("uqk,ukd->uqd", pn, vh, preferred_element_type=F32)
                lse = jnp.where(mine, mx + jnp.log(den), lse)

            if dil == 1:
                acc_ref[branch, pl.ds(row, grp_rows), :] = o.reshape(grp_rows, LANES)
                lse_ref[branch, pl.ds(row, grp_rows), :] = lse.reshape(grp_rows, LANES)
            else:
                per = max(GROUP // nblk, 1)
                blocks_per = GROUP // per
                for k in range(per):
                    if nblk >= GROUP:
                        start = gi // (nblk // GROUP) + dil * BAND * ((gi % (nblk // GROUP)) * GROUP)
                    else:
                        start = gi * per + k
                    rows = pl.ds(start, blocks_per * BAND, stride=dil)
                    part = slice(k * blocks_per, (k + 1) * blocks_per)
                    acc_ref[branch, rows, :] = o[part].reshape(blocks_per * BAND, LANES)
                    lse_ref[branch, rows, :] = lse[part].reshape(blocks_per * BAND, LANES)
            return carry

        lax.fori_loop(0, n_units // GROUP, group, 0)

    l0, l1, l2 = lse_ref[0], lse_ref[1], lse_ref[2]
    mx = jnp.maximum(jnp.maximum(l0, l1), l2)
    w0, w1, w2 = jnp.exp(l0 - mx), jnp.exp(l1 - mx), jnp.exp(l2 - mx)
    inv = 1.0 / (w0 + w1 + w2)
    o_ref[...] = (acc_ref[0] * (w0 * inv) + acc_ref[1] * (w1 * inv) + acc_ref[2] * (w2 * inv)).astype(o_ref.dtype)


def band_attention(qkv, bias_tbl, batch, seq):
    n_chunk = seq // CHUNK
    n_pair = A_WIDTH // LANES
    blk = (CHUNK, LANES)
    in_specs = []
    for branch in range(N_BRANCH):
        qcol, kcol, vcol = [(3 * branch + kind) * n_pair for kind in range(3)]

        def cur(col):
            return lambda b, c, hp, col=col: (b * n_chunk + c, col + hp)

        def prev(col):
            return lambda b, c, hp, col=col: (b * n_chunk + jnp.maximum(c - 1, 0), col + hp)

        in_specs += [pl.BlockSpec(blk, cur(qcol)), pl.BlockSpec(blk, prev(kcol)), pl.BlockSpec(blk, cur(kcol)),
                     pl.BlockSpec(blk, prev(vcol)), pl.BlockSpec(blk, cur(vcol))]
    in_specs.append(pl.BlockSpec((N_BRANCH, 2, BAND, 2 * BAND), lambda b, c, hp: (0, hp, 0, 0)))
    return pl.pallas_call(
        _band_attn_kernel,
        grid=(batch, n_chunk, n_pair),
        in_specs=in_specs,
        out_specs=pl.BlockSpec(blk, lambda b, c, hp: (b * n_chunk + c, hp)),
        out_shape=jax.ShapeDtypeStruct((batch * seq, A_WIDTH), BF16),
        scratch_shapes=[pltpu.VMEM((N_BRANCH, CHUNK, LANES), F32)] * 2
        + [pltpu.VMEM((N_BRANCH - 1, CHUNK, LANES), BF16)] * 2,
        compiler_params=_params("parallel", "parallel", "arbitrary"),
        name="band_attention",
    )(*([qkv] * 15), bias_tbl)


def _mix_out_kernel(a_ref, u_ref, halo_ref, x_ref, wpool_ref, scale_ref, woa_ref, wop_ref, o_ref, ext_ref,
                    *, tiles_per_seq):
    i = pl.program_id(0)
    tm = a_ref.shape[0]
    halo = halo_ref.shape[0]
    seq_tile = i % tiles_per_seq
    ext_ref[0:halo, :] = jnp.where(seq_tile == 0, 0.0, halo_ref[...])
    ext_ref[halo:halo + tm, :] = u_ref[...]
    pos = seq_tile * tm + lax.broadcasted_iota(jnp.int32, (tm, POOL_GROUP_DIM), 0)
    acc = x_ref[...] + jnp.dot(a_ref[...], woa_ref[...], preferred_element_type=F32)
    for grp, win in enumerate(POOL_WINDOWS):
        cols = slice(grp * POOL_GROUP_DIM, (grp + 1) * POOL_GROUP_DIM)
        tot = ext_ref[halo:halo + tm, cols]
        for back in range(1, win):
            tot = tot + ext_ref[halo - back:halo - back + tm, cols]
        cnt = jnp.minimum(pos + 1, win).astype(F32)
        mix = tot / cnt - ext_ref[halo:halo + tm, cols]
        y = jnp.dot(mix.astype(BF16), wpool_ref[grp], preferred_element_type=F32) * scale_ref[:, cols]
        acc = acc + jnp.dot(y.astype(BF16), wop_ref[cols, :], preferred_element_type=F32)
    o_ref[...] = acc


def mix_out(a, u, x, w_pool, pool_scale, w_out_a, w_out_p, tm, seq):
    m, d = x.shape
    halo = 16
    assert seq % tm == 0 and tm % halo == 0
    kern = functools.partial(_mix_out_kernel, tiles_per_seq=seq // tm)
    return pl.pallas_call(
        kern,
        grid=(m // tm,),
        in_specs=[
            pl.BlockSpec((tm, A_WIDTH), lambda i: (i, 0)),
            pl.BlockSpec((tm, POOL_WIDTH), lambda i: (i, 0)),
            pl.BlockSpec((halo, POOL_WIDTH), lambda i: (jnp.maximum(i * (tm // halo) - 1, 0), 0)),
            pl.BlockSpec((tm, d), lambda i: (i, 0)),
            pl.BlockSpec((len(POOL_WINDOWS), POOL_GROUP_DIM, POOL_GROUP_DIM), lambda i: (0, 0, 0)),
            pl.BlockSpec((1, POOL_WIDTH), lambda i: (0, 0)),
            pl.BlockSpec((A_WIDTH, d), lambda i: (0, 0)),
            pl.BlockSpec((POOL_WIDTH, d), lambda i: (0, 0)),
        ],
        out_specs=pl.BlockSpec((tm, d), lambda i: (i, 0)),
        out_shape=jax.ShapeDtypeStruct((m, d), F32),
        scratch_shapes=[pltpu.VMEM((tm + halo, POOL_WIDTH), F32)],
        compiler_params=_params("parallel"),
        name="mix_out",
    )(a, u, u, x, w_pool, pool_scale, w_out_a, w_out_p)


def _mix_out_sample_kernel(a_ref, mix_ref, x_ref, wpool_ref, scale_ref, woa_ref, wop_ref, o_ref):
    acc = x_ref[...] + jnp.dot(a_ref[...].astype(BF16), woa_ref[...], preferred_element_type=F32)
    for grp in range(len(POOL_WINDOWS)):
        cols = slice(grp * POOL_GROUP_DIM, (grp + 1) * POOL_GROUP_DIM)
        y = jnp.dot(mix_ref[:, cols].astype(BF16), wpool_ref[grp], preferred_element_type=F32) * scale_ref[:, cols]
        acc = acc + jnp.dot(y.astype(BF16), wop_ref[cols, :], preferred_element_type=F32)
    o_ref[...] = acc


def mix_out_sample(a, mix, x, w_pool, pool_scale, w_out_a, w_out_p):
    m, d = x.shape
    return pl.pallas_call(
        _mix_out_sample_kernel,
        out_shape=jax.ShapeDtypeStruct((m, d), F32),
        compiler_params=pltpu.CompilerParams(vmem_limit_bytes=VMEM_LIMIT_BYTES),
        name="mix_out_sample",
    )(a, mix, x, w_pool, pool_scale, w_out_a, w_out_p)


def _top2(logits):
    lane = lax.broadcasted_iota(jnp.int32, logits.shape, 1)
    lg = jnp.where(lane < N_EXPERTS, logits, -jnp.inf)
    m1 = jnp.max(lg, axis=-1, keepdims=True)
    i1 = jnp.min(jnp.where(lg == m1, lane, LANES), axis=-1, keepdims=True)
    rest = jnp.where(lane == i1, -jnp.inf, lg)
    m2 = jnp.max(rest, axis=-1, keepdims=True)
    i2 = jnp.min(jnp.where(rest == m2, lane, LANES), axis=-1, keepdims=True)
    e2 = jnp.exp(m2 - m1)
    return lane, i1, i2, 1.0 / (1.0 + e2), e2 / (1.0 + e2)


def _top2_gates(logits):
    lane, i1, i2, g1, g2 = _top2(logits)
    return jnp.where(lane == i1, g1, 0.0) + jnp.where(lane == i2, g2, 0.0)


ROUTE_E1, ROUTE_E2, ROUTE_R1, ROUTE_R2, ROUTE_G1, ROUTE_G2 = range(6)


def _router_kernel(x_ref, g_ref, wr_ref, tri_ref, o_ref):
    hn = _rmsnorm(x_ref[...], g_ref[...]).astype(BF16)
    logits = jnp.dot(hn, wr_ref[...], preferred_element_type=F32)
    lane, i1, i2, g1, g2 = _top2(logits)
    chosen = jnp.where(jnp.logical_or(lane == i1, lane == i2), 1.0, 0.0).astype(BF16)
    before = jnp.dot(tri_ref[...], chosen, preferred_element_type=F32)
    r1 = jnp.sum(jnp.where(lane == i1, before, 0.0), axis=-1, keepdims=True)
    r2 = jnp.sum(jnp.where(lane == i2, before, 0.0), axis=-1, keepdims=True)
    rec = jnp.zeros(logits.shape, F32)
    for pos, val in ((ROUTE_E1, i1.astype(F32)), (ROUTE_E2, i2.astype(F32)), (ROUTE_R1, r1), (ROUTE_R2, r2),
                     (ROUTE_G1, g1), (ROUTE_G2, g2)):
        rec = jnp.where(lane == pos, val, rec)
    o_ref[...] = rec


def route_tokens(x, g, w_router, tm):
    m, d = x.shape
    tri = (jnp.arange(tm)[:, None] > jnp.arange(tm)[None, :]).astype(BF16)
    return pl.pallas_call(
        _router_kernel,
        grid=(m // tm,),
        in_specs=[
            pl.BlockSpec((tm, d), lambda i: (i, 0)),
            pl.BlockSpec((1, d), lambda i: (0, 0)),
            pl.BlockSpec((d, LANES), lambda i: (0, 0)),
            pl.BlockSpec((tm, tm), lambda i: (0, 0)),
        ],
        out_specs=pl.BlockSpec((tm, LANES), lambda i: (i, 0)),
        out_shape=jax.ShapeDtypeStruct((m, LANES), F32),
        compiler_params=_params("parallel"),
        name="route_tokens",
    )(x, g, w_router, tri)


def _row_copy(src, src_row, dst, dst_row, sem):
    return pltpu.make_async_copy(src.at[pl.ds(src_row, 1)], dst.at[pl.ds(dst_row, 1)], sem)


def _scatter_rows_kernel(dest_ref, x_ref, o_hbm, sem):
    tm = x_ref.shape[0]

    def issue(r, carry):
        for k in range(2):
            _row_copy(x_ref, r, o_hbm, dest_ref[0, 0, 2 * r + k], sem).start()
        return carry

    def drain(r, carry):
        for k in range(2):
            _row_copy(x_ref, 0, o_hbm, 0, sem).wait()
        return carry

    lax.fori_loop(0, tm, issue, 0, unroll=8)
    lax.fori_loop(0, tm, drain, 0, unroll=8)


def scatter_rows(x, dest, tm):
    m, d = x.shape
    dest3 = dest.reshape(m // tm, 1, 2 * tm)
    return pl.pallas_call(
        _scatter_rows_kernel,
        grid=(m // tm,),
        in_specs=[
            pl.BlockSpec((1, 1, 2 * tm), lambda i: (i, 0, 0), memory_space=pltpu.SMEM),
            pl.BlockSpec((tm, d), lambda i: (i, 0)),
        ],
        out_specs=pl.BlockSpec(memory_space=pl.ANY),
        out_shape=jax.ShapeDtypeStruct((2 * m, d), x.dtype),
        scratch_shapes=[pltpu.SemaphoreType.DMA(())],
        compiler_params=_params("arbitrary"),
        name="scatter_rows",
    )(dest3, x)


def _expert_kernel(tile_ref, exp_ref, lo_ref, hi_ref, nvis_ref, x_ref, g_ref, wg_ref, wu_ref, wd_ref, o_ref,
                   hn_ref, acc_ref):
    v = pl.program_id(0)
    j = pl.program_id(1)

    @pl.when(v < nvis_ref[0])
    def _():
        @pl.when(j == 0)
        def _():
            hn_ref[...] = _rmsnorm(x_ref[...], g_ref[...]).astype(BF16)
            acc_ref[...] = jnp.zeros_like(acc_ref)

        hn = hn_ref[...]
        a = jnp.dot(hn, wg_ref[...], preferred_element_type=F32)
        b = jnp.dot(hn, wu_ref[...], preferred_element_type=F32)
        h = (a * _sigmoid(a) * b).astype(BF16)
        acc_ref[...] += jnp.dot(h, wd_ref[...], preferred_element_type=F32)

        @pl.when(j == pl.num_programs(1) - 1)
        def _():
            row = lax.broadcasted_iota(jnp.int32, (o_ref.shape[0], 1), 0)
            mine = jnp.logical_and(row >= lo_ref[v], row < hi_ref[v])
            first = jnp.logical_or(v == 0, tile_ref[v] != tile_ref[jnp.maximum(v - 1, 0)])

            @pl.when(first)
            def _():
                o_ref[...] = jnp.where(mine, acc_ref[...], 0.0)

            @pl.when(jnp.logical_not(first))
            def _():
                o_ref[...] = jnp.where(mine, acc_ref[...], o_ref[...])


def expert_swiglu(xs, g, w_gate, w_up, w_down, visits, tm, tf):
    m, d = xs.shape
    f = w_gate.shape[2]
    n_f = f // tf
    assert m % tm == 0 and f % tf == 0 and n_f == 2
    tile, expert, lo, hi, n_visit = visits

    def f_blk(v, j):
        return j + (v % 2) * (1 - 2 * j)

    grid_spec = pltpu.PrefetchScalarGridSpec(
        num_scalar_prefetch=5,
        grid=(tile.shape[0], n_f),
        in_specs=[
            pl.BlockSpec((tm, d), lambda v, j, t, e, lo, hi, n: (t[v], 0)),
            pl.BlockSpec((1, d), lambda v, j, t, e, lo, hi, n: (0, 0)),
            pl.BlockSpec((None, d, tf), lambda v, j, t, e, lo, hi, n: (e[v], 0, f_blk(v, j))),
            pl.BlockSpec((None, d, tf), lambda v, j, t, e, lo, hi, n: (e[v], 0, f_blk(v, j))),
            pl.BlockSpec((None, tf, d), lambda v, j, t, e, lo, hi, n: (e[v], f_blk(v, j), 0)),
        ],
        out_specs=pl.BlockSpec((tm, d), lambda v, j, t, e, lo, hi, n: (t[v], 0)),
        scratch_shapes=[pltpu.VMEM((tm, d), BF16), pltpu.VMEM((tm, d), F32)],
    )
    return pl.pallas_call(
        _expert_kernel,
        grid_spec=grid_spec,
        out_shape=jax.ShapeDtypeStruct((m, d), F32),
        compiler_params=_params("arbitrary", "arbitrary"),
        name="expert_swiglu",
    )(tile, expert, lo, hi, n_visit, xs, g, w_gate, w_up, w_down)


def _combine_kernel(dest_ref, x_ref, rec_ref, gf_ref, ys_hbm, o_ref, y1_ref, y2_ref, sem):
    tm = x_ref.shape[0]
    bufs = (y1_ref, y2_ref)

    def issue(r, carry):
        for k in range(2):
            _row_copy(ys_hbm, dest_ref[0, 0, 2 * r + k], bufs[k], r, sem).start()
        return carry

    def drain(r, carry):
        for k in range(2):
            _row_copy(ys_hbm, 0, bufs[k], 0, sem).wait()
        return carry

    lax.fori_loop(0, tm, issue, 0, unroll=8)
    lax.fori_loop(0, tm, drain, 0, unroll=8)
    g1 = rec_ref[:, ROUTE_G1:ROUTE_G1 + 1]
    g2 = rec_ref[:, ROUTE_G2:ROUTE_G2 + 1]
    out = x_ref[...] + (g1 * y1_ref[...] + g2 * y2_ref[...])
    o_ref[...] = _rmsnorm(out, gf_ref[...])


def combine_rows(x, rec, dest, ys, g_final, tm):
    m, d = x.shape
    dest3 = dest.reshape(m // tm, 1, 2 * tm)
    return pl.pallas_call(
        _combine_kernel,
        grid=(m // tm,),
        in_specs=[
            pl.BlockSpec((1, 1, 2 * tm), lambda i: (i, 0, 0), memory_space=pltpu.SMEM),
            pl.BlockSpec((tm, d), lambda i: (i, 0)),
            pl.BlockSpec((tm, LANES), lambda i: (i, 0)),
            pl.BlockSpec((1, d), lambda i: (0, 0)),
            pl.BlockSpec(memory_space=pl.ANY),
        ],
        out_specs=pl.BlockSpec((tm, d), lambda i: (i, 0)),
        out_shape=jax.ShapeDtypeStruct((m, d), F32),
        scratch_shapes=[pltpu.VMEM((tm, d), F32), pltpu.VMEM((tm, d), F32), pltpu.SemaphoreType.DMA(())],
        compiler_params=_params("arbitrary"),
        name="combine_rows",
    )(dest3, x, rec, g_final, ys)


def _visit_list(rec, tile_tokens, tm):
    n = rec.shape[0]
    e = rec[:, ROUTE_E1:ROUTE_E2 + 1].astype(jnp.int32)
    rank = rec[:, ROUTE_R1:ROUTE_R2 + 1].astype(jnp.int32)
    onehot = (e[:, :, None] == jnp.arange(N_EXPERTS)[None, None, :]).astype(jnp.int32)
    per_tile = onehot.reshape(n // tile_tokens, tile_tokens * 2, N_EXPERTS).sum(axis=1)
    count = per_tile.sum(axis=0)
    start = jnp.cumsum(count) - count
    tile_base = jnp.cumsum(per_tile, axis=0) - per_tile
    base = (start[None, :] + tile_base)[:, None, :]
    dest = rank + jnp.sum(onehot.reshape(n // tile_tokens, tile_tokens * 2, N_EXPERTS) * base, axis=-1).reshape(n, 2)

    n_tile = 2 * n // tm
    n_visit_max = n_tile + N_EXPERTS
    end = start + count
    first_tile = start // tm
    last_tile = jnp.maximum(end - 1, start) // tm
    n_vis = jnp.where(count > 0, last_tile - first_tile + 1, 0)
    vis_end = jnp.cumsum(n_vis)
    vis_start = vis_end - n_vis
    total = vis_end[-1]
    v = jnp.arange(n_visit_max)
    vc = jnp.minimum(v, total - 1)
    ex = jnp.sum(vc[:, None] >= vis_end[None, :], axis=1)
    tile = first_tile[ex] + vc - vis_start[ex]
    lo = jnp.clip(start[ex] - tile * tm, 0, tm)
    hi = jnp.clip(end[ex] - tile * tm, 0, tm)
    live = v < total
    as_i32 = lambda a: a.astype(jnp.int32)
    visits = (as_i32(tile), as_i32(ex), as_i32(jnp.where(live, lo, 0)), as_i32(jnp.where(live, hi, 0)),
              as_i32(total).reshape(1))
    return dest.astype(jnp.int32), visits


def routed_experts(x, g, w_router, w_gate, w_up, w_down, g_final):
    rec = route_tokens(x, g, w_router, 1024)
    dest, visits = _visit_list(rec, 1024, 512)
    xs = scatter_rows(x, dest, 512)
    ys = expert_swiglu(xs, g, w_gate, w_up, w_down, visits, 512, w_gate.shape[2] // 2)
    return combine_rows(x, rec, dest, ys, g_final, 256)


def _swiglu_kernel(*refs, n_expert, final_norm):
    x_ref, g_ref = refs[0], refs[1]
    k = 2
    if n_expert > 1:
        wr_ref = refs[k]
        k += 1
    wg_ref, wu_ref, wd_ref = refs[k:k + 3]
    k += 3
    if final_norm:
        gf_ref = refs[k]
        k += 1
    o_ref, hn_ref, acc_ref = refs[k:k + 3]
    if n_expert > 1:
        gate_ref = refs[k + 3]
    e = pl.program_id(1)
    j = pl.program_id(2)

    @pl.when(jnp.logical_and(e == 0, j == 0))
    def _():
        hn = _rmsnorm(x_ref[...], g_ref[...])
        hn_ref[...] = hn.astype(BF16)
        acc_ref[...] = jnp.zeros_like(acc_ref)
        if n_expert > 1:
            logits = jnp.dot(hn.astype(BF16), wr_ref[...], preferred_element_type=F32)
            gate_ref[...] = _top2_gates(logits)

    hn = hn_ref[...]
    a = jnp.dot(hn, wg_ref[...], preferred_element_type=F32)
    b = jnp.dot(hn, wu_ref[...], preferred_element_type=F32)
    h = (a * _sigmoid(a) * b).astype(BF16)
    y = jnp.dot(h, wd_ref[...], preferred_element_type=F32)
    if n_expert > 1:
        lane = lax.broadcasted_iota(jnp.int32, gate_ref.shape, 1)
        gate = jnp.sum(jnp.where(lane == e, gate_ref[...], 0.0), axis=-1, keepdims=True)
        y = gate * y
    acc_ref[...] += y

    @pl.when(jnp.logical_and(e == n_expert - 1, j == pl.num_programs(2) - 1))
    def _():
        out = x_ref[...] + acc_ref[...]
        if final_norm:
            out = _rmsnorm(out, gf_ref[...])
        o_ref[...] = out


def swiglu_block(x, g, w_gate, w_up, w_down, tm, tf, w_router=None, g_final=None):
    m, d = x.shape
    n_expert, _, f = w_gate.shape
    assert m % tm == 0 and f % tf == 0
    args = [x, g]
    in_specs = [pl.BlockSpec((tm, d), lambda i, e, j: (i, 0)), pl.BlockSpec((1, d), lambda i, e, j: (0, 0))]
    if n_expert > 1:
        args.append(w_router)
        in_specs.append(pl.BlockSpec((d, LANES), lambda i, e, j: (0, 0)))
    args += [w_gate, w_up, w_down]
    in_specs += [
        pl.BlockSpec((None, d, tf), lambda i, e, j: (e, 0, j)),
        pl.BlockSpec((None, d, tf), lambda i, e, j: (e, 0, j)),
        pl.BlockSpec((None, tf, d), lambda i, e, j: (e, j, 0)),
    ]
    if g_final is not None:
        args.append(g_final)
        in_specs.append(pl.BlockSpec((1, d), lambda i, e, j: (0, 0)))
    scratch = [pltpu.VMEM((tm, d), BF16), pltpu.VMEM((tm, d), F32)]
    if n_expert > 1:
        scratch.append(pltpu.VMEM((tm, LANES), F32))
    kern = functools.partial(_swiglu_kernel, n_expert=n_expert, final_norm=g_final is not None)
    return pl.pallas_call(
        kern,
        grid=(m // tm, n_expert, f // tf),
        in_specs=in_specs,
        out_specs=pl.BlockSpec((tm, d), lambda i, e, j: (i, 0)),
        out_shape=jax.ShapeDtypeStruct((m, d), F32),
        scratch_shapes=scratch,
        compiler_params=_params("parallel", "arbitrary", "arbitrary"),
        name="swiglu_block",
    )(*args)


def _ln_silu_pw2(z, lng_ref, lnb_ref, w2_ref, b2_ref):
    mu = jnp.mean(z, axis=-1, keepdims=True)
    zc = z - mu
    var = jnp.mean(zc * zc, axis=-1, keepdims=True)
    zn = zc * lax.rsqrt(var + LN_EPS) * lng_ref[...] + lnb_ref[...]
    act = (zn * _sigmoid(zn)).astype(BF16)
    return jnp.dot(act, w2_ref[...], preferred_element_type=F32) + b2_ref[...]


def _conv_kernel(glu_ref, halo_ref, x_ref, wdw_ref, bdw_ref, lng_ref, lnb_ref, w2_ref, b2_ref, o_ref,
                 ext_ref, z_ref, *, tiles_per_seq, row_chunk):
    i = pl.program_id(0)
    tm = glu_ref.shape[0]
    halo = halo_ref.shape[0]
    n_slab = glu_ref.shape[1] // LANES
    at_start = i % tiles_per_seq == 0
    for cb in range(n_slab):
        cols = slice(cb * LANES, (cb + 1) * LANES)
        ext_ref[cb, 0:halo, :] = jnp.where(at_start, 0.0, _round_bf16(halo_ref[:, cols]))
        ext_ref[cb, halo:halo + tm, :] = _round_bf16(glu_ref[:, cols])
    base = halo - (CONV_K - 1)

    def slab(cb, carry):
        for rc in range(tm // row_chunk):
            r0 = rc * row_chunk
            acc = jnp.zeros((row_chunk, LANES), F32) + bdw_ref[cb]
            for tap in range(CONV_K):
                acc = acc + ext_ref[cb, r0 + base + tap:r0 + base + tap + row_chunk, :] * wdw_ref[cb, tap:tap + 1, :]
            z_ref[cb, r0:r0 + row_chunk, :] = acc
        return carry

    lax.fori_loop(0, n_slab, slab, 0)
    z = jnp.concatenate([z_ref[cb] for cb in range(n_slab)], axis=-1)
    o_ref[...] = x_ref[...] + _ln_silu_pw2(z, lng_ref, lnb_ref, w2_ref, b2_ref)


def conv_block(glu, x, w_dw, b_dw, ln_g, ln_b, w_pw2, b_pw2, tm, seq):
    m, d = x.shape
    halo = 32
    n_slab = d // LANES
    assert seq % tm == 0 and tm % halo == 0
    kern = functools.partial(_conv_kernel, tiles_per_seq=seq // tm, row_chunk=64)
    vec = pl.BlockSpec((1, d), lambda i: (0, 0))
    w_dw_slab = jnp.transpose(w_dw.reshape(w_dw.shape[0], n_slab, LANES), (1, 0, 2))
    b_dw_slab = b_dw.reshape(n_slab, 1, LANES)
    return pl.pallas_call(
        kern,
        grid=(m // tm,),
        in_specs=[
            pl.BlockSpec((tm, d), lambda i: (i, 0)),
            pl.BlockSpec((halo, d), lambda i: (jnp.maximum(i * (tm // halo) - 1, 0), 0)),
            pl.BlockSpec((tm, d), lambda i: (i, 0)),
            pl.BlockSpec(w_dw_slab.shape, lambda i: (0, 0, 0)),
            pl.BlockSpec(b_dw_slab.shape, lambda i: (0, 0, 0)),
            vec, vec,
            pl.BlockSpec((d, d), lambda i: (0, 0)),
            vec,
        ],
        out_specs=pl.BlockSpec((tm, d), lambda i: (i, 0)),
        out_shape=jax.ShapeDtypeStruct((m, d), F32),
        scratch_shapes=[pltpu.VMEM((n_slab, tm + halo, LANES), F32), pltpu.VMEM((n_slab, tm, LANES), F32)],
        compiler_params=_params("parallel"),
        name="conv_block",
    )(glu, glu, x, w_dw_slab, b_dw_slab, ln_g, ln_b, w_pw2, b_pw2)


def _conv_sample_kernel(ext_ref, x_ref, wdw_ref, bdw_ref, lng_ref, lnb_ref, w2_ref, b2_ref, o_ref, z_ref,
                        *, steps):
    n = ext_ref.shape[0]
    z_ref[...] = jnp.zeros_like(z_ref)

    def one(s, carry):
        acc = jnp.zeros((steps, z_ref.shape[2]), F32) + bdw_ref[...]
        for tap in range(CONV_K):
            acc = acc + _round_bf16(ext_ref[s, tap:tap + steps, :]) * wdw_ref[tap:tap + 1, :]
        z_ref[s, 0:steps, :] = acc
        return carry

    lax.fori_loop(0, n, one, 0)
    z = z_ref[...].reshape(n * z_ref.shape[1], z_ref.shape[2])
    y = _ln_silu_pw2(z, lng_ref, lnb_ref, w2_ref, b2_ref)
    o_ref[...] = x_ref[...] + y.reshape(o_ref.shape)


def conv_block_sample(ext, x, w_dw, b_dw, ln_g, ln_b, w_pw2, b_pw2, steps):
    n, _, d = ext.shape
    kern = functools.partial(_conv_sample_kernel, steps=steps)
    return pl.pallas_call(
        kern,
        out_shape=jax.ShapeDtypeStruct((n, 8, d), F32),
        scratch_shapes=[pltpu.VMEM((n, 8, d), F32)],
        compiler_params=pltpu.CompilerParams(vmem_limit_bytes=VMEM_LIMIT_BYTES),
        name="conv_block_sample",
    )(ext, x, w_dw, b_dw, ln_g, ln_b, w_pw2, b_pw2)


def _sample_attn_kernel(q_ref, kvn_ref, c0_ref, c1_ref, c2_ref, b0_ref, b12_ref, bn_ref, ext_ref,
                        a_ref, mix_ref, *, steps):
    caches = (c0_ref, c1_ref, c2_ref)
    for t in range(steps):
        outs, lses = [], []
        for branch in range(N_BRANCH):
            q = _round_bf16(q_ref[0, t, branch]) * (A_HEAD_DIM ** -0.5)
            c_ref = caches[branch]
            res = min(t, c_ref.shape[2] - 1)
            kc = _round_bf16(c_ref[0, :, res, 0])
            vc = _round_bf16(c_ref[0, :, res, 1])
            kn = _round_bf16(kvn_ref[0, :, 0, branch])
            vn = _round_bf16(kvn_ref[0, :, 1, branch])
            bias = b0_ref[t] if branch == 0 else b12_ref[branch - 1]
            s_c = jnp.sum(kc * q[None], axis=-1, keepdims=True) + bias
            s_n = jnp.sum(kn * q[None], axis=-1, keepdims=True) + bn_ref[t, branch]
            mx = jnp.maximum(jnp.max(s_c, axis=0), jnp.max(s_n, axis=0))
            den = jnp.sum(jnp.exp(s_c - mx[None]), axis=0) + jnp.sum(jnp.exp(s_n - mx[None]), axis=0)
            lse = mx + jnp.log(den)
            p_c = _round_bf16(jnp.exp(s_c - lse[None]))
            p_n = _round_bf16(jnp.exp(s_n - lse[None]))
            outs.append(jnp.sum(p_c * vc, axis=0) + jnp.sum(p_n * vn, axis=0))
            lses.append(lse)
        mx = jnp.maximum(jnp.maximum(lses[0], lses[1]), lses[2])
        wts = [jnp.exp(lse - mx) for lse in lses]
        inv = 1.0 / (wts[0] + wts[1] + wts[2])
        a_ref[0, t] = outs[0] * (wts[0] * inv) + outs[1] * (wts[1] * inv) + outs[2] * (wts[2] * inv)

    base = ext_ref.shape[1] - steps
    for t in range(steps):
        for grp, win in enumerate(POOL_WINDOWS):
            cols = slice(grp * POOL_GROUP_DIM, (grp + 1) * POOL_GROUP_DIM)
            last = base + t
            tot = jnp.sum(ext_ref[0, last - win + 1:last + 1, cols], axis=0, keepdims=True)
            mix_ref[0, t:t + 1, cols] = tot / float(win) - ext_ref[0, last:last + 1, cols]


def sample_attention(q, kv_new, caches, b0, b12, bn, ext_pool):
    n, steps = q.shape[0], q.shape[1]
    hd = (A_HEADS, A_HEAD_DIM)
    in_specs = [
        pl.BlockSpec((1, steps, N_BRANCH) + hd, lambda i: (i, 0, 0, 0, 0)),
        pl.BlockSpec((1, steps, 2, N_BRANCH) + hd, lambda i: (i, 0, 0, 0, 0, 0)),
    ]
    for c in caches:
        res = min(c.shape[2], steps)
        in_specs.append(pl.BlockSpec((1, BAND, res, 2) + hd, lambda i: (i, 0, 0, 0, 0, 0)))
    in_specs += [
        pl.BlockSpec(b0.shape, lambda i: (0, 0, 0, 0)),
        pl.BlockSpec(b12.shape, lambda i: (0, 0, 0, 0)),
        pl.BlockSpec(bn.shape, lambda i: (0, 0, 0, 0, 0)),
        pl.BlockSpec((1,) + ext_pool.shape[1:], lambda i: (i, 0, 0)),
    ]
    kern = functools.partial(_sample_attn_kernel, steps=steps)
    return pl.pallas_call(
        kern,
        grid=(n,),
        in_specs=in_specs,
        out_specs=[
            pl.BlockSpec((1, steps) + hd, lambda i: (i, 0, 0, 0)),
            pl.BlockSpec((1, steps, POOL_WIDTH), lambda i: (i, 0, 0)),
        ],
        out_shape=[
            jax.ShapeDtypeStruct((n, steps) + hd, F32),
            jax.ShapeDtypeStruct((n, steps, POOL_WIDTH), F32),
        ],
        compiler_params=_params("parallel"),
        name="sample_attention",
    )(q, kv_new, *caches, b0, b12, bn, ext_pool)


def _t5_bucket(dist):
    max_exact = NUM_BUCKETS // 2
    d = jnp.maximum(dist, 1).astype(F32)
    large = max_exact + (jnp.log(d / max_exact) / math.log(REL_MAX_DIST / max_exact)
                         * (NUM_BUCKETS - max_exact)).astype(jnp.int32)
    return jnp.where(dist < max_exact, dist, jnp.minimum(large, NUM_BUCKETS - 1))


def _slot_bias(rel_bias):
    rb = rel_bias.astype(F32).reshape(NUM_BUCKETS, N_BRANCH, A_HEADS)
    slot = jnp.arange(BAND + 1)
    return jnp.stack([rb[_t5_bucket(slot * dil), g] for g, (_, dil) in enumerate(A_GROUPS)])


def _prompt_bias_tables(rel_bias):
    sb = _slot_bias(rel_bias)
    qi = jnp.arange(BAND)[:, None]
    kj = jnp.arange(BAND)[None, :]
    out = []
    for rel in (qi + BAND - kj, qi - kj):
        ok = (rel >= 0) & (rel <= BAND)
        tbl = sb[:, jnp.clip(rel, 0, BAND)]
        out.append(jnp.where(ok[None, :, :, None], tbl, NEG))
    return jnp.transpose(jnp.concatenate(out, axis=2), (0, 3, 1, 2))


def _sample_bias_tables(rel_bias, steps):
    sb = _slot_bias(rel_bias)
    shape = (BAND, A_HEADS, A_HEAD_DIM)
    row = jnp.arange(BAND)
    b0 = []
    for t in range(steps):
        slot = BAND + t - row
        tbl = jnp.where((slot <= BAND)[:, None], sb[0, jnp.clip(slot, 0, BAND)], NEG)
        b0.append(jnp.broadcast_to(tbl[:, :, None], shape))
    b0 = jnp.stack(b0)
    b12 = jnp.stack([jnp.broadcast_to(sb[g, BAND - row][:, :, None], shape) for g in (1, 2)])
    bn = []
    for t in range(steps):
        per_branch = []
        for g, (_, dil) in enumerate(A_GROUPS):
            diff = t - jnp.arange(steps)
            ok = (diff >= 0) & (diff % dil == 0)
            tbl = jnp.where(ok[:, None], sb[g, jnp.clip(diff // dil, 0, BAND)], NEG)
            per_branch.append(jnp.broadcast_to(tbl[:, :, None], (steps, A_HEADS, A_HEAD_DIM)))
        bn.append(jnp.stack(per_branch))
    return b0, b12, jnp.stack(bn)


def kernel(x_prompt, x_sample, cache_kv_w128, cache_kv_w512, cache_kv_w2048, state_pool, state_conv, norm_mix0, w_in0, rel_bias, w_pool, pool_scale, w_out0, norm_ffn0, w_ff_gate, w_ff_up, w_ff_down, norm_mix1, w_pw1, b_pw1, w_dw, b_dw, ln_g, ln_b, w_pw2, b_pw2, norm_ffn1, w_router, we_gate, we_up, we_down, norm_final):
    batch, seq, d = x_prompt.shape
    n_s, steps, _ = x_sample.shape
    caches_in = (cache_kv_w128, cache_kv_w512, cache_kv_w2048)
    assert norm_mix0.shape[0] == 1 and norm_mix1.shape[0] == 1, "two layers: one mixer of each kind"
    assert seq % CHUNK == 0 and steps <= min(dil for _, dil in A_GROUPS[1:])
    for c, (window, _) in zip(caches_in, A_GROUPS):
        assert c.shape[2] == window, "cache must hold a full window"
    qkv_w = N_BRANCH * 3 * A_WIDTH

    w_in = w_in0[0]
    w_qkv = w_in[:, :qkv_w].reshape(d, 3, N_BRANCH, A_WIDTH)
    w_qkv_perm = jnp.transpose(w_qkv, (0, 2, 1, 3)).reshape(d, qkv_w).astype(BF16)
    w_kv = w_in[:, A_WIDTH * N_BRANCH:qkv_w].astype(BF16)
    w_u = w_in[:, qkv_w:].astype(BF16)
    w_in_bf = w_in.astype(BF16)
    g_mix0, g_ffn0 = norm_mix0, norm_ffn0
    g_mix1, g_ffn1 = norm_mix1, norm_ffn1
    g_final = norm_final.reshape(1, d)
    wpool_bf = w_pool[0].astype(BF16)
    w_out_a = w_out0[0, :A_WIDTH].astype(BF16)
    w_out_p = w_out0[0, A_WIDTH:].astype(BF16)
    wff = (w_ff_gate.astype(BF16), w_ff_up.astype(BF16), w_ff_down.astype(BF16))
    w_pw1_bf = w_pw1[0].astype(BF16)
    w_pw2_bf = w_pw2[0].astype(BF16)
    w_dw_pad = jnp.pad(w_dw[0], ((0, 32 - CONV_K), (0, 0)))
    wex = (we_gate[0].astype(BF16), we_up[0].astype(BF16), we_down[0].astype(BF16))
    w_router_pad = jnp.pad(w_router[0], ((0, 0), (0, LANES - N_EXPERTS))).astype(BF16)
    zeros = lambda n: jnp.zeros((1, n), F32)

    xp = x_prompt.reshape(batch * seq, d)
    qkv = qkv_perm(xp, g_mix0, w_qkv_perm)
    u_p = rms_matmul(xp, g_mix0, w_u, zeros(POOL_WIDTH), 1024, POOL_WIDTH, F32)
    a_p = band_attention(qkv, _prompt_bias_tables(rel_bias), batch, seq)
    h_p = mix_out(a_p, u_p, xp, wpool_bf, pool_scale, w_out_a, w_out_p, 512, seq)
    h_p = swiglu_block(h_p, g_ffn0, *wff, tm=512, tf=1408)
    glu_p = rms_glu(h_p, g_mix1, w_pw1_bf, b_pw1, 1024, 512)
    h_p = conv_block(glu_p, h_p, w_dw_pad, b_dw, ln_g, ln_b, w_pw2_bf, b_pw2, 256, seq)
    y_p = routed_experts(h_p, g_ffn1, w_router_pad, *wex, g_final)

    keep = A_GROUPS[-1][0]
    x_tail = x_prompt[:, seq - keep:].reshape(batch * keep, d)
    kv_tail = rms_matmul(x_tail, g_mix0, w_kv, zeros(w_kv.shape[1]), 1024, 512, F32)
    kv_tail = kv_tail.reshape(batch, keep, 2, N_BRANCH, A_HEADS, A_HEAD_DIM)
    kv_p = [kv_tail[:, keep - window:, :, g][None] for g, (window, _) in enumerate(A_GROUPS)]
    u_p3 = u_p.reshape(batch, seq, POOL_WIDTH)
    pool_p = u_p3[:, seq - POOL_BUF:][None]
    conv_p = glu_p.reshape(batch, seq, d)[:, seq - (CONV_K - 1):][None]

    m_s = n_s * steps
    xs = x_sample.reshape(m_s, d)
    proj = rms_matmul(xs, g_mix0, w_in_bf, zeros(w_in_bf.shape[1]), m_s, 512, F32)
    qkv_s = proj[:, :qkv_w].reshape(n_s, steps, 3, N_BRANCH, A_HEADS, A_HEAD_DIM)
    u_s = proj[:, qkv_w:].reshape(n_s, steps, POOL_WIDTH)
    q_s = qkv_s[:, :, 0]
    kv_new = qkv_s[:, :, 1:]
    cache_views = [c[0].reshape(n_s, BAND, dil, 2, A_HEADS, A_HEAD_DIM) for c, (_, dil) in zip(caches_in, A_GROUPS)]
    ext_pool = jnp.concatenate([state_pool[0], u_s], axis=1)
    b0, b12, bn = _sample_bias_tables(rel_bias, steps)
    a_s, mix_s = sample_attention(q_s, kv_new, cache_views, b0, b12, bn, ext_pool)
    h_s = mix_out_sample(a_s.reshape(m_s, A_WIDTH), mix_s.reshape(m_s, POOL_WIDTH), xs,
                         wpool_bf, pool_scale, w_out_a, w_out_p)
    h_s = swiglu_block(h_s, g_ffn0, *wff, tm=m_s, tf=1408)
    glu_s = rms_glu(h_s, g_mix1, w_pw1_bf, b_pw1, m_s, 512)
    ext_conv = jnp.concatenate([state_conv[0], glu_s.reshape(n_s, steps, d)], axis=1)
    ext_conv_pad = jnp.pad(ext_conv, ((0, 0), (0, 40 - ext_conv.shape[1]), (0, 0)))
    h_s3 = jnp.pad(h_s.reshape(n_s, steps, d), ((0, 0), (0, 8 - steps), (0, 0)))
    h_s = conv_block_sample(ext_conv_pad, h_s3, w_dw_pad, b_dw, ln_g, ln_b, w_pw2_bf, b_pw2, steps)
    h_s = h_s[:, :steps].reshape(m_s, d)
    y_s = swiglu_block(h_s, g_ffn1, *wex, tm=m_s, tf=896, w_router=w_router_pad, g_final=g_final)

    kv_s = []
    for g, c in enumerate(caches_in):
        new = kv_new[:, :, :, g]
        kv_s.append(jnp.concatenate([c[0, :, steps:], new], axis=1)[None])
    pool_s = ext_pool[:, steps:][None]
    conv_s = ext_conv[:, steps:][None]

    return (y_p.reshape(batch, seq, d), y_s.reshape(n_s, steps, d), kv_p[0], kv_p[1], kv_p[2], pool_p, conv_p,
            kv_s[0], kv_s[1], kv_s[2], pool_s, conv_s)
```

```python
import functools
import math

import jax
import jax.numpy as jnp
from jax import lax
from jax.experimental import pallas as pl
from jax.experimental.pallas import tpu as pltpu

F32 = jnp.float32
BF16 = jnp.bfloat16

A_GROUPS = ((128, 1), (512, 4), (2048, 16))
N_BRANCH = 3
A_HEADS = 8
A_HEAD_DIM = 64
A_WIDTH = 512
BAND = 128
NUM_BUCKETS = 32
REL_MAX_DIST = 2048
POOL_WINDOWS = (2, 4, 8, 16)
POOL_GROUP_DIM = 128
POOL_WIDTH = 512
POOL_BUF = 15
CONV_K = 31
N_EXPERTS = 8
RMS_EPS = 1e-6
LN_EPS = 1e-5
NEG = -1e30

VMEM_LIMIT_BYTES = 56 * 1024 * 1024
LANES = 128
CHUNK = 2048
GROUP = 4


def _params(*sem):
    return pltpu.CompilerParams(dimension_semantics=sem, vmem_limit_bytes=VMEM_LIMIT_BYTES)


def _rmsnorm(x, g):
    return x * lax.rsqrt(jnp.mean(x * x, axis=-1, keepdims=True) + RMS_EPS) * g


def _sigmoid(x):
    return 1.0 / (1.0 + jnp.exp(-x))


def _round_bf16(x):
    return x.astype(BF16).astype(F32)


def _rms_matmul_kernel(x_ref, g_ref, w_ref, b_ref, o_ref, hn_ref):
    @pl.when(pl.program_id(1) == 0)
    def _():
        hn_ref[...] = _rmsnorm(x_ref[...], g_ref[...]).astype(BF16)

    acc = jnp.dot(hn_ref[...], w_ref[...], preferred_element_type=F32)
    o_ref[...] = (acc + b_ref[...]).astype(o_ref.dtype)


def rms_matmul(x, g, w, b, tm, tn, out_dtype):
    m, k = x.shape
    n = w.shape[1]
    assert m % tm == 0 and n % tn == 0
    return pl.pallas_call(
        _rms_matmul_kernel,
        grid=(m // tm, n // tn),
        in_specs=[
            pl.BlockSpec((tm, k), lambda i, j: (i, 0)),
            pl.BlockSpec((1, k), lambda i, j: (0, 0)),
            pl.BlockSpec((k, tn), lambda i, j: (0, j)),
            pl.BlockSpec((1, tn), lambda i, j: (0, j)),
        ],
        out_specs=pl.BlockSpec((tm, tn), lambda i, j: (i, j)),
        out_shape=jax.ShapeDtypeStruct((m, n), out_dtype),
        scratch_shapes=[pltpu.VMEM((tm, k), BF16)],
        compiler_params=_params("parallel", "arbitrary"),
        name="rms_matmul",
    )(x, g, w, b)


def _rms_matmul_t_kernel(x_ref, g_ref, wt_ref, o_ref, hn_ref):
    @pl.when(pl.program_id(2) == 0)
    def _():
        hn_ref[...] = _rmsnorm(x_ref[0], g_ref[...]).astype(BF16)

    nt = (((1,), (1,)), ((), ()))
    o_ref[0] = lax.dot_general(wt_ref[...], hn_ref[...], nt, preferred_element_type=F32)


def rms_matmul_t(x, g, wt, row0, rows, tm, tn):
    batch, _, k = x.shape
    n = wt.shape[0]
    assert row0 % tm == 0 and rows % tm == 0 and n % tn == 0
    return pl.pallas_call(
        _rms_matmul_t_kernel,
        grid=(batch, rows // tm, n // tn),
        in_specs=[
            pl.BlockSpec((1, tm, k), lambda b, i, j: (b, row0 // tm + i, 0)),
            pl.BlockSpec((1, k), lambda b, i, j: (0, 0)),
            pl.BlockSpec((tn, k), lambda b, i, j: (j, 0)),
        ],
        out_specs=pl.BlockSpec((1, tn, tm), lambda b, i, j: (b, j, i)),
        out_shape=jax.ShapeDtypeStruct((batch, n, rows), F32),
        scratch_shapes=[pltpu.VMEM((tm, k), BF16)],
        compiler_params=_params("parallel", "parallel", "arbitrary"),
        name="rms_matmul_t",
    )(x, g, wt)


def _rms_glu_kernel(x_ref, g_ref, wv_ref, wg_ref, bv_ref, bg_ref, o_ref, hn_ref):
    @pl.when(pl.program_id(1) == 0)
    def _():
        hn_ref[...] = _rmsnorm(x_ref[...], g_ref[...]).astype(BF16)

    hn = hn_ref[...]
    val = jnp.dot(hn, wv_ref[...], preferred_element_type=F32) + bv_ref[...]
    gate = jnp.dot(hn, wg_ref[...], preferred_element_type=F32) + bg_ref[...]
    o_ref[...] = val * _sigmoid(gate)


def rms_glu(x, g, w, b, tm, tn):
    m, k = x.shape
    c = w.shape[1] // 2
    nb = c // tn
    return pl.pallas_call(
        _rms_glu_kernel,
        grid=(m // tm, nb),
        in_specs=[
            pl.BlockSpec((tm, k), lambda i, j: (i, 0)),
            pl.BlockSpec((1, k), lambda i, j: (0, 0)),
            pl.BlockSpec((k, tn), lambda i, j: (0, j)),
            pl.BlockSpec((k, tn), lambda i, j: (0, j + nb)),
            pl.BlockSpec((1, tn), lambda i, j: (0, j)),
            pl.BlockSpec((1, tn), lambda i, j: (0, j + nb)),
        ],
        out_specs=pl.BlockSpec((tm, tn), lambda i, j: (i, j)),
        out_shape=jax.ShapeDtypeStruct((m, c), F32),
        scratch_shapes=[pltpu.VMEM((tm, k), BF16)],
        compiler_params=_params("parallel", "arbitrary"),
        name="rms_glu",
    )(x, g, w, w, b, b)


def _qkv_perm_kernel(x_ref, g_ref, w_ref, o_ref, hn_ref, slab_ref):
    j = pl.program_id(1)

    @pl.when(j == 0)
    def _():
        hn_ref[...] = _rmsnorm(x_ref[...], g_ref[...]).astype(BF16)

    res = jnp.dot(hn_ref[...], w_ref[...], preferred_element_type=F32)
    n_slab = res.shape[1] // LANES

    @pl.when(j < 3)
    def _():
        o_ref[...] = res.astype(BF16)

    for s in range(n_slab):
        slab_ref[s] = res[:, s * LANES:(s + 1) * LANES]

    for branch in (1, 2):
        dil = A_GROUPS[branch][1]
        rows = CHUNK // dil

        @pl.when(j // 3 == branch)
        def _():
            for r in range(dil):
                for s in range(n_slab):
                    v = slab_ref[s, pl.ds(r, rows, stride=dil), :]
                    o_ref[r * rows:(r + 1) * rows, s * LANES:(s + 1) * LANES] = v.astype(BF16)


def qkv_perm(x, g, w):
    m, k = x.shape
    n = w.shape[1]
    tn = A_WIDTH
    return pl.pallas_call(
        _qkv_perm_kernel,
        grid=(m // CHUNK, n // tn),
        in_specs=[
            pl.BlockSpec((CHUNK, k), lambda i, j: (i, 0)),
            pl.BlockSpec((1, k), lambda i, j: (0, 0)),
            pl.BlockSpec((k, tn), lambda i, j: (0, j)),
        ],
        out_specs=pl.BlockSpec((CHUNK, tn), lambda i, j: (i, j)),
        out_shape=jax.ShapeDtypeStruct((m, n), BF16),
        scratch_shapes=[pltpu.VMEM((CHUNK, k), BF16), pltpu.VMEM((tn // LANES, CHUNK, LANES), F32)],
        compiler_params=_params("parallel", "arbitrary"),
        name="qkv_perm",
    )(x, g, w)


def _band_attn_kernel(*refs):
    qkv_refs = refs[:15]
    bias_ref, o_ref, acc_ref, lse_ref, kp_ref, vp_ref = refs[15:]
    c = pl.program_id(1)
    n_units = CHUNK // BAND
    grp_rows = GROUP * BAND
    head_of_lane = lax.broadcasted_iota(jnp.int32, (1, 1, LANES), 2) // A_HEAD_DIM
    key_is_prev = lax.broadcasted_iota(jnp.int32, (1, 1, 2 * BAND), 2) < BAND

    for branch, (_, dil) in enumerate(A_GROUPS):
        q_ref, kprev_ref, kcur_ref, vprev_ref, vcur_ref = qkv_refs[5 * branch:5 * branch + 5]
        nblk = n_units // dil
        span = nblk * BAND
        if nblk == 1:
            kp_src, vp_src = kprev_ref, vprev_ref
        else:
            kp_src, vp_src = kp_ref.at[branch], vp_ref.at[branch]
            for res in range(dil):
                lo = res * span
                for dst, prev, cur in ((kp_src, kprev_ref, kcur_ref), (vp_src, vprev_ref, vcur_ref)):
                    dst[lo:lo + BAND, :] = prev[lo + span - BAND:lo + span, :]
                    dst[lo + BAND:lo + span, :] = cur[lo:lo + span - BAND, :]

        def group(gi, carry, q_ref=q_ref, kcur_ref=kcur_ref, vcur_ref=vcur_ref, kp_src=kp_src, vp_src=vp_src,
                  nblk=nblk, dil=dil, branch=branch):
            row = pl.multiple_of(gi * grp_rows, grp_rows)

            def blocks(ref):
                return ref[pl.ds(row, grp_rows), :].reshape(GROUP, BAND, LANES)

            q = blocks(q_ref) * jnp.asarray(A_HEAD_DIM ** -0.5, BF16)
            kcat = jnp.concatenate([blocks(kp_src), blocks(kcur_ref)], axis=1)
            vcat = jnp.concatenate([blocks(vp_src), blocks(vcur_ref)], axis=1)
            unit = gi * GROUP + lax.broadcasted_iota(jnp.int32, (GROUP, 1, 1), 0)
            no_prev = jnp.logical_and(unit % nblk == 0, c == 0)
            pen = jnp.where(jnp.logical_and(no_prev, key_is_prev), NEG, 0.0)

            o = jnp.zeros((GROUP, BAND, LANES), F32)
            lse = jnp.zeros((GROUP, BAND, LANES), F32)
            for hh in range(2):
                mine = head_of_lane == hh
                kh = jnp.where(mine, kcat, jnp.zeros_like(kcat))
                vh = jnp.where(mine, vcat, jnp.zeros_like(vcat))
                s = jnp.einsum("uqd,ukd->uqk", q, kh, preferred_element_type=F32) + bias_ref[branch, hh][None] + pen
                mx = jnp.max(s, axis=-1, keepdims=True)
                p = jnp.exp(s - mx)
                den = jnp.sum(p, axis=-1, keepdims=True)
                pn = (p * (1.0 / den)).astype(BF16)
                o = o + jnp.einsum("uqk,ukd->uqd", pn, vh, preferred_element_type=F32)
                lse = jnp.where(mine, mx + jnp.log(den), lse)

            if dil == 1:
                acc_ref[branch, pl.ds(row, grp_rows), :] = o.reshape(grp_rows, LANES)
                lse_ref[branch, pl.ds(row, grp_rows), :] = lse.reshape(grp_rows, LANES)
            else:
                per = max(GROUP // nblk, 1)
                blocks_per = GROUP // per
                for k in range(per):
                    if nblk >= GROUP:
                        start = gi // (nblk // GROUP) + dil * BAND * ((gi % (nblk // GROUP)) * GROUP)
                    else:
                        start = gi * per + k
                    rows = pl.ds(start, blocks_per * BAND, stride=dil)
                    part = slice(k * blocks_per, (k + 1) * blocks_per)
                    acc_ref[branch, rows, :] = o[part].reshape(blocks_per * BAND, LANES)
                    lse_ref[branch, rows, :] = lse[part].reshape(blocks_per * BAND, LANES)
            return carry

        lax.fori_loop(0, n_units // GROUP, group, 0)

    l0, l1, l2 = lse_ref[0], lse_ref[1], lse_ref[2]
    mx = jnp.maximum(jnp.maximum(l0, l1), l2)
    w0, w1, w2 = jnp.exp(l0 - mx), jnp.exp(l1 - mx), jnp.exp(l2 - mx)
    inv = 1.0 / (w0 + w1 + w2)
    o_ref[...] = (acc_ref[0] * (w0 * inv) + acc_ref[1] * (w1 * inv) + acc_ref[2] * (w2 * inv)).astype(o_ref.dtype)


def band_attention(qkv, bias_tbl, batch, seq):
    n_chunk = seq // CHUNK
    n_pair = A_WIDTH // LANES
    blk = (CHUNK, LANES)
    in_specs = []
    for branch in range(N_BRANCH):
        qcol, kcol, vcol = [(3 * branch + kind) * n_pair for kind in range(3)]

        def cur(col):
            return lambda b, c, hp, col=col: (b * n_chunk + c, col + hp)

        def prev(col):
            return lambda b, c, hp, col=col: (b * n_chunk + jnp.maximum(c - 1, 0), col + hp)

        in_specs += [pl.BlockSpec(blk, cur(qcol)), pl.BlockSpec(blk, prev(kcol)), pl.BlockSpec(blk, cur(kcol)),
                     pl.BlockSpec(blk, prev(vcol)), pl.BlockSpec(blk, cur(vcol))]
    in_specs.append(pl.BlockSpec((N_BRANCH, 2, BAND, 2 * BAND), lambda b, c, hp: (0, hp, 0, 0)))
    return pl.pallas_call(
        _band_attn_kernel,
        grid=(batch, n_chunk, n_pair),
        in_specs=in_specs,
        out_specs=pl.BlockSpec(blk, lambda b, c, hp: (b * n_chunk + c, hp)),
        out_shape=jax.ShapeDtypeStruct((batch * seq, A_WIDTH), BF16),
        scratch_shapes=[pltpu.VMEM((N_BRANCH, CHUNK, LANES), F32)] * 2
        + [pltpu.VMEM((N_BRANCH - 1, CHUNK, LANES), BF16)] * 2,
        compiler_params=_params("parallel", "parallel", "arbitrary"),
        name="band_attention",
    )(*([qkv] * 15), bias_tbl)


def _mix_out_kernel(a_ref, u_ref, halo_ref, x_ref, wpool_ref, scale_ref, woa_ref, wop_ref, o_ref, ext_ref,
                    *, tiles_per_seq):
    i = pl.program_id(0)
    tm = a_ref.shape[0]
    halo = halo_ref.shape[0]
    seq_tile = i % tiles_per_seq
    ext_ref[0:halo, :] = jnp.where(seq_tile == 0, 0.0, halo_ref[...])
    ext_ref[halo:halo + tm, :] = u_ref[...]
    pos = seq_tile * tm + lax.broadcasted_iota(jnp.int32, (tm, POOL_GROUP_DIM), 0)
    acc = x_ref[...] + jnp.dot(a_ref[...], woa_ref[...], preferred_element_type=F32)
    for grp, win in enumerate(POOL_WINDOWS):
        cols = slice(grp * POOL_GROUP_DIM, (grp + 1) * POOL_GROUP_DIM)
        tot = ext_ref[halo:halo + tm, cols]
        for back in range(1, win):
            tot = tot + ext_ref[halo - back:halo - back + tm, cols]
        cnt = jnp.minimum(pos + 1, win).astype(F32)
        mix = tot / cnt - ext_ref[halo:halo + tm, cols]
        y = jnp.dot(mix.astype(BF16), wpool_ref[grp], preferred_element_type=F32) * scale_ref[:, cols]
        acc = acc + jnp.dot(y.astype(BF16), wop_ref[cols, :], preferred_element_type=F32)
    o_ref[...] = acc


def mix_out(a, u, x, w_pool, pool_scale, w_out_a, w_out_p, tm, seq):
    m, d = x.shape
    halo = 16
    assert seq % tm == 0 and tm % halo == 0
    kern = functools.partial(_mix_out_kernel, tiles_per_seq=seq // tm)
    return pl.pallas_call(
        kern,
        grid=(m // tm,),
        in_specs=[
            pl.BlockSpec((tm, A_WIDTH), lambda i: (i, 0)),
            pl.BlockSpec((tm, POOL_WIDTH), lambda i: (i, 0)),
            pl.BlockSpec((halo, POOL_WIDTH), lambda i: (jnp.maximum(i * (tm // halo) - 1, 0), 0)),
            pl.BlockSpec((tm, d), lambda i: (i, 0)),
            pl.BlockSpec((len(POOL_WINDOWS), POOL_GROUP_DIM, POOL_GROUP_DIM), lambda i: (0, 0, 0)),
            pl.BlockSpec((1, POOL_WIDTH), lambda i: (0, 0)),
            pl.BlockSpec((A_WIDTH, d), lambda i: (0, 0)),
            pl.BlockSpec((POOL_WIDTH, d), lambda i: (0, 0)),
        ],
        out_specs=pl.BlockSpec((tm, d), lambda i: (i, 0)),
        out_shape=jax.ShapeDtypeStruct((m, d), F32),
        scratch_shapes=[pltpu.VMEM((tm + halo, POOL_WIDTH), F32)],
        compiler_params=_params("parallel"),
        name="mix_out",
    )(a, u, u, x, w_pool, pool_scale, w_out_a, w_out_p)


def _mix_out_sample_kernel(a_ref, mix_ref, x_ref, wpool_ref, scale_ref, woa_ref, wop_ref, o_ref):
    acc = x_ref[...] + jnp.dot(a_ref[...].astype(BF16), woa_ref[...], preferred_element_type=F32)
    for grp in range(len(POOL_WINDOWS)):
        cols = slice(grp * POOL_GROUP_DIM, (grp + 1) * POOL_GROUP_DIM)
        y = jnp.dot(mix_ref[:, cols].astype(BF16), wpool_ref[grp], preferred_element_type=F32) * scale_ref[:, cols]
        acc = acc + jnp.dot(y.astype(BF16), wop_ref[cols, :], preferred_element_type=F32)
    o_ref[...] = acc


def mix_out_sample(a, mix, x, w_pool, pool_scale, w_out_a, w_out_p):
    m, d = x.shape
    return pl.pallas_call(
        _mix_out_sample_kernel,
        out_shape=jax.ShapeDtypeStruct((m, d), F32),
        compiler_params=pltpu.CompilerParams(vmem_limit_bytes=VMEM_LIMIT_BYTES),
        name="mix_out_sample",
    )(a, mix, x, w_pool, pool_scale, w_out_a, w_out_p)


def _top2(logits):
    lane = lax.broadcasted_iota(jnp.int32, logits.shape, 1)
    lg = jnp.where(lane < N_EXPERTS, logits, -jnp.inf)
    m1 = jnp.max(lg, axis=-1, keepdims=True)
    i1 = jnp.min(jnp.where(lg == m1, lane, LANES), axis=-1, keepdims=True)
    rest = jnp.where(lane == i1, -jnp.inf, lg)
    m2 = jnp.max(rest, axis=-1, keepdims=True)
    i2 = jnp.min(jnp.where(rest == m2, lane, LANES), axis=-1, keepdims=True)
    e2 = jnp.exp(m2 - m1)
    return lane, i1, i2, 1.0 / (1.0 + e2), e2 / (1.0 + e2)


def _top2_gates(logits):
    lane, i1, i2, g1, g2 = _top2(logits)
    return jnp.where(lane == i1, g1, 0.0) + jnp.where(lane == i2, g2, 0.0)


ROUTE_E1, ROUTE_E2, ROUTE_R1, ROUTE_R2, ROUTE_G1, ROUTE_G2 = range(6)
EXPERT_TILE = 512


def _router_kernel(x_ref, g_ref, wr_ref, tri_ref, o_ref):
    hn = _rmsnorm(x_ref[...], g_ref[...]).astype(BF16)
    logits = jnp.dot(hn, wr_ref[...], preferred_element_type=F32)
    lane, i1, i2, g1, g2 = _top2(logits)
    chosen = jnp.where(jnp.logical_or(lane == i1, lane == i2), 1.0, 0.0).astype(BF16)
    before = jnp.dot(tri_ref[...], chosen, preferred_element_type=F32)
    r1 = jnp.sum(jnp.where(lane == i1, before, 0.0), axis=-1, keepdims=True)
    r2 = jnp.sum(jnp.where(lane == i2, before, 0.0), axis=-1, keepdims=True)
    rec = jnp.zeros(logits.shape, F32)
    for pos, val in ((ROUTE_E1, i1.astype(F32)), (ROUTE_E2, i2.astype(F32)), (ROUTE_R1, r1), (ROUTE_R2, r2),
                     (ROUTE_G1, g1), (ROUTE_G2, g2)):
        rec = jnp.where(lane == pos, val, rec)
    o_ref[...] = rec


def route_tokens(x, g, w_router, tm):
    m, d = x.shape
    tri = (jnp.arange(tm)[:, None] > jnp.arange(tm)[None, :]).astype(BF16)
    return pl.pallas_call(
        _router_kernel,
        grid=(m // tm,),
        in_specs=[
            pl.BlockSpec((tm, d), lambda i: (i, 0)),
            pl.BlockSpec((1, d), lambda i: (0, 0)),
            pl.BlockSpec((d, LANES), lambda i: (0, 0)),
            pl.BlockSpec((tm, tm), lambda i: (0, 0)),
        ],
        out_specs=pl.BlockSpec((tm, LANES), lambda i: (i, 0)),
        out_shape=jax.ShapeDtypeStruct((m, LANES), F32),
        compiler_params=_params("parallel"),
        name="route_tokens",
    )(x, g, w_router, tri)


def _row_copy(src, src_row, dst, dst_row, sem):
    return pltpu.make_async_copy(src.at[pl.ds(src_row, 1)], dst.at[pl.ds(dst_row, 1)], sem)


def _scatter_rows_kernel(dest_ref, x_ref, o_hbm, sem):
    tm = x_ref.shape[0]

    def issue(r, carry):
        for k in range(2):
            _row_copy(x_ref, r, o_hbm, dest_ref[0, 0, 2 * r + k], sem).start(priority=k)
        return carry

    def drain(r, carry):
        for k in range(2):
            _row_copy(x_ref, 0, o_hbm, 0, sem).wait()
        return carry

    lax.fori_loop(0, tm, issue, 0, unroll=8)
    lax.fori_loop(0, tm, drain, 0, unroll=8)


def scatter_rows(x, dest, tm):
    m, d = x.shape
    dest3 = dest.reshape(m // tm, 1, 2 * tm)
    return pl.pallas_call(
        _scatter_rows_kernel,
        grid=(m // tm,),
        in_specs=[
            pl.BlockSpec((1, 1, 2 * tm), lambda i: (i, 0, 0), memory_space=pltpu.SMEM),
            pl.BlockSpec((tm, d), lambda i: (i, 0)),
        ],
        out_specs=pl.BlockSpec(memory_space=pl.ANY),
        out_shape=jax.ShapeDtypeStruct((2 * m, d), x.dtype),
        scratch_shapes=[pltpu.SemaphoreType.DMA(())],
        compiler_params=_params("arbitrary"),
        name="scatter_rows",
    )(dest3, x)


def _expert_kernel(tile_ref, exp_ref, lo_ref, hi_ref, nvis_ref, x_ref, g_ref, wg_ref, wu_ref, wd_ref, o_ref,
                   hn_ref, acc_ref):
    v = pl.program_id(0)
    j = pl.program_id(1)

    @pl.when(v < nvis_ref[0])
    def _():
        @pl.when(j == 0)
        def _():
            hn_ref[...] = _rmsnorm(x_ref[...], g_ref[...]).astype(BF16)
            acc_ref[...] = jnp.zeros_like(acc_ref)

        hn = hn_ref[...]
        a = jnp.dot(hn, wg_ref[...], preferred_element_type=F32)
        b = jnp.dot(hn, wu_ref[...], preferred_element_type=F32)
        h = (a * _sigmoid(a) * b).astype(BF16)
        acc_ref[...] += jnp.dot(h, wd_ref[...], preferred_element_type=F32)

        @pl.when(j == pl.num_programs(1) - 1)
        def _():
            row = lax.broadcasted_iota(jnp.int32, (o_ref.shape[0], 1), 0)
            mine = jnp.logical_and(row >= lo_ref[v], row < hi_ref[v])
            first = jnp.logical_or(v == 0, tile_ref[v] != tile_ref[jnp.maximum(v - 1, 0)])

            @pl.when(first)
            def _():
                o_ref[...] = jnp.where(mine, acc_ref[...], 0.0)

            @pl.when(jnp.logical_not(first))
            def _():
                o_ref[...] = jnp.where(mine, acc_ref[...], o_ref[...])


def expert_swiglu(xs, g, w_gate, w_up, w_down, visits, tm, tf):
    m, d = xs.shape
    f = w_gate.shape[2]
    n_f = f // tf
    assert m % tm == 0 and f % tf == 0
    tile, expert, lo, hi, n_visit = visits

    def f_blk(v, j):
        return j + (v % 2) * (n_f - 1 - 2 * j)

    grid_spec = pltpu.PrefetchScalarGridSpec(
        num_scalar_prefetch=5,
        grid=(tile.shape[0], n_f),
        in_specs=[
            pl.BlockSpec((tm, d), lambda v, j, t, e, lo, hi, n: (t[v], 0)),
            pl.BlockSpec((1, d), lambda v, j, t, e, lo, hi, n: (0, 0)),
            pl.BlockSpec((None, d, tf), lambda v, j, t, e, lo, hi, n: (e[v], 0, f_blk(v, j))),
            pl.BlockSpec((None, d, tf), lambda v, j, t, e, lo, hi, n: (e[v], 0, f_blk(v, j))),
            pl.BlockSpec((None, tf, d), lambda v, j, t, e, lo, hi, n: (e[v], f_blk(v, j), 0)),
        ],
        out_specs=pl.BlockSpec((tm, d), lambda v, j, t, e, lo, hi, n: (t[v], 0)),
        scratch_shapes=[pltpu.VMEM((tm, d), BF16), pltpu.VMEM((tm, d), F32)],
    )
    return pl.pallas_call(
        _expert_kernel,
        grid_spec=grid_spec,
        out_shape=jax.ShapeDtypeStruct((m, d), F32),
        compiler_params=_params("arbitrary", "arbitrary"),
        name="expert_swiglu",
    )(tile, expert, lo, hi, n_visit, xs, g, w_gate, w_up, w_down)


def _combine_kernel(dest_ref, x_ref, rec_ref, gf_ref, ys_hbm, o_ref, y1_ref, y2_ref, sem):
    tm = x_ref.shape[0]
    bufs = (y1_ref, y2_ref)

    def issue(r, carry):
        for k in range(2):
            _row_copy(ys_hbm, dest_ref[0, 0, 2 * r + k], bufs[k], r, sem).start(priority=k)
        return carry

    def drain(r, carry):
        for k in range(2):
            _row_copy(ys_hbm, 0, bufs[k], 0, sem).wait()
        return carry

    lax.fori_loop(0, tm, issue, 0, unroll=8)
    lax.fori_loop(0, tm, drain, 0, unroll=8)
    g1 = rec_ref[:, ROUTE_G1:ROUTE_G1 + 1]
    g2 = rec_ref[:, ROUTE_G2:ROUTE_G2 + 1]
    out = x_ref[...] + (g1 * y1_ref[...] + g2 * y2_ref[...])
    o_ref[...] = _rmsnorm(out, gf_ref[...])


def combine_rows(x, rec, dest, ys, g_final, tm):
    m, d = x.shape
    dest3 = dest.reshape(m // tm, 1, 2 * tm)
    return pl.pallas_call(
        _combine_kernel,
        grid=(m // tm,),
        in_specs=[
            pl.BlockSpec((1, 1, 2 * tm), lambda i: (i, 0, 0), memory_space=pltpu.SMEM),
            pl.BlockSpec((tm, d), lambda i: (i, 0)),
            pl.BlockSpec((tm, LANES), lambda i: (i, 0)),
            pl.BlockSpec((1, d), lambda i: (0, 0)),
            pl.BlockSpec(memory_space=pl.ANY),
        ],
        out_specs=pl.BlockSpec((tm, d), lambda i: (i, 0)),
        out_shape=jax.ShapeDtypeStruct((m, d), F32),
        scratch_shapes=[pltpu.VMEM((tm, d), F32), pltpu.VMEM((tm, d), F32), pltpu.SemaphoreType.DMA(())],
        compiler_params=_params("arbitrary"),
        name="combine_rows",
    )(dest3, x, rec, g_final, ys)


def _visit_list(rec, tile_tokens, tm):
    n = rec.shape[0]
    e = rec[:, ROUTE_E1:ROUTE_E2 + 1].astype(jnp.int32)
    rank = rec[:, ROUTE_R1:ROUTE_R2 + 1].astype(jnp.int32)
    onehot = (e[:, :, None] == jnp.arange(N_EXPERTS)[None, None, :]).astype(jnp.int32)
    per_tile = onehot.reshape(n // tile_tokens, tile_tokens * 2, N_EXPERTS).sum(axis=1)
    count = per_tile.sum(axis=0)
    start = jnp.cumsum(count) - count
    tile_base = jnp.cumsum(per_tile, axis=0) - per_tile
    base = (start[None, :] + tile_base)[:, None, :]
    dest = rank + jnp.sum(onehot.reshape(n // tile_tokens, tile_tokens * 2, N_EXPERTS) * base, axis=-1).reshape(n, 2)

    n_tile = 2 * n // tm
    n_visit_max = n_tile + N_EXPERTS
    end = start + count
    first_tile = start // tm
    last_tile = jnp.maximum(end - 1, start) // tm
    n_vis = jnp.where(count > 0, last_tile - first_tile + 1, 0)
    vis_end = jnp.cumsum(n_vis)
    vis_start = vis_end - n_vis
    total = vis_end[-1]
    v = jnp.arange(n_visit_max)
    vc = jnp.minimum(v, total - 1)
    ex = jnp.sum(vc[:, None] >= vis_end[None, :], axis=1)
    tile = first_tile[ex] + vc - vis_start[ex]
    lo = jnp.clip(start[ex] - tile * tm, 0, tm)
    hi = jnp.clip(end[ex] - tile * tm, 0, tm)
    live = v < total
    as_i32 = lambda a: a.astype(jnp.int32)
    visits = (as_i32(tile), as_i32(ex), as_i32(jnp.where(live, lo, 0)), as_i32(jnp.where(live, hi, 0)),
              as_i32(total).reshape(1))
    return dest.astype(jnp.int32), visits


def routed_experts(x, g, w_router, w_gate, w_up, w_down, g_final):
    rec = route_tokens(x, g, w_router, 1024)
    dest, visits = _visit_list(rec, 1024, EXPERT_TILE)
    xs = scatter_rows(x, dest, 512)
    ys = expert_swiglu(xs, g, w_gate, w_up, w_down, visits, EXPERT_TILE, w_gate.shape[2] // 2)
    return combine_rows(x, rec, dest, ys, g_final, 256)


def _swiglu_kernel(*refs, n_expert, final_norm):
    x_ref, g_ref = refs[0], refs[1]
    k = 2
    if n_expert > 1:
        wr_ref = refs[k]
        k += 1
    wg_ref, wu_ref, wd_ref = refs[k:k + 3]
    k += 3
    if final_norm:
        gf_ref = refs[k]
        k += 1
    o_ref, hn_ref, acc_ref = refs[k:k + 3]
    if n_expert > 1:
        gate_ref = refs[k + 3]
    e = pl.program_id(1)
    j = pl.program_id(2)

    @pl.when(jnp.logical_and(e == 0, j == 0))
    def _():
        hn = _rmsnorm(x_ref[...], g_ref[...])
        hn_ref[...] = hn.astype(BF16)
        acc_ref[...] = jnp.zeros_like(acc_ref)
        if n_expert > 1:
            logits = jnp.dot(hn.astype(BF16), wr_ref[...], preferred_element_type=F32)
            gate_ref[...] = _top2_gates(logits)

    hn = hn_ref[...]
    a = jnp.dot(hn, wg_ref[...], preferred_element_type=F32)
    b = jnp.dot(hn, wu_ref[...], preferred_element_type=F32)
    h = (a * _sigmoid(a) * b).astype(BF16)
    y = jnp.dot(h, wd_ref[...], preferred_element_type=F32)
    if n_expert > 1:
        lane = lax.broadcasted_iota(jnp.int32, gate_ref.shape, 1)
        gate = jnp.sum(jnp.where(lane == e, gate_ref[...], 0.0), axis=-1, keepdims=True)
        y = gate * y
    acc_ref[...] += y

    @pl.when(jnp.logical_and(e == n_expert - 1, j == pl.num_programs(2) - 1))
    def _():
        out = x_ref[...] + acc_ref[...]
        if final_norm:
            out = _rmsnorm(out, gf_ref[...])
        o_ref[...] = out


def swiglu_block(x, g, w_gate, w_up, w_down, tm, tf, w_router=None, g_final=None):
    m, d = x.shape
    n_expert, _, f = w_gate.shape
    assert m % tm == 0 and f % tf == 0
    args = [x, g]
    in_specs = [pl.BlockSpec((tm, d), lambda i, e, j: (i, 0)), pl.BlockSpec((1, d), lambda i, e, j: (0, 0))]
    if n_expert > 1:
        args.append(w_router)
        in_specs.append(pl.BlockSpec((d, LANES), lambda i, e, j: (0, 0)))
    args += [w_gate, w_up, w_down]
    in_specs += [
        pl.BlockSpec((None, d, tf), lambda i, e, j: (e, 0, j)),
        pl.BlockSpec((None, d, tf), lambda i, e, j: (e, 0, j)),
        pl.BlockSpec((None, tf, d), lambda i, e, j: (e, j, 0)),
    ]
    if g_final is not None:
        args.append(g_final)
        in_specs.append(pl.BlockSpec((1, d), lambda i, e, j: (0, 0)))
    scratch = [pltpu.VMEM((tm, d), BF16), pltpu.VMEM((tm, d), F32)]
    if n_expert > 1:
        scratch.append(pltpu.VMEM((tm, LANES), F32))
    kern = functools.partial(_swiglu_kernel, n_expert=n_expert, final_norm=g_final is not None)
    return pl.pallas_call(
        kern,
        grid=(m // tm, n_expert, f // tf),
        in_specs=in_specs,
        out_specs=pl.BlockSpec((tm, d), lambda i, e, j: (i, 0)),
        out_shape=jax.ShapeDtypeStruct((m, d), F32),
        scratch_shapes=scratch,
        compiler_params=_params("parallel", "arbitrary", "arbitrary"),
        name="swiglu_block",
    )(*args)


def _ln_silu_pw2(z, lng_ref, lnb_ref, w2_ref, b2_ref):
    mu = jnp.mean(z, axis=-1, keepdims=True)
    zc = z - mu
    var = jnp.mean(zc * zc, axis=-1, keepdims=True)
    zn = zc * lax.rsqrt(var + LN_EPS) * lng_ref[...] + lnb_ref[...]
    act = (zn * _sigmoid(zn)).astype(BF16)
    return jnp.dot(act, w2_ref[...], preferred_element_type=F32) + b2_ref[...]


def _conv_kernel(glu_ref, halo_ref, x_ref, wdw_ref, bdw_ref, lng_ref, lnb_ref, w2_ref, b2_ref, o_ref,
                 ext_ref, z_ref, *, tiles_per_seq, row_chunk):
    i = pl.program_id(0)
    tm = glu_ref.shape[0]
    halo = halo_ref.shape[0]
    n_slab = glu_ref.shape[1] // LANES
    at_start = i % tiles_per_seq == 0
    for cb in range(n_slab):
        cols = slice(cb * LANES, (cb + 1) * LANES)
        ext_ref[cb, 0:halo, :] = jnp.where(at_start, 0.0, _round_bf16(halo_ref[:, cols]))
        ext_ref[cb, halo:halo + tm, :] = _round_bf16(glu_ref[:, cols])
    base = halo - (CONV_K - 1)

    def slab(cb, carry):
        for rc in range(tm // row_chunk):
            r0 = rc * row_chunk
            acc = jnp.zeros((row_chunk, LANES), F32) + bdw_ref[cb]
            for tap in range(CONV_K):
                acc = acc + ext_ref[cb, r0 + base + tap:r0 + base + tap + row_chunk, :] * wdw_ref[cb, tap:tap + 1, :]
            z_ref[cb, r0:r0 + row_chunk, :] = acc
        return carry

    lax.fori_loop(0, n_slab, slab, 0)
    z = jnp.concatenate([z_ref[cb] for cb in range(n_slab)], axis=-1)
    o_ref[...] = x_ref[...] + _ln_silu_pw2(z, lng_ref, lnb_ref, w2_ref, b2_ref)


def conv_block(glu, x, w_dw, b_dw, ln_g, ln_b, w_pw2, b_pw2, tm, seq):
    m, d = x.shape
    halo = 32
    n_slab = d // LANES
    assert seq % tm == 0 and tm % halo == 0
    kern = functools.partial(_conv_kernel, tiles_per_seq=seq // tm, row_chunk=64)
    vec = pl.BlockSpec((1, d), lambda i: (0, 0))
    w_dw_slab = jnp.transpose(w_dw.reshape(w_dw.shape[0], n_slab, LANES), (1, 0, 2))
    b_dw_slab = b_dw.reshape(n_slab, 1, LANES)
    return pl.pallas_call(
        kern,
        grid=(m // tm,),
        in_specs=[
            pl.BlockSpec((tm, d), lambda i: (i, 0)),
            pl.BlockSpec((halo, d), lambda i: (jnp.maximum(i * (tm // halo) - 1, 0), 0)),
            pl.BlockSpec((tm, d), lambda i: (i, 0)),
            pl.BlockSpec(w_dw_slab.shape, lambda i: (0, 0, 0)),
            pl.BlockSpec(b_dw_slab.shape, lambda i: (0, 0, 0)),
            vec, vec,
            pl.BlockSpec((d, d), lambda i: (0, 0)),
            vec,
        ],
        out_specs=pl.BlockSpec((tm, d), lambda i: (i, 0)),
        out_shape=jax.ShapeDtypeStruct((m, d), F32),
        scratch_shapes=[pltpu.VMEM((n_slab, tm + halo, LANES), F32), pltpu.VMEM((n_slab, tm, LANES), F32)],
        compiler_params=_params("parallel"),
        name="conv_block",
    )(glu, glu, x, w_dw_slab, b_dw_slab, ln_g, ln_b, w_pw2, b_pw2)


def _conv_sample_kernel(ext_ref, x_ref, wdw_ref, bdw_ref, lng_ref, lnb_ref, w2_ref, b2_ref, o_ref, z_ref,
                        *, steps):
    n = ext_ref.shape[0]
    z_ref[...] = jnp.zeros_like(z_ref)

    def one(s, carry):
        acc = jnp.zeros((steps, z_ref.shape[2]), F32) + bdw_ref[...]
        for tap in range(CONV_K):
            acc = acc + _round_bf16(ext_ref[s, tap:tap + steps, :]) * wdw_ref[tap:tap + 1, :]
        z_ref[s, 0:steps, :] = acc
        return carry

    lax.fori_loop(0, n, one, 0)
    z = z_ref[...].reshape(n * z_ref.shape[1], z_ref.shape[2])
    y = _ln_silu_pw2(z, lng_ref, lnb_ref, w2_ref, b2_ref)
    o_ref[...] = x_ref[...] + y.reshape(o_ref.shape)


def conv_block_sample(ext, x, w_dw, b_dw, ln_g, ln_b, w_pw2, b_pw2, steps):
    n, _, d = ext.shape
    kern = functools.partial(_conv_sample_kernel, steps=steps)
    return pl.pallas_call(
        kern,
        out_shape=jax.ShapeDtypeStruct((n, 8, d), F32),
        scratch_shapes=[pltpu.VMEM((n, 8, d), F32)],
        compiler_params=pltpu.CompilerParams(vmem_limit_bytes=VMEM_LIMIT_BYTES),
        name="conv_block_sample",
    )(ext, x, w_dw, b_dw, ln_g, ln_b, w_pw2, b_pw2)


def _sample_attn_kernel(q_ref, kn_ref, vn_ref, kvt_ref, c0_ref, c1_ref, c2_ref, b0_ref, b1_ref, b2_ref, bn_ref,
                        ext_ref, a_ref, mix_ref, o0_ref, o1_ref, o2_ref, *, steps):
    caches = (c0_ref, c1_ref, c2_ref)
    biases = (b0_ref, b1_ref, b2_ref)
    new_caches = (o0_ref, o1_ref, o2_ref)
    nt = (((1,), (1,)), ((), ()))
    lane = lax.broadcasted_iota(jnp.int32, (A_HEAD_DIM, LANES), 1)
    keep = LANES - steps

    def head(h, carry):
        outs, lses = [], []
        for branch in range(N_BRANCH):
            c_ref = caches[branch]
            q = (q_ref[0, branch, h] * (A_HEAD_DIM ** -0.5)).astype(BF16)
            kt = c_ref[0, 0, h].astype(BF16)
            vt = c_ref[0, 1, h].astype(BF16)
            kn = kn_ref[0, branch, h].astype(BF16)
            vn = vn_ref[0, branch, h].astype(BF16)
            s_c = jnp.dot(q, kt, preferred_element_type=F32) + biases[branch][h]
            s_n = lax.dot_general(q, kn, nt, preferred_element_type=F32) + bn_ref[branch, h]
            mx = jnp.maximum(jnp.max(s_c, axis=-1, keepdims=True), jnp.max(s_n, axis=-1, keepdims=True))
            den = (jnp.sum(jnp.exp(s_c - mx), axis=-1, keepdims=True)
                   + jnp.sum(jnp.exp(s_n - mx), axis=-1, keepdims=True))
            lse = mx + jnp.log(den)
            p_c = jnp.exp(s_c - lse).astype(BF16)
            p_n = jnp.exp(s_n - lse).astype(BF16)
            outs.append(lax.dot_general(p_c, vt, nt, preferred_element_type=F32)
                        + jnp.dot(p_n, vn, preferred_element_type=F32))
            lses.append(lse)
        mx = jnp.maximum(jnp.maximum(lses[0], lses[1]), lses[2])
        wts = [jnp.exp(lse - mx) for lse in lses]
        inv = 1.0 / (wts[0] + wts[1] + wts[2])
        a_ref[0, h] = outs[0] * (wts[0] * inv) + outs[1] * (wts[1] * inv) + outs[2] * (wts[2] * inv)

        for branch in range(N_BRANCH):
            c_ref, o_ref = caches[branch], new_caches[branch]
            n_tile = c_ref.shape[-1] // LANES
            for kv in range(2):
                nxt = pltpu.roll(c_ref[0, kv, h, :, 0:LANES], keep, 1)
                for j in range(n_tile):
                    cur = nxt
                    if j + 1 < n_tile:
                        nxt = pltpu.roll(c_ref[0, kv, h, :, (j + 1) * LANES:(j + 2) * LANES], keep, 1)
                    else:
                        nxt = kvt_ref[0, branch, kv, h]
                    o_ref[0, kv, h, :, j * LANES:(j + 1) * LANES] = jnp.where(lane < keep, cur, nxt)
        return carry

    lax.fori_loop(0, q_ref.shape[2], head, 0)

    base = ext_ref.shape[1] - steps
    for t in range(steps):
        for grp, win in enumerate(POOL_WINDOWS):
            cols = slice(grp * POOL_GROUP_DIM, (grp + 1) * POOL_GROUP_DIM)
            last = base + t
            tot = jnp.sum(ext_ref[0, last - win + 1:last + 1, cols], axis=0, keepdims=True)
            mix_ref[0, t:t + 1, cols] = tot / float(win) - ext_ref[0, last:last + 1, cols]


def sample_attention(q, kn, vn, kvt, caches, biases, bias_new, ext_pool, steps):
    n, rows = q.shape[0], q.shape[3]
    hg = A_HEADS // 2
    per_head = (hg, rows, A_HEAD_DIM)
    in_specs = [pl.BlockSpec((1, N_BRANCH) + per_head, lambda i, j: (i, 0, j, 0, 0))] * 3
    in_specs.append(pl.BlockSpec((1, N_BRANCH, 2, hg, A_HEAD_DIM, LANES), lambda i, j: (i, 0, 0, j, 0, 0)))
    cache_specs = [pl.BlockSpec((1, 2, hg, A_HEAD_DIM, c.shape[-1]), lambda i, j: (i, 0, j, 0, 0)) for c in caches]
    in_specs += cache_specs
    in_specs += [pl.BlockSpec((hg, rows, b.shape[-1]), lambda i, j: (j, 0, 0)) for b in biases]
    in_specs += [
        pl.BlockSpec((N_BRANCH, hg, rows, rows), lambda i, j: (0, j, 0, 0)),
        pl.BlockSpec((1,) + ext_pool.shape[1:], lambda i, j: (i, 0, 0)),
    ]
    kern = functools.partial(_sample_attn_kernel, steps=steps)
    return pl.pallas_call(
        kern,
        grid=(n, A_HEADS // hg),
        in_specs=in_specs,
        out_specs=[
            pl.BlockSpec((1,) + per_head, lambda i, j: (i, j, 0, 0)),
            pl.BlockSpec((1, steps, POOL_WIDTH), lambda i, j: (i, 0, 0)),
        ] + cache_specs,
        out_shape=[
            jax.ShapeDtypeStruct((n, A_HEADS, rows, A_HEAD_DIM), F32),
            jax.ShapeDtypeStruct((n, steps, POOL_WIDTH), F32),
        ] + [jax.ShapeDtypeStruct(c.shape, F32) for c in caches],
        compiler_params=_params("parallel", "arbitrary"),
        name="sample_attention",
    )(q, kn, vn, kvt, *caches, *biases, bias_new, ext_pool)


def _t5_bucket(dist):
    max_exact = NUM_BUCKETS // 2
    d = jnp.maximum(dist, 1).astype(F32)
    large = max_exact + (jnp.log(d / max_exact) / math.log(REL_MAX_DIST / max_exact)
                         * (NUM_BUCKETS - max_exact)).astype(jnp.int32)
    return jnp.where(dist < max_exact, dist, jnp.minimum(large, NUM_BUCKETS - 1))


def _slot_bias(rel_bias):
    rb = rel_bias.astype(F32).reshape(NUM_BUCKETS, N_BRANCH, A_HEADS)
    slot = jnp.arange(BAND + 1)
    out = []
    for g, (_, dil) in enumerate(A_GROUPS):
        pick = jax.nn.one_hot(_t5_bucket(slot * dil), NUM_BUCKETS, dtype=F32)
        out.append(jnp.einsum("sb,bh->hs", pick, rb[:, g], precision=lax.Precision.HIGHEST))
    return jnp.stack(out)


def _sliding_rows(v, n_rows, width):
    period = v.shape[-1]
    assert period >= n_rows + width - 1
    flat = jnp.tile(v, (1,) * (v.ndim - 1) + (n_rows + 1,))[..., :(period + 1) * n_rows]
    return flat.reshape(v.shape[:-1] + (n_rows, period + 1))[..., :width]


def _prompt_bias_tables(rel_bias):
    sb = _slot_bias(rel_bias)
    lead = sb.shape[:2]
    diag = jnp.concatenate([jnp.full(lead + (BAND - 1,), NEG, F32), sb[..., ::-1],
                            jnp.full(lead + (BAND,), NEG, F32)], axis=-1)
    return _sliding_rows(diag, BAND, 2 * BAND)[:, :, ::-1]


def _sample_bias_tables(rel_bias, steps, rows):
    sb = _slot_bias(rel_bias)
    cache_tables = []
    for g, (window, dil) in enumerate(A_GROUPS):
        spread = jnp.concatenate([sb[g][..., None], jnp.full(sb[g].shape + (dil - 1,), NEG, F32)], axis=-1)
        by_dist = spread.reshape(A_HEADS, (BAND + 1) * dil)
        need = window + rows
        by_dist = jnp.pad(by_dist, ((0, 0), (0, max(need - by_dist.shape[1], 0))), constant_values=NEG)[:, :need]
        by_dist = jnp.where(jnp.arange(need)[None, :] <= window, by_dist, NEG)
        cache_tables.append(_sliding_rows(by_dist[:, ::-1], rows, window)[:, ::-1])
    t_q = jnp.arange(rows)[:, None]
    t_k = jnp.arange(rows)[None, :]
    new_tables = []
    for g, (_, dil) in enumerate(A_GROUPS):
        diff = t_q - t_k
        ok = (diff >= 0) & (diff % dil == 0) & (t_k < steps)
        pick = jax.nn.one_hot(jnp.where(ok, diff // dil, BAND + 1).reshape(-1), BAND + 1, dtype=F32)
        vals = jnp.einsum("ps,hs->hp", pick, sb[g], precision=lax.Precision.HIGHEST).reshape(A_HEADS, rows, rows)
        new_tables.append(jnp.where(ok[None], vals, NEG))
    return cache_tables, jnp.stack(new_tables)


def kernel(x_prompt, x_sample, cache_kv_w128, cache_kv_w512, cache_kv_w2048, state_pool, state_conv, norm_mix0, w_in0, rel_bias, w_pool, pool_scale, w_out0, norm_ffn0, w_ff_gate, w_ff_up, w_ff_down, norm_mix1, w_pw1, b_pw1, w_dw, b_dw, ln_g, ln_b, w_pw2, b_pw2, norm_ffn1, w_router, we_gate, we_up, we_down, norm_final):
    batch, seq, d = x_prompt.shape
    n_s, steps, _ = x_sample.shape
    caches_in = (cache_kv_w128, cache_kv_w512, cache_kv_w2048)
    assert norm_mix0.shape[0] == 1 and norm_mix1.shape[0] == 1, "two layers: one mixer of each kind"
    assert seq % CHUNK == 0 and steps <= min(dil for _, dil in A_GROUPS[1:])
    for c, (window, _) in zip(caches_in, A_GROUPS):
        assert c.shape[2] == window, "cache must hold a full window"
    qkv_w = N_BRANCH * 3 * A_WIDTH

    w_in = w_in0[0]
    w_qkv = w_in[:, :qkv_w].reshape(d, 3, N_BRANCH, A_WIDTH)
    w_qkv_perm = jnp.transpose(w_qkv, (0, 2, 1, 3)).reshape(d, qkv_w).astype(BF16)
    w_kv = w_in[:, A_WIDTH * N_BRANCH:qkv_w].astype(BF16)
    w_u = w_in[:, qkv_w:].astype(BF16)
    w_in_bf = w_in.astype(BF16)
    g_mix0, g_ffn0 = norm_mix0, norm_ffn0
    g_mix1, g_ffn1 = norm_mix1, norm_ffn1
    g_final = norm_final.reshape(1, d)
    wpool_bf = w_pool[0].astype(BF16)
    w_out_a = w_out0[0, :A_WIDTH].astype(BF16)
    w_out_p = w_out0[0, A_WIDTH:].astype(BF16)
    wff = (w_ff_gate.astype(BF16), w_ff_up.astype(BF16), w_ff_down.astype(BF16))
    w_pw1_bf = w_pw1[0].astype(BF16)
    w_pw2_bf = w_pw2[0].astype(BF16)
    w_dw_pad = jnp.pad(w_dw[0], ((0, 32 - CONV_K), (0, 0)))
    wex = (we_gate[0].astype(BF16), we_up[0].astype(BF16), we_down[0].astype(BF16))
    w_router_pad = jnp.pad(w_router[0], ((0, 0), (0, LANES - N_EXPERTS))).astype(BF16)
    zeros = lambda n: jnp.zeros((1, n), F32)

    xp = x_prompt.reshape(batch * seq, d)
    qkv = qkv_perm(xp, g_mix0, w_qkv_perm)
    u_p = rms_matmul(xp, g_mix0, w_u, zeros(POOL_WIDTH), 1024, POOL_WIDTH, F32)
    a_p = band_attention(qkv, _prompt_bias_tables(rel_bias), batch, seq)
    h_p = mix_out(a_p, u_p, xp, wpool_bf, pool_scale, w_out_a, w_out_p, 512, seq)
    h_p = swiglu_block(h_p, g_ffn0, *wff, tm=512, tf=1408)
    glu_p = rms_glu(h_p, g_mix1, w_pw1_bf, b_pw1, 1024, 512)
    h_p = conv_block(glu_p, h_p, w_dw_pad, b_dw, ln_g, ln_b, w_pw2_bf, b_pw2, 256, seq)
    y_p = routed_experts(h_p, g_ffn1, w_router_pad, *wex, g_final)

    keep = A_GROUPS[-1][0]
    kv_tail = rms_matmul_t(x_prompt, g_mix0, w_kv.T, seq - keep, keep, 1024, 512)
    kv_tail = kv_tail.reshape(batch, 2, N_BRANCH, A_HEADS, A_HEAD_DIM, keep)
    kv_p = [jnp.transpose(kv_tail[:, :, g, :, :, keep - window:], (0, 4, 1, 2, 3))[None]
            for g, (window, _) in enumerate(A_GROUPS)]
    u_p3 = u_p.reshape(batch, seq, POOL_WIDTH)
    pool_p = u_p3[:, seq - POOL_BUF:][None]
    conv_p = glu_p.reshape(batch, seq, d)[:, seq - (CONV_K - 1):][None]

    m_s = n_s * steps
    xs = x_sample.reshape(m_s, d)
    proj = rms_matmul(xs, g_mix0, w_in_bf, zeros(w_in_bf.shape[1]), m_s, 512, F32)
    qkv_s = proj[:, :qkv_w].reshape(n_s, steps, 3, N_BRANCH, A_HEADS, A_HEAD_DIM)
    u_s = proj[:, qkv_w:].reshape(n_s, steps, POOL_WIDTH)
    rows = 8
    per_head = jnp.transpose(qkv_s, (2, 0, 3, 4, 1, 5))
    per_head = jnp.pad(per_head, ((0, 0),) * 4 + ((0, rows - steps), (0, 0)))
    new_cols = jnp.transpose(qkv_s[:, :, 1:], (0, 3, 2, 4, 5, 1))
    new_cols = jnp.pad(new_cols, ((0, 0),) * 5 + ((LANES - steps, 0),))
    caches_t = [jnp.transpose(c[0], (0, 2, 3, 4, 1)) for c in caches_in]
    ext_pool = jnp.concatenate([state_pool[0], u_s], axis=1)
    cache_bias, new_bias = _sample_bias_tables(rel_bias, steps, rows)
    a_s, mix_s, *caches_out = sample_attention(per_head[0], per_head[1], per_head[2], new_cols, caches_t,
                                               cache_bias, new_bias, ext_pool, steps)
    a_s = jnp.transpose(a_s[:, :, :steps], (0, 2, 1, 3)).reshape(m_s, A_WIDTH)
    h_s = mix_out_sample(a_s, mix_s.reshape(m_s, POOL_WIDTH), xs, wpool_bf, pool_scale, w_out_a, w_out_p)
    h_s = swiglu_block(h_s, g_ffn0, *wff, tm=m_s, tf=1408)
    glu_s = rms_glu(h_s, g_mix1, w_pw1_bf, b_pw1, m_s, 512)
    ext_conv = jnp.concatenate([state_conv[0], glu_s.reshape(n_s, steps, d)], axis=1)
    ext_conv_pad = jnp.pad(ext_conv, ((0, 0), (0, 40 - ext_conv.shape[1]), (0, 0)))
    h_s3 = jnp.pad(h_s.reshape(n_s, steps, d), ((0, 0), (0, 8 - steps), (0, 0)))
    h_s = conv_block_sample(ext_conv_pad, h_s3, w_dw_pad, b_dw, ln_g, ln_b, w_pw2_bf, b_pw2, steps)
    h_s = h_s[:, :steps].reshape(m_s, d)
    y_s = swiglu_block(h_s, g_ffn1, *wex, tm=m_s, tf=896, w_router=w_router_pad, g_final=g_final)

    kv_s = [jnp.transpose(c, (0, 4, 1, 2, 3))[None] for c in caches_out]
    pool_s = ext_pool[:, steps:][None]
    conv_s = ext_conv[:, steps:][None]

    return (y_p.reshape(batch, seq, d), y_s.reshape(n_s, steps, d), kv_p[0], kv_p[1], kv_p[2], pool_p, conv_p,
            kv_s[0], kv_s[1], kv_s[2], pool_s, conv_s)
```

```python
import functools
import math

import jax
import jax.numpy as jnp
from jax import lax
from jax.experimental import pallas as pl
from jax.experimental.pallas import tpu as pltpu

F32 = jnp.float32
BF16 = jnp.bfloat16

A_GROUPS = ((128, 1), (512, 4), (2048, 16))
N_BRANCH = 3
A_HEADS = 8
A_HEAD_DIM = 64
A_WIDTH = 512
BAND = 128
NUM_BUCKETS = 32
REL_MAX_DIST = 2048
POOL_WINDOWS = (2, 4, 8, 16)
POOL_GROUP_DIM = 128
POOL_WIDTH = 512
POOL_BUF = 15
CONV_K = 31
N_EXPERTS = 8
RMS_EPS = 1e-6
LN_EPS = 1e-5
NEG = -1e30

VMEM_LIMIT_BYTES = 56 * 1024 * 1024
LANES = 128
CHUNK = 2048
GROUP = 8


def _params(*sem):
    return pltpu.CompilerParams(dimension_semantics=sem, vmem_limit_bytes=VMEM_LIMIT_BYTES)


def _rmsnorm(x, g):
    return x * lax.rsqrt(jnp.mean(x * x, axis=-1, keepdims=True) + RMS_EPS) * g


def _sigmoid(x):
    return 1.0 / (1.0 + jnp.exp(-x))


def _round_bf16(x):
    return x.astype(BF16).astype(F32)


def _rms_matmul_kernel(x_ref, g_ref, w_ref, b_ref, o_ref, hn_ref):
    @pl.when(pl.program_id(1) == 0)
    def _():
        hn_ref[...] = _rmsnorm(x_ref[...], g_ref[...]).astype(BF16)

    acc = jnp.dot(hn_ref[...], w_ref[...], preferred_element_type=F32)
    o_ref[...] = (acc + b_ref[...]).astype(o_ref.dtype)


def rms_matmul(x, g, w, b, tm, tn, out_dtype):
    m, k = x.shape
    n = w.shape[1]
    assert m % tm == 0 and n % tn == 0
    return pl.pallas_call(
        _rms_matmul_kernel,
        grid=(m // tm, n // tn),
        in_specs=[
            pl.BlockSpec((tm, k), lambda i, j: (i, 0)),
            pl.BlockSpec((1, k), lambda i, j: (0, 0)),
            pl.BlockSpec((k, tn), lambda i, j: (0, j)),
            pl.BlockSpec((1, tn), lambda i, j: (0, j)),
        ],
        out_specs=pl.BlockSpec((tm, tn), lambda i, j: (i, j)),
        out_shape=jax.ShapeDtypeStruct((m, n), out_dtype),
        scratch_shapes=[pltpu.VMEM((tm, k), BF16)],
        compiler_params=_params("parallel", "arbitrary"),
        name="rms_matmul",
    )(x, g, w, b)


def _rms_matmul_t_kernel(x_ref, g_ref, wt_ref, o_ref, hn_ref):
    @pl.when(pl.program_id(2) == 0)
    def _():
        hn_ref[...] = _rmsnorm(x_ref[0], g_ref[...]).astype(BF16)

    nt = (((1,), (1,)), ((), ()))
    o_ref[0] = lax.dot_general(wt_ref[...], hn_ref[...], nt, preferred_element_type=F32)


def rms_matmul_t(x, g, wt, row0, rows, tm, tn):
    batch, _, k = x.shape
    n = wt.shape[0]
    assert row0 % tm == 0 and rows % tm == 0 and n % tn == 0
    return pl.pallas_call(
        _rms_matmul_t_kernel,
        grid=(batch, rows // tm, n // tn),
        in_specs=[
            pl.BlockSpec((1, tm, k), lambda b, i, j: (b, row0 // tm + i, 0)),
            pl.BlockSpec((1, k), lambda b, i, j: (0, 0)),
            pl.BlockSpec((tn, k), lambda b, i, j: (j, 0)),
        ],
        out_specs=pl.BlockSpec((1, tn, tm), lambda b, i, j: (b, j, i)),
        out_shape=jax.ShapeDtypeStruct((batch, n, rows), F32),
        scratch_shapes=[pltpu.VMEM((tm, k), BF16)],
        compiler_params=_params("parallel", "parallel", "arbitrary"),
        name="rms_matmul_t",
    )(x, g, wt)


def _rms_glu_kernel(x_ref, g_ref, wv_ref, wg_ref, bv_ref, bg_ref, o_ref, hn_ref):
    @pl.when(pl.program_id(1) == 0)
    def _():
        hn_ref[...] = _rmsnorm(x_ref[...], g_ref[...]).astype(BF16)

    hn = hn_ref[...]
    val = jnp.dot(hn, wv_ref[...], preferred_element_type=F32) + bv_ref[...]
    gate = jnp.dot(hn, wg_ref[...], preferred_element_type=F32) + bg_ref[...]
    o_ref[...] = val * _sigmoid(gate)


def rms_glu(x, g, w, b, tm, tn):
    m, k = x.shape
    c = w.shape[1] // 2
    nb = c // tn
    return pl.pallas_call(
        _rms_glu_kernel,
        grid=(m // tm, nb),
        in_specs=[
            pl.BlockSpec((tm, k), lambda i, j: (i, 0)),
            pl.BlockSpec((1, k), lambda i, j: (0, 0)),
            pl.BlockSpec((k, tn), lambda i, j: (0, j)),
            pl.BlockSpec((k, tn), lambda i, j: (0, j + nb)),
            pl.BlockSpec((1, tn), lambda i, j: (0, j)),
            pl.BlockSpec((1, tn), lambda i, j: (0, j + nb)),
        ],
        out_specs=pl.BlockSpec((tm, tn), lambda i, j: (i, j)),
        out_shape=jax.ShapeDtypeStruct((m, c), F32),
        scratch_shapes=[pltpu.VMEM((tm, k), BF16)],
        compiler_params=_params("parallel", "arbitrary"),
        name="rms_glu",
    )(x, g, w, w, b, b)


PERM_BLOCK = 256


def _qkv_perm_kernel(x_ref, g_ref, w_ref, perm_ref, o_ref, u_ref, hn_ref):
    j = pl.program_id(1)
    n_qkv = 3 * N_BRANCH

    @pl.when(j == 0)
    def _():
        hn_ref[...] = _rmsnorm(x_ref[...], g_ref[...]).astype(BF16)

    def project():
        return jnp.dot(hn_ref[...], w_ref[...], preferred_element_type=F32)

    @pl.when(j == n_qkv)
    def _():
        u_ref[...] = project()

    @pl.when(j < 3)
    def _():
        o_ref[...] = project().astype(BF16)

    for branch in (1, 2):
        dil = A_GROUPS[branch][1]
        rows = CHUNK // dil
        per = PERM_BLOCK // dil

        @pl.when(jnp.logical_and(j // 3 == branch, j < n_qkv))
        def _():
            res_bf = project().astype(BF16)
            for b in range(CHUNK // PERM_BLOCK):
                blk = jnp.dot(perm_ref[branch - 1], res_bf[b * PERM_BLOCK:(b + 1) * PERM_BLOCK],
                              preferred_element_type=F32).astype(BF16)
                for r in range(dil):
                    o_ref[r * rows + b * per:r * rows + (b + 1) * per, :] = blk[r * per:(r + 1) * per]


def qkv_perm(x, g, w_in):
    m, k = x.shape
    tn = A_WIDTH
    n_qkv = 3 * N_BRANCH
    assert w_in.shape[1] == (n_qkv + 1) * tn
    perms = []
    for _, dil in A_GROUPS[1:]:
        p = jnp.arange(PERM_BLOCK)
        src = (p % (PERM_BLOCK // dil)) * dil + p // (PERM_BLOCK // dil)
        perms.append((src[:, None] == jnp.arange(PERM_BLOCK)[None, :]).astype(BF16))

    def w_block(i, j):
        return 0, jnp.where(j < n_qkv, (j % 3) * N_BRANCH + j // 3, n_qkv)

    return pl.pallas_call(
        _qkv_perm_kernel,
        grid=(m // CHUNK, n_qkv + 1),
        in_specs=[
            pl.BlockSpec((CHUNK, k), lambda i, j: (i, 0)),
            pl.BlockSpec((1, k), lambda i, j: (0, 0)),
            pl.BlockSpec((k, tn), w_block),
            pl.BlockSpec((N_BRANCH - 1, PERM_BLOCK, PERM_BLOCK), lambda i, j: (0, 0, 0)),
        ],
        out_specs=[
            pl.BlockSpec((CHUNK, tn), lambda i, j: (i, jnp.minimum(j, n_qkv - 1))),
            pl.BlockSpec((CHUNK, tn), lambda i, j: (i, 0)),
        ],
        out_shape=[jax.ShapeDtypeStruct((m, n_qkv * tn), BF16), jax.ShapeDtypeStruct((m, tn), F32)],
        scratch_shapes=[pltpu.VMEM((CHUNK, k), BF16)],
        compiler_params=_params("parallel", "arbitrary"),
        name="qkv_perm",
    )(x, g, w_in, jnp.stack(perms))


def _band_attn_kernel(*refs):
    qkv_refs = refs[:15]
    bias_ref, o_ref, acc_ref, lse_ref, kp_ref, vp_ref = refs[15:]
    c = pl.program_id(1)
    n_units = CHUNK // BAND
    grp_rows = GROUP * BAND
    head_of_lane = lax.broadcasted_iota(jnp.int32, (1, 1, LANES), 2) // A_HEAD_DIM
    key_is_prev = lax.broadcasted_iota(jnp.int32, (1, 1, 2 * BAND), 2) < BAND

    for branch, (_, dil) in enumerate(A_GROUPS):
        q_ref, kprev_ref, kcur_ref, vprev_ref, vcur_ref = qkv_refs[5 * branch:5 * branch + 5]
        nblk = n_units // dil
        span = nblk * BAND
        if nblk == 1:
            kp_src, vp_src = kprev_ref, vprev_ref
        else:
            kp_src, vp_src = kp_ref.at[branch], vp_ref.at[branch]
            for res in range(dil):
                lo = res * span
                for dst, prev, cur in ((kp_src, kprev_ref, kcur_ref), (vp_src, vprev_ref, vcur_ref)):
                    dst[lo:lo + BAND, :] = prev[lo + span - BAND:lo + span, :]
                    dst[lo + BAND:lo + span, :] = cur[lo:lo + span - BAND, :]

        def group(gi, carry, q_ref=q_ref, kcur_ref=kcur_ref, vcur_ref=vcur_ref, kp_src=kp_src, vp_src=vp_src,
                  nblk=nblk, dil=dil, branch=branch):
            row = pl.multiple_of(gi * grp_rows, grp_rows)

            def blocks(ref):
                return ref[pl.ds(row, grp_rows), :].reshape(GROUP, BAND, LANES)

            q = blocks(q_ref) * jnp.asarray(A_HEAD_DIM ** -0.5, BF16)
            kcat = jnp.concatenate([blocks(kp_src), blocks(kcur_ref)], axis=1)
            vcat = jnp.concatenate([blocks(vp_src), blocks(vcur_ref)], axis=1)
            unit = gi * GROUP + lax.broadcasted_iota(jnp.int32, (GROUP, 1, 1), 0)
            no_prev = jnp.logical_and(unit % nblk == 0, c == 0)
            pen = jnp.where(jnp.logical_and(no_prev, key_is_prev), NEG, 0.0)

            o = jnp.zeros((GROUP, BAND, LANES), F32)
            lse = jnp.zeros((GROUP, BAND, LANES), F32)
            for hh in range(2):
                mine = head_of_lane == hh
                kh = jnp.where(mine, kcat, jnp.zeros_like(kcat))
                vh = jnp.where(mine, vcat, jnp.zeros_like(vcat))
                s = jnp.einsum("uqd,ukd->uqk", q, kh, preferred_element_type=F32) + bias_ref[branch, hh][None] + pen
                mx = jnp.max(s, axis=-1, keepdims=True)
                p = jnp.exp(s - mx)
                den = jnp.sum(p, axis=-1, keepdims=True)
                pn = (p * (1.0 / den)).astype(BF16)
                o = o + jnp.einsum("uqk,ukd->uqd", pn, vh, preferred_element_type=F32)
                lse = jnp.where(mine, mx + jnp.log(den), lse)

            if dil == 1:
                acc_ref[branch, pl.ds(row, grp_rows), :] = o.reshape(grp_rows, LANES)
                lse_ref[branch, pl.ds(row, grp_rows), :] = lse.reshape(grp_rows, LANES)
            else:
                per = max(GROUP // nblk, 1)
                blocks_per = GROUP // per
                for k in range(per):
                    if nblk >= GROUP:
                        start = gi // (nblk // GROUP) + dil * BAND * ((gi % (nblk // GROUP)) * GROUP)
                    else:
                        start = gi * per + k
                    rows = pl.ds(start, blocks_per * BAND, stride=dil)
                    part = slice(k * blocks_per, (k + 1) * blocks_per)
                    acc_ref[branch, rows, :] = o[part].reshape(blocks_per * BAND, LANES)
                    lse_ref[branch, rows, :] = lse[part].reshape(blocks_per * BAND, LANES)
            return carry

        lax.fori_loop(0, n_units // GROUP, group, 0)

    l0, l1, l2 = lse_ref[0], lse_ref[1], lse_ref[2]
    mx = jnp.maximum(jnp.maximum(l0, l1), l2)
    w0, w1, w2 = jnp.exp(l0 - mx), jnp.exp(l1 - mx), jnp.exp(l2 - mx)
    inv = 1.0 / (w0 + w1 + w2)
    o_ref[...] = (acc_ref[0] * (w0 * inv) + acc_ref[1] * (w1 * inv) + acc_ref[2] * (w2 * inv)).astype(o_ref.dtype)


def band_attention(qkv, bias_tbl, batch, seq):
    n_chunk = seq // CHUNK
    n_pair = A_WIDTH // LANES
    blk = (CHUNK, LANES)
    in_specs = []
    for branch in range(N_BRANCH):
        qcol, kcol, vcol = [(3 * branch + kind) * n_pair for kind in range(3)]

        def cur(col):
            return lambda b, c, hp, col=col: (b * n_chunk + c, col + hp)

        def prev(col):
            return lambda b, c, hp, col=col: (b * n_chunk + jnp.maximum(c - 1, 0), col + hp)

        in_specs += [pl.BlockSpec(blk, cur(qcol)), pl.BlockSpec(blk, prev(kcol)), pl.BlockSpec(blk, cur(kcol)),
                     pl.BlockSpec(blk, prev(vcol)), pl.BlockSpec(blk, cur(vcol))]
    in_specs.append(pl.BlockSpec((N_BRANCH, 2, BAND, 2 * BAND), lambda b, c, hp: (0, hp, 0, 0)))
    return pl.pallas_call(
        _band_attn_kernel,
        grid=(batch, n_chunk, n_pair),
        in_specs=in_specs,
        out_specs=pl.BlockSpec(blk, lambda b, c, hp: (b * n_chunk + c, hp)),
        out_shape=jax.ShapeDtypeStruct((batch * seq, A_WIDTH), BF16),
        scratch_shapes=[pltpu.VMEM((N_BRANCH, CHUNK, LANES), F32)] * 2
        + [pltpu.VMEM((N_BRANCH - 1, CHUNK, LANES), BF16)] * 2,
        compiler_params=_params("parallel", "parallel", "arbitrary"),
        name="band_attention",
    )(*([qkv] * 15), bias_tbl)


def _mix_out_kernel(a_ref, u_ref, halo_ref, x_ref, wpool_ref, scale_ref, woa_ref, wop_ref, o_ref, ext_ref,
                    *, tiles_per_seq):
    i = pl.program_id(0)
    tm = a_ref.shape[0]
    halo = halo_ref.shape[0]
    seq_tile = i % tiles_per_seq
    ext_ref[0:halo, :] = jnp.where(seq_tile == 0, 0.0, halo_ref[...])
    ext_ref[halo:halo + tm, :] = u_ref[...]
    pos = seq_tile * tm + lax.broadcasted_iota(jnp.int32, (tm, POOL_GROUP_DIM), 0)
    acc = x_ref[...] + jnp.dot(a_ref[...], woa_ref[...], preferred_element_type=F32)
    for grp, win in enumerate(POOL_WINDOWS):
        cols = slice(grp * POOL_GROUP_DIM, (grp + 1) * POOL_GROUP_DIM)
        tot = ext_ref[halo:halo + tm, cols]
        for back in range(1, win):
            tot = tot + ext_ref[halo - back:halo - back + tm, cols]
        cnt = jnp.minimum(pos + 1, win).astype(F32)
        mix = tot / cnt - ext_ref[halo:halo + tm, cols]
        y = jnp.dot(mix.astype(BF16), wpool_ref[grp], preferred_element_type=F32) * scale_ref[:, cols]
        acc = acc + jnp.dot(y.astype(BF16), wop_ref[cols, :], preferred_element_type=F32)
    o_ref[...] = acc


def mix_out(a, u, x, w_pool, pool_scale, w_out_a, w_out_p, tm, seq):
    m, d = x.shape
    halo = 16
    assert seq % tm == 0 and tm % halo == 0
    kern = functools.partial(_mix_out_kernel, tiles_per_seq=seq // tm)
    return pl.pallas_call(
        kern,
        grid=(m // tm,),
        in_specs=[
            pl.BlockSpec((tm, A_WIDTH), lambda i: (i, 0)),
            pl.BlockSpec((tm, POOL_WIDTH), lambda i: (i, 0)),
            pl.BlockSpec((halo, POOL_WIDTH), lambda i: (jnp.maximum(i * (tm // halo) - 1, 0), 0)),
            pl.BlockSpec((tm, d), lambda i: (i, 0)),
            pl.BlockSpec((len(POOL_WINDOWS), POOL_GROUP_DIM, POOL_GROUP_DIM), lambda i: (0, 0, 0)),
            pl.BlockSpec((1, POOL_WIDTH), lambda i: (0, 0)),
            pl.BlockSpec((A_WIDTH, d), lambda i: (0, 0)),
            pl.BlockSpec((POOL_WIDTH, d), lambda i: (0, 0)),
        ],
        out_specs=pl.BlockSpec((tm, d), lambda i: (i, 0)),
        out_shape=jax.ShapeDtypeStruct((m, d), F32),
        scratch_shapes=[pltpu.VMEM((tm + halo, POOL_WIDTH), F32)],
        compiler_params=_params("parallel"),
        name="mix_out",
    )(a, u, u, x, w_pool, pool_scale, w_out_a, w_out_p)


def _mix_out_sample_kernel(a_ref, mix_ref, x_ref, wpool_ref, scale_ref, woa_ref, wop_ref, o_ref):
    acc = x_ref[...] + jnp.dot(a_ref[...].astype(BF16), woa_ref[...], preferred_element_type=F32)
    for grp in range(len(POOL_WINDOWS)):
        cols = slice(grp * POOL_GROUP_DIM, (grp + 1) * POOL_GROUP_DIM)
        y = jnp.dot(mix_ref[:, cols].astype(BF16), wpool_ref[grp], preferred_element_type=F32) * scale_ref[:, cols]
        acc = acc + jnp.dot(y.astype(BF16), wop_ref[cols, :], preferred_element_type=F32)
    o_ref[...] = acc


def mix_out_sample(a, mix, x, w_pool, pool_scale, w_out_a, w_out_p):
    m, d = x.shape
    return pl.pallas_call(
        _mix_out_sample_kernel,
        out_shape=jax.ShapeDtypeStruct((m, d), F32),
        compiler_params=pltpu.CompilerParams(vmem_limit_bytes=VMEM_LIMIT_BYTES),
        name="mix_out_sample",
    )(a, mix, x, w_pool, pool_scale, w_out_a, w_out_p)


def _top2(logits):
    lane = lax.broadcasted_iota(jnp.int32, logits.shape, 1)
    lg = jnp.where(lane < N_EXPERTS, logits, -jnp.inf)
    m1 = jnp.max(lg, axis=-1, keepdims=True)
    i1 = jnp.min(jnp.where(lg == m1, lane, LANES), axis=-1, keepdims=True)
    rest = jnp.where(lane == i1, -jnp.inf, lg)
    m2 = jnp.max(rest, axis=-1, keepdims=True)
    i2 = jnp.min(jnp.where(rest == m2, lane, LANES), axis=-1, keepdims=True)
    e2 = jnp.exp(m2 - m1)
    return lane, i1, i2, 1.0 / (1.0 + e2), e2 / (1.0 + e2)


def _top2_gates(logits):
    lane, i1, i2, g1, g2 = _top2(logits)
    return jnp.where(lane == i1, g1, 0.0) + jnp.where(lane == i2, g2, 0.0)


ROUTE_E1, ROUTE_E2, ROUTE_R1, ROUTE_R2, ROUTE_G1, ROUTE_G2 = range(6)
EXPERT_TILE = 512


def _router_kernel(x_ref, g_ref, wr_ref, tri_ref, o_ref):
    hn = _rmsnorm(x_ref[...], g_ref[...]).astype(BF16)
    logits = jnp.dot(hn, wr_ref[...], preferred_element_type=F32)
    lane, i1, i2, g1, g2 = _top2(logits)
    chosen = jnp.where(jnp.logical_or(lane == i1, lane == i2), 1.0, 0.0).astype(BF16)
    before = jnp.dot(tri_ref[...], chosen, preferred_element_type=F32)
    r1 = jnp.sum(jnp.where(lane == i1, before, 0.0), axis=-1, keepdims=True)
    r2 = jnp.sum(jnp.where(lane == i2, before, 0.0), axis=-1, keepdims=True)
    rec = jnp.zeros(logits.shape, F32)
    for pos, val in ((ROUTE_E1, i1.astype(F32)), (ROUTE_E2, i2.astype(F32)), (ROUTE_R1, r1), (ROUTE_R2, r2),
                     (ROUTE_G1, g1), (ROUTE_G2, g2)):
        rec = jnp.where(lane == pos, val, rec)
    o_ref[...] = rec


def route_tokens(x, g, w_router, tm):
    m, d = x.shape
    tri = (jnp.arange(tm)[:, None] > jnp.arange(tm)[None, :]).astype(BF16)
    return pl.pallas_call(
        _router_kernel,
        grid=(m // tm,),
        in_specs=[
            pl.BlockSpec((tm, d), lambda i: (i, 0)),
            pl.BlockSpec((1, d), lambda i: (0, 0)),
            pl.BlockSpec((d, LANES), lambda i: (0, 0)),
            pl.BlockSpec((tm, tm), lambda i: (0, 0)),
        ],
        out_specs=pl.BlockSpec((tm, LANES), lambda i: (i, 0)),
        out_shape=jax.ShapeDtypeStruct((m, LANES), F32),
        compiler_params=_params("parallel"),
        name="route_tokens",
    )(x, g, w_router, tri)


def _row_copy(src, src_row, dst, dst_row, sem):
    return pltpu.make_async_copy(src.at[pl.ds(src_row, 1)], dst.at[pl.ds(dst_row, 1)], sem)


def _scatter_rows_kernel(dest_ref, x_ref, o_hbm, sem):
    tm = x_ref.shape[0]

    def issue(r, carry):
        for k in range(2):
            _row_copy(x_ref, r, o_hbm, dest_ref[0, 0, 2 * r + k], sem).start(priority=k)
        return carry

    def drain(r, carry):
        for k in range(2):
            _row_copy(x_ref, 0, o_hbm, 0, sem).wait()
        return carry

    lax.fori_loop(0, tm, issue, 0, unroll=8)
    lax.fori_loop(0, tm, drain, 0, unroll=8)


def scatter_rows(x, dest, tm):
    m, d = x.shape
    dest3 = dest.reshape(m // tm, 1, 2 * tm)
    return pl.pallas_call(
        _scatter_rows_kernel,
        grid=(m // tm,),
        in_specs=[
            pl.BlockSpec((1, 1, 2 * tm), lambda i: (i, 0, 0), memory_space=pltpu.SMEM),
            pl.BlockSpec((tm, d), lambda i: (i, 0)),
        ],
        out_specs=pl.BlockSpec(memory_space=pl.ANY),
        out_shape=jax.ShapeDtypeStruct((2 * m, d), x.dtype),
        scratch_shapes=[pltpu.SemaphoreType.DMA(())],
        compiler_params=_params("arbitrary"),
        name="scatter_rows",
    )(dest3, x)


def _expert_kernel(tile_ref, exp_ref, lo_ref, hi_ref, nvis_ref, x_ref, g_ref, wg_ref, wu_ref, wd_ref, o_ref,
                   hn_ref, acc_ref):
    v = pl.program_id(0)
    j = pl.program_id(1)

    @pl.when(v < nvis_ref[0])
    def _():
        @pl.when(j == 0)
        def _():
            hn_ref[...] = _rmsnorm(x_ref[...], g_ref[...]).astype(BF16)
            acc_ref[...] = jnp.zeros_like(acc_ref)

        hn = hn_ref[...]
        a = jnp.dot(hn, wg_ref[...], preferred_element_type=F32)
        b = jnp.dot(hn, wu_ref[...], preferred_element_type=F32)
        h = (a * _sigmoid(a) * b).astype(BF16)
        acc_ref[...] += jnp.dot(h, wd_ref[...], preferred_element_type=F32)

        @pl.when(j == pl.num_programs(1) - 1)
        def _():
            row = lax.broadcasted_iota(jnp.int32, (o_ref.shape[0], 1), 0)
            mine = jnp.logical_and(row >= lo_ref[v], row < hi_ref[v])
            first = jnp.logical_or(v == 0, tile_ref[v] != tile_ref[jnp.maximum(v - 1, 0)])

            @pl.when(first)
            def _():
                o_ref[...] = jnp.where(mine, acc_ref[...], 0.0)

            @pl.when(jnp.logical_not(first))
            def _():
                o_ref[...] = jnp.where(mine, acc_ref[...], o_ref[...])


def expert_swiglu(xs, g, w_gate, w_up, w_down, visits, tm, tf):
    m, d = xs.shape
    f = w_gate.shape[2]
    n_f = f // tf
    assert m % tm == 0 and f % tf == 0
    tile, expert, lo, hi, n_visit = visits

    def f_blk(v, j):
        return j + (v % 2) * (n_f - 1 - 2 * j)

    grid_spec = pltpu.PrefetchScalarGridSpec(
        num_scalar_prefetch=5,
        grid=(tile.shape[0], n_f),
        in_specs=[
            pl.BlockSpec((tm, d), lambda v, j, t, e, lo, hi, n: (t[v], 0)),
            pl.BlockSpec((1, d), lambda v, j, t, e, lo, hi, n: (0, 0)),
            pl.BlockSpec((None, d, tf), lambda v, j, t, e, lo, hi, n: (e[v], 0, f_blk(v, j))),
            pl.BlockSpec((None, d, tf), lambda v, j, t, e, lo, hi, n: (e[v], 0, f_blk(v, j))),
            pl.BlockSpec((None, tf, d), lambda v, j, t, e, lo, hi, n: (e[v], f_blk(v, j), 0)),
        ],
        out_specs=pl.BlockSpec((tm, d), lambda v, j, t, e, lo, hi, n: (t[v], 0)),
        scratch_shapes=[pltpu.VMEM((tm, d), BF16), pltpu.VMEM((tm, d), F32)],
    )
    return pl.pallas_call(
        _expert_kernel,
        grid_spec=grid_spec,
        out_shape=jax.ShapeDtypeStruct((m, d), F32),
        compiler_params=_params("arbitrary", "arbitrary"),
        name="expert_swiglu",
    )(tile, expert, lo, hi, n_visit, xs, g, w_gate, w_up, w_down)


def _combine_kernel(dest_ref, x_ref, rec_ref, gf_ref, ys_hbm, o_ref, y1_ref, y2_ref, sem):
    tm = x_ref.shape[0]
    bufs = (y1_ref, y2_ref)

    def issue(r, carry):
        for k in range(2):
            _row_copy(ys_hbm, dest_ref[0, 0, 2 * r + k], bufs[k], r, sem).start(priority=k)
        return carry

    def drain(r, carry):
        for k in range(2):
            _row_copy(ys_hbm, 0, bufs[k], 0, sem).wait()
        return carry

    lax.fori_loop(0, tm, issue, 0, unroll=8)
    lax.fori_loop(0, tm, drain, 0, unroll=8)
    g1 = rec_ref[:, ROUTE_G1:ROUTE_G1 + 1]
    g2 = rec_ref[:, ROUTE_G2:ROUTE_G2 + 1]
    out = x_ref[...] + (g1 * y1_ref[...] + g2 * y2_ref[...])
    o_ref[...] = _rmsnorm(out, gf_ref[...])


def combine_rows(x, rec, dest, ys, g_final, tm):
    m, d = x.shape
    dest3 = dest.reshape(m // tm, 1, 2 * tm)
    return pl.pallas_call(
        _combine_kernel,
        grid=(m // tm,),
        in_specs=[
            pl.BlockSpec((1, 1, 2 * tm), lambda i: (i, 0, 0), memory_space=pltpu.SMEM),
            pl.BlockSpec((tm, d), lambda i: (i, 0)),
            pl.BlockSpec((tm, LANES), lambda i: (i, 0)),
            pl.BlockSpec((1, d), lambda i: (0, 0)),
            pl.BlockSpec(memory_space=pl.ANY),
        ],
        out_specs=pl.BlockSpec((tm, d), lambda i: (i, 0)),
        out_shape=jax.ShapeDtypeStruct((m, d), F32),
        scratch_shapes=[pltpu.VMEM((tm, d), F32), pltpu.VMEM((tm, d), F32), pltpu.SemaphoreType.DMA(())],
        compiler_params=_params("arbitrary"),
        name="combine_rows",
    )(dest3, x, rec, g_final, ys)


def _visit_list(rec, tile_tokens, tm):
    n = rec.shape[0]
    e = rec[:, ROUTE_E1:ROUTE_E2 + 1].astype(jnp.int32)
    rank = rec[:, ROUTE_R1:ROUTE_R2 + 1].astype(jnp.int32)
    onehot = (e[:, :, None] == jnp.arange(N_EXPERTS)[None, None, :]).astype(jnp.int32)
    per_tile = onehot.reshape(n // tile_tokens, tile_tokens * 2, N_EXPERTS).sum(axis=1)
    count = per_tile.sum(axis=0)
    start = jnp.cumsum(count) - count
    tile_base = jnp.cumsum(per_tile, axis=0) - per_tile
    base = (start[None, :] + tile_base)[:, None, :]
    dest = rank + jnp.sum(onehot.reshape(n // tile_tokens, tile_tokens * 2, N_EXPERTS) * base, axis=-1).reshape(n, 2)

    n_tile = 2 * n // tm
    n_visit_max = n_tile + N_EXPERTS
    end = start + count
    first_tile = start // tm
    last_tile = jnp.maximum(end - 1, start) // tm
    n_vis = jnp.where(count > 0, last_tile - first_tile + 1, 0)
    vis_end = jnp.cumsum(n_vis)
    vis_start = vis_end - n_vis
    total = vis_end[-1]
    v = jnp.arange(n_visit_max)
    vc = jnp.minimum(v, total - 1)
    ex = jnp.sum(vc[:, None] >= vis_end[None, :], axis=1)
    tile = first_tile[ex] + vc - vis_start[ex]
    lo = jnp.clip(start[ex] - tile * tm, 0, tm)
    hi = jnp.clip(end[ex] - tile * tm, 0, tm)
    live = v < total
    as_i32 = lambda a: a.astype(jnp.int32)
    visits = (as_i32(tile), as_i32(ex), as_i32(jnp.where(live, lo, 0)), as_i32(jnp.where(live, hi, 0)),
              as_i32(total).reshape(1))
    return dest.astype(jnp.int32), visits


def routed_experts(x, g, w_router, w_gate, w_up, w_down, g_final):
    rec = route_tokens(x, g, w_router, 1024)
    dest, visits = _visit_list(rec, 1024, EXPERT_TILE)
    xs = scatter_rows(x, dest, 512)
    ys = expert_swiglu(xs, g, w_gate, w_up, w_down, visits, EXPERT_TILE, w_gate.shape[2] // 2)
    return combine_rows(x, rec, dest, ys, g_final, 256)


def _swiglu_kernel(*refs, n_expert, final_norm):
    x_ref, g_ref = refs[0], refs[1]
    k = 2
    if n_expert > 1:
        wr_ref = refs[k]
        k += 1
    wg_ref, wu_ref, wd_ref = refs[k:k + 3]
    k += 3
    if final_norm:
        gf_ref = refs[k]
        k += 1
    o_ref, hn_ref, acc_ref = refs[k:k + 3]
    if n_expert > 1:
        gate_ref = refs[k + 3]
    e = pl.program_id(1)
    j = pl.program_id(2)

    @pl.when(jnp.logical_and(e == 0, j == 0))
    def _():
        hn = _rmsnorm(x_ref[...], g_ref[...])
        hn_ref[...] = hn.astype(BF16)
        acc_ref[...] = jnp.zeros_like(acc_ref)
        if n_expert > 1:
            logits = jnp.dot(hn.astype(BF16), wr_ref[...], preferred_element_type=F32)
            gate_ref[...] = _top2_gates(logits)

    hn = hn_ref[...]
    a = jnp.dot(hn, wg_ref[...], preferred_element_type=F32)
    b = jnp.dot(hn, wu_ref[...], preferred_element_type=F32)
    h = (a * _sigmoid(a) * b).astype(BF16)
    y = jnp.dot(h, wd_ref[...], preferred_element_type=F32)
    if n_expert > 1:
        lane = lax.broadcasted_iota(jnp.int32, gate_ref.shape, 1)
        gate = jnp.sum(jnp.where(lane == e, gate_ref[...], 0.0), axis=-1, keepdims=True)
        y = gate * y
    acc_ref[...] += y

    @pl.when(jnp.logical_and(e == n_expert - 1, j == pl.num_programs(2) - 1))
    def _():
        out = x_ref[...] + acc_ref[...]
        if final_norm:
            out = _rmsnorm(out, gf_ref[...])
        o_ref[...] = out


def swiglu_block(x, g, w_gate, w_up, w_down, tm, tf, w_router=None, g_final=None):
    m, d = x.shape
    n_expert, _, f = w_gate.shape
    assert m % tm == 0 and f % tf == 0
    args = [x, g]
    in_specs = [pl.BlockSpec((tm, d), lambda i, e, j: (i, 0)), pl.BlockSpec((1, d), lambda i, e, j: (0, 0))]
    if n_expert > 1:
        args.append(w_router)
        in_specs.append(pl.BlockSpec((d, LANES), lambda i, e, j: (0, 0)))
    args += [w_gate, w_up, w_down]
    in_specs += [
        pl.BlockSpec((None, d, tf), lambda i, e, j: (e, 0, j)),
        pl.BlockSpec((None, d, tf), lambda i, e, j: (e, 0, j)),
        pl.BlockSpec((None, tf, d), lambda i, e, j: (e, j, 0)),
    ]
    if g_final is not None:
        args.append(g_final)
        in_specs.append(pl.BlockSpec((1, d), lambda i, e, j: (0, 0)))
    scratch = [pltpu.VMEM((tm, d), BF16), pltpu.VMEM((tm, d), F32)]
    if n_expert > 1:
        scratch.append(pltpu.VMEM((tm, LANES), F32))
    kern = functools.partial(_swiglu_kernel, n_expert=n_expert, final_norm=g_final is not None)
    return pl.pallas_call(
        kern,
        grid=(m // tm, n_expert, f // tf),
        in_specs=in_specs,
        out_specs=pl.BlockSpec((tm, d), lambda i, e, j: (i, 0)),
        out_shape=jax.ShapeDtypeStruct((m, d), F32),
        scratch_shapes=scratch,
        compiler_params=_params("parallel", "arbitrary", "arbitrary"),
        name="swiglu_block",
    )(*args)


def _ln_silu_pw2(z, lng_ref, lnb_ref, w2_ref, b2_ref):
    mu = jnp.mean(z, axis=-1, keepdims=True)
    zc = z - mu
    var = jnp.mean(zc * zc, axis=-1, keepdims=True)
    zn = zc * lax.rsqrt(var + LN_EPS) * lng_ref[...] + lnb_ref[...]
    act = (zn * _sigmoid(zn)).astype(BF16)
    return jnp.dot(act, w2_ref[...], preferred_element_type=F32) + b2_ref[...]


def _conv_kernel(glu_ref, halo_ref, x_ref, wdw_ref, bdw_ref, lng_ref, lnb_ref, w2_ref, b2_ref, o_ref,
                 ext_ref, z_ref, *, tiles_per_seq, row_chunk):
    i = pl.program_id(0)
    tm = glu_ref.shape[0]
    halo = halo_ref.shape[0]
    n_slab = glu_ref.shape[1] // LANES
    at_start = i % tiles_per_seq == 0
    for cb in range(n_slab):
        cols = slice(cb * LANES, (cb + 1) * LANES)
        ext_ref[cb, 0:halo, :] = jnp.where(at_start, 0.0, _round_bf16(halo_ref[:, cols]))
        ext_ref[cb, halo:halo + tm, :] = _round_bf16(glu_ref[:, cols])
    base = halo - (CONV_K - 1)

    def slab(cb, carry):
        for rc in range(tm // row_chunk):
            r0 = rc * row_chunk
            acc = jnp.zeros((row_chunk, LANES), F32) + bdw_ref[cb]
            for tap in range(CONV_K):
                acc = acc + ext_ref[cb, r0 + base + tap:r0 + base + tap + row_chunk, :] * wdw_ref[cb, tap:tap + 1, :]
            z_ref[cb, r0:r0 + row_chunk, :] = acc
        return carry

    lax.fori_loop(0, n_slab, slab, 0)
    z = jnp.concatenate([z_ref[cb] for cb in range(n_slab)], axis=-1)
    o_ref[...] = x_ref[...] + _ln_silu_pw2(z, lng_ref, lnb_ref, w2_ref, b2_ref)


def conv_block(glu, x, w_dw, b_dw, ln_g, ln_b, w_pw2, b_pw2, tm, seq):
    m, d = x.shape
    halo = 32
    n_slab = d // LANES
    assert seq % tm == 0 and tm % halo == 0
    kern = functools.partial(_conv_kernel, tiles_per_seq=seq // tm, row_chunk=64)
    vec = pl.BlockSpec((1, d), lambda i: (0, 0))
    w_dw_slab = jnp.transpose(w_dw.reshape(w_dw.shape[0], n_slab, LANES), (1, 0, 2))
    b_dw_slab = b_dw.reshape(n_slab, 1, LANES)
    return pl.pallas_call(
        kern,
        grid=(m // tm,),
        in_specs=[
            pl.BlockSpec((tm, d), lambda i: (i, 0)),
            pl.BlockSpec((halo, d), lambda i: (jnp.maximum(i * (tm // halo) - 1, 0), 0)),
            pl.BlockSpec((tm, d), lambda i: (i, 0)),
            pl.BlockSpec(w_dw_slab.shape, lambda i: (0, 0, 0)),
            pl.BlockSpec(b_dw_slab.shape, lambda i: (0, 0, 0)),
            vec, vec,
            pl.BlockSpec((d, d), lambda i: (0, 0)),
            vec,
        ],
        out_specs=pl.BlockSpec((tm, d), lambda i: (i, 0)),
        out_shape=jax.ShapeDtypeStruct((m, d), F32),
        scratch_shapes=[pltpu.VMEM((n_slab, tm + halo, LANES), F32), pltpu.VMEM((n_slab, tm, LANES), F32)],
        compiler_params=_params("parallel"),
        name="conv_block",
    )(glu, glu, x, w_dw_slab, b_dw_slab, ln_g, ln_b, w_pw2, b_pw2)


def _conv_sample_kernel(ext_ref, x_ref, wdw_ref, bdw_ref, lng_ref, lnb_ref, w2_ref, b2_ref, o_ref, z_ref,
                        *, steps):
    n = ext_ref.shape[0]
    z_ref[...] = jnp.zeros_like(z_ref)

    def one(s, carry):
        acc = jnp.zeros((steps, z_ref.shape[2]), F32) + bdw_ref[...]
        for tap in range(CONV_K):
            acc = acc + _round_bf16(ext_ref[s, tap:tap + steps, :]) * wdw_ref[tap:tap + 1, :]
        z_ref[s, 0:steps, :] = acc
        return carry

    lax.fori_loop(0, n, one, 0)
    z = z_ref[...].reshape(n * z_ref.shape[1], z_ref.shape[2])
    y = _ln_silu_pw2(z, lng_ref, lnb_ref, w2_ref, b2_ref)
    o_ref[...] = x_ref[...] + y.reshape(o_ref.shape)


def conv_block_sample(ext, x, w_dw, b_dw, ln_g, ln_b, w_pw2, b_pw2, steps):
    n, _, d = ext.shape
    kern = functools.partial(_conv_sample_kernel, steps=steps)
    return pl.pallas_call(
        kern,
        out_shape=jax.ShapeDtypeStruct((n, 8, d), F32),
        scratch_shapes=[pltpu.VMEM((n, 8, d), F32)],
        compiler_params=pltpu.CompilerParams(vmem_limit_bytes=VMEM_LIMIT_BYTES),
        name="conv_block_sample",
    )(ext, x, w_dw, b_dw, ln_g, ln_b, w_pw2, b_pw2)


def _sample_attn_kernel(q_ref, kn_ref, vn_ref, kvt_ref, c0_ref, c1_ref, c2_ref, b0_ref, b1_ref, b2_ref, bn_ref,
                        ext_ref, a_ref, mix_ref, o0_ref, o1_ref, o2_ref, *, steps):
    caches = (c0_ref, c1_ref, c2_ref)
    biases = (b0_ref, b1_ref, b2_ref)
    new_caches = (o0_ref, o1_ref, o2_ref)
    n_head = q_ref.shape[2]
    lane = lax.broadcasted_iota(jnp.int32, (n_head, A_HEAD_DIM, LANES), 2)
    keep = LANES - steps

    outs, lses = [], []
    for branch in range(N_BRANCH):
        c_ref = caches[branch]
        q = (q_ref[0, branch] * (A_HEAD_DIM ** -0.5)).astype(BF16)
        kt = c_ref[0, 0].astype(BF16)
        vt = c_ref[0, 1].astype(BF16)
        kn = kn_ref[0, branch].astype(BF16)
        vn = vn_ref[0, branch].astype(BF16)
        s_c = jnp.einsum("hqd,hdl->hql", q, kt, preferred_element_type=F32) + biases[branch][...]
        s_n = jnp.einsum("hqd,hkd->hqk", q, kn, preferred_element_type=F32) + bn_ref[branch]
        mx = jnp.maximum(jnp.max(s_c, axis=-1, keepdims=True), jnp.max(s_n, axis=-1, keepdims=True))
        den = (jnp.sum(jnp.exp(s_c - mx), axis=-1, keepdims=True)
               + jnp.sum(jnp.exp(s_n - mx), axis=-1, keepdims=True))
        lse = mx + jnp.log(den)
        p_c = jnp.exp(s_c - lse).astype(BF16)
        p_n = jnp.exp(s_n - lse).astype(BF16)
        outs.append(jnp.einsum("hql,hdl->hqd", p_c, vt, preferred_element_type=F32)
                    + jnp.einsum("hqk,hkd->hqd", p_n, vn, preferred_element_type=F32))
        lses.append(lse)
    mx = jnp.maximum(jnp.maximum(lses[0], lses[1]), lses[2])
    wts = [jnp.exp(lse - mx) for lse in lses]
    inv = 1.0 / (wts[0] + wts[1] + wts[2])
    a_ref[0] = outs[0] * (wts[0] * inv) + outs[1] * (wts[1] * inv) + outs[2] * (wts[2] * inv)

    for branch in range(N_BRANCH):
        c_ref, o_ref = caches[branch], new_caches[branch]
        n_tile = c_ref.shape[-1] // LANES
        for kv in range(2):
            nxt = pltpu.roll(c_ref[0, kv, :, :, 0:LANES], keep, 2)
            for j in range(n_tile):
                cur = nxt
                if j + 1 < n_tile:
                    nxt = pltpu.roll(c_ref[0, kv, :, :, (j + 1) * LANES:(j + 2) * LANES], keep, 2)
                else:
                    nxt = kvt_ref[0, branch, kv]
                o_ref[0, kv, :, :, j * LANES:(j + 1) * LANES] = jnp.where(lane < keep, cur, nxt)

    base = ext_ref.shape[1] - steps
    for t in range(steps):
        for grp, win in enumerate(POOL_WINDOWS):
            cols = slice(grp * POOL_GROUP_DIM, (grp + 1) * POOL_GROUP_DIM)
            last = base + t
            tot = jnp.sum(ext_ref[0, last - win + 1:last + 1, cols], axis=0, keepdims=True)
            mix_ref[0, t:t + 1, cols] = tot / float(win) - ext_ref[0, last:last + 1, cols]


def sample_attention(q, kn, vn, kvt, caches, biases, bias_new, ext_pool, steps):
    n, rows = q.shape[0], q.shape[3]
    hg = A_HEADS // 2
    per_head = (hg, rows, A_HEAD_DIM)
    in_specs = [pl.BlockSpec((1, N_BRANCH) + per_head, lambda i, j: (i, 0, j, 0, 0))] * 3
    in_specs.append(pl.BlockSpec((1, N_BRANCH, 2, hg, A_HEAD_DIM, LANES), lambda i, j: (i, 0, 0, j, 0, 0)))
    cache_specs = [pl.BlockSpec((1, 2, hg, A_HEAD_DIM, c.shape[-1]), lambda i, j: (i, 0, j, 0, 0)) for c in caches]
    in_specs += cache_specs
    in_specs += [pl.BlockSpec((hg, rows, b.shape[-1]), lambda i, j: (j, 0, 0)) for b in biases]
    in_specs += [
        pl.BlockSpec((N_BRANCH, hg, rows, rows), lambda i, j: (0, j, 0, 0)),
        pl.BlockSpec((1,) + ext_pool.shape[1:], lambda i, j: (i, 0, 0)),
    ]
    kern = functools.partial(_sample_attn_kernel, steps=steps)
    return pl.pallas_call(
        kern,
        grid=(n, A_HEADS // hg),
        in_specs=in_specs,
        out_specs=[
            pl.BlockSpec((1,) + per_head, lambda i, j: (i, j, 0, 0)),
            pl.BlockSpec((1, steps, POOL_WIDTH), lambda i, j: (i, 0, 0)),
        ] + cache_specs,
        out_shape=[
            jax.ShapeDtypeStruct((n, A_HEADS, rows, A_HEAD_DIM), F32),
            jax.ShapeDtypeStruct((n, steps, POOL_WIDTH), F32),
        ] + [jax.ShapeDtypeStruct(c.shape, F32) for c in caches],
        compiler_params=_params("parallel", "arbitrary"),
        name="sample_attention",
    )(q, kn, vn, kvt, *caches, *biases, bias_new, ext_pool)


def _t5_bucket(dist):
    max_exact = NUM_BUCKETS // 2
    d = jnp.maximum(dist, 1).astype(F32)
    large = max_exact + (jnp.log(d / max_exact) / math.log(REL_MAX_DIST / max_exact)
                         * (NUM_BUCKETS - max_exact)).astype(jnp.int32)
    return jnp.where(dist < max_exact, dist, jnp.minimum(large, NUM_BUCKETS - 1))


def _slot_bias(rel_bias):
    rb = rel_bias.astype(F32).reshape(NUM_BUCKETS, N_BRANCH, A_HEADS)
    slot = jnp.arange(BAND + 1)
    out = []
    for g, (_, dil) in enumerate(A_GROUPS):
        pick = jax.nn.one_hot(_t5_bucket(slot * dil), NUM_BUCKETS, dtype=F32)
        out.append(jnp.einsum("sb,bh->hs", pick, rb[:, g], precision=lax.Precision.HIGHEST))
    return jnp.stack(out)


def _sliding_rows(v, n_rows, width):
    period = v.shape[-1]
    assert period >= n_rows + width - 1
    flat = jnp.tile(v, (1,) * (v.ndim - 1) + (n_rows + 1,))[..., :(period + 1) * n_rows]
    return flat.reshape(v.shape[:-1] + (n_rows, period + 1))[..., :width]


def _prompt_bias_tables(rel_bias):
    sb = _slot_bias(rel_bias)
    lead = sb.shape[:2]
    diag = jnp.concatenate([jnp.full(lead + (BAND - 1,), NEG, F32), sb[..., ::-1],
                            jnp.full(lead + (BAND,), NEG, F32)], axis=-1)
    return _sliding_rows(diag, BAND, 2 * BAND)[:, :, ::-1]


def _sample_bias_tables(rel_bias, steps, rows):
    sb = _slot_bias(rel_bias)
    cache_tables = []
    for g, (window, dil) in enumerate(A_GROUPS):
        spread = jnp.concatenate([sb[g][..., None], jnp.full(sb[g].shape + (dil - 1,), NEG, F32)], axis=-1)
        by_dist = spread.reshape(A_HEADS, (BAND + 1) * dil)
        need = window + rows
        by_dist = jnp.pad(by_dist, ((0, 0), (0, max(need - by_dist.shape[1], 0))), constant_values=NEG)[:, :need]
        by_dist = jnp.where(jnp.arange(need)[None, :] <= window, by_dist, NEG)
        cache_tables.append(_sliding_rows(by_dist[:, ::-1], rows, window)[:, ::-1])
    t_q = jnp.arange(rows)[:, None]
    t_k = jnp.arange(rows)[None, :]
    new_tables = []
    for g, (_, dil) in enumerate(A_GROUPS):
        diff = t_q - t_k
        ok = (diff >= 0) & (diff % dil == 0) & (t_k < steps)
        pick = jax.nn.one_hot(jnp.where(ok, diff // dil, BAND + 1).reshape(-1), BAND + 1, dtype=F32)
        vals = jnp.einsum("ps,hs->hp", pick, sb[g], precision=lax.Precision.HIGHEST).reshape(A_HEADS, rows, rows)
        new_tables.append(jnp.where(ok[None], vals, NEG))
    return cache_tables, jnp.stack(new_tables)


def kernel(x_prompt, x_sample, cache_kv_w128, cache_kv_w512, cache_kv_w2048, state_pool, state_conv, norm_mix0, w_in0, rel_bias, w_pool, pool_scale, w_out0, norm_ffn0, w_ff_gate, w_ff_up, w_ff_down, norm_mix1, w_pw1, b_pw1, w_dw, b_dw, ln_g, ln_b, w_pw2, b_pw2, norm_ffn1, w_router, we_gate, we_up, we_down, norm_final):
    batch, seq, d = x_prompt.shape
    n_s, steps, _ = x_sample.shape
    caches_in = (cache_kv_w128, cache_kv_w512, cache_kv_w2048)
    assert norm_mix0.shape[0] == 1 and norm_mix1.shape[0] == 1, "two layers: one mixer of each kind"
    assert seq % CHUNK == 0 and steps <= min(dil for _, dil in A_GROUPS[1:])
    for c, (window, _) in zip(caches_in, A_GROUPS):
        assert c.shape[2] == window, "cache must hold a full window"
    qkv_w = N_BRANCH * 3 * A_WIDTH

    w_in = w_in0[0]
    w_in_bf = w_in.astype(BF16)
    w_kv = w_in_bf[:, A_WIDTH * N_BRANCH:qkv_w]
    g_mix0, g_ffn0 = norm_mix0, norm_ffn0
    g_mix1, g_ffn1 = norm_mix1, norm_ffn1
    g_final = norm_final.reshape(1, d)
    wpool_bf = w_pool[0].astype(BF16)
    w_out_a = w_out0[0, :A_WIDTH].astype(BF16)
    w_out_p = w_out0[0, A_WIDTH:].astype(BF16)
    wff = (w_ff_gate.astype(BF16), w_ff_up.astype(BF16), w_ff_down.astype(BF16))
    w_pw1_bf = w_pw1[0].astype(BF16)
    w_pw2_bf = w_pw2[0].astype(BF16)
    w_dw_pad = jnp.pad(w_dw[0], ((0, 32 - CONV_K), (0, 0)))
    wex = (we_gate[0].astype(BF16), we_up[0].astype(BF16), we_down[0].astype(BF16))
    w_router_pad = jnp.pad(w_router[0], ((0, 0), (0, LANES - N_EXPERTS))).astype(BF16)
    zeros = lambda n: jnp.zeros((1, n), F32)

    xp = x_prompt.reshape(batch * seq, d)
    qkv, u_p = qkv_perm(xp, g_mix0, w_in_bf)
    a_p = band_attention(qkv, _prompt_bias_tables(rel_bias), batch, seq)
    h_p = mix_out(a_p, u_p, xp, wpool_bf, pool_scale, w_out_a, w_out_p, 512, seq)
    h_p = swiglu_block(h_p, g_ffn0, *wff, tm=512, tf=1408)
    glu_p = rms_glu(h_p, g_mix1, w_pw1_bf, b_pw1, 1024, 512)
    h_p = conv_block(glu_p, h_p, w_dw_pad, b_dw, ln_g, ln_b, w_pw2_bf, b_pw2, 256, seq)
    y_p = routed_experts(h_p, g_ffn1, w_router_pad, *wex, g_final)

    keep = A_GROUPS[-1][0]
    kv_tail = rms_matmul_t(x_prompt, g_mix0, w_kv.T, seq - keep, keep, 1024, 512)
    kv_tail = kv_tail.reshape(batch, 2, N_BRANCH, A_HEADS, A_HEAD_DIM, keep)
    kv_p = [jnp.transpose(kv_tail[:, :, g, :, :, keep - window:], (0, 4, 1, 2, 3))[None]
            for g, (window, _) in enumerate(A_GROUPS)]
    u_p3 = u_p.reshape(batch, seq, POOL_WIDTH)
    pool_p = u_p3[:, seq - POOL_BUF:][None]
    conv_p = glu_p.reshape(batch, seq, d)[:, seq - (CONV_K - 1):][None]

    m_s = n_s * steps
    xs = x_sample.reshape(m_s, d)
    proj = rms_matmul(xs, g_mix0, w_in_bf, zeros(w_in_bf.shape[1]), m_s, 512, F32)
    qkv_s = proj[:, :qkv_w].reshape(n_s, steps, 3, N_BRANCH, A_HEADS, A_HEAD_DIM)
    u_s = proj[:, qkv_w:].reshape(n_s, steps, POOL_WIDTH)
    rows = 8
    per_head = jnp.transpose(qkv_s, (2, 0, 3, 4, 1, 5))
    per_head = jnp.pad(per_head, ((0, 0),) * 4 + ((0, rows - steps), (0, 0)))
    new_cols = jnp.transpose(qkv_s[:, :, 1:], (0, 3, 2, 4, 5, 1))
    new_cols = jnp.pad(new_cols, ((0, 0),) * 5 + ((LANES - steps, 0),))
    caches_t = [jnp.transpose(c[0], (0, 2, 3, 4, 1)) for c in caches_in]
    ext_pool = jnp.concatenate([state_pool[0], u_s], axis=1)
    cache_bias, new_bias = _sample_bias_tables(rel_bias, steps, rows)
    a_s, mix_s, *caches_out = sample_attention(per_head[0], per_head[1], per_head[2], new_cols, caches_t,
                                               cache_bias, new_bias, ext_pool, steps)
    a_s = jnp.transpose(a_s[:, :, :steps], (0, 2, 1, 3)).reshape(m_s, A_WIDTH)
    h_s = mix_out_sample(a_s, mix_s.reshape(m_s, POOL_WIDTH), xs, wpool_bf, pool_scale, w_out_a, w_out_p)
    h_s = swiglu_block(h_s, g_ffn0, *wff, tm=m_s, tf=1408)
    glu_s = rms_glu(h_s, g_mix1, w_pw1_bf, b_pw1, m_s, 512)
    ext_conv = jnp.concatenate([state_conv[0], glu_s.reshape(n_s, steps, d)], axis=1)
    ext_conv_pad = jnp.pad(ext_conv, ((0, 0), (0, 40 - ext_conv.shape[1]), (0, 0)))
    h_s3 = jnp.pad(h_s.reshape(n_s, steps, d), ((0, 0), (0, 8 - steps), (0, 0)))
    h_s = conv_block_sample(ext_conv_pad, h_s3, w_dw_pad, b_dw, ln_g, ln_b, w_pw2_bf, b_pw2, steps)
    h_s = h_s[:, :steps].reshape(m_s, d)
    y_s = swiglu_block(h_s, g_ffn1, *wex, tm=m_s, tf=896, w_router=w_router_pad, g_final=g_final)

    kv_s = [jnp.transpose(c, (0, 4, 1, 2, 3))[None] for c in caches_out]
    pool_s = ext_pool[:, steps:][None]
    conv_s = ext_conv[:, steps:][None]

    return (y_p.reshape(batch, seq, d), y_s.reshape(n_s, steps, d), kv_p[0], kv_p[1], kv_p[2], pool_p, conv_p,
            kv_s[0], kv_s[1], kv_s[2], pool_s, conv_s)
```

```python
import functools
import math

import jax
import jax.numpy as jnp
from jax import lax
from jax.experimental import pallas as pl
from jax.experimental.pallas import tpu as pltpu

F32 = jnp.float32
BF16 = jnp.bfloat16

A_GROUPS = ((128, 1), (512, 4), (2048, 16))
N_BRANCH = 3
A_HEADS = 8
A_HEAD_DIM = 64
A_WIDTH = 512
BAND = 128
NUM_BUCKETS = 32
REL_MAX_DIST = 2048
POOL_WINDOWS = (2, 4, 8, 16)
POOL_GROUP_DIM = 128
POOL_WIDTH = 512
POOL_BUF = 15
CONV_K = 31
N_EXPERTS = 8
RMS_EPS = 1e-6
LN_EPS = 1e-5
NEG = -1e30

VMEM_LIMIT_BYTES = 56 * 1024 * 1024
LANES = 128
CHUNK = 2048
GROUP = 8


def _params(*sem):
    return pltpu.CompilerParams(dimension_semantics=sem, vmem_limit_bytes=VMEM_LIMIT_BYTES)


def _rmsnorm(x, g):
    return x * lax.rsqrt(jnp.mean(x * x, axis=-1, keepdims=True) + RMS_EPS) * g


def _sigmoid(x):
    return 1.0 / (1.0 + jnp.exp(-x))


def _round_bf16(x):
    return x.astype(BF16).astype(F32)


def _rms_matmul_kernel(x_ref, g_ref, w_ref, b_ref, o_ref, hn_ref):
    @pl.when(pl.program_id(1) == 0)
    def _():
        hn_ref[...] = _rmsnorm(x_ref[...], g_ref[...]).astype(BF16)

    acc = jnp.dot(hn_ref[...], w_ref[...], preferred_element_type=F32)
    o_ref[...] = (acc + b_ref[...]).astype(o_ref.dtype)


def rms_matmul(x, g, w, b, tm, tn, out_dtype):
    m, k = x.shape
    n = w.shape[1]
    assert m % tm == 0 and n % tn == 0
    return pl.pallas_call(
        _rms_matmul_kernel,
        grid=(m // tm, n // tn),
        in_specs=[
            pl.BlockSpec((tm, k), lambda i, j: (i, 0)),
            pl.BlockSpec((1, k), lambda i, j: (0, 0)),
            pl.BlockSpec((k, tn), lambda i, j: (0, j)),
            pl.BlockSpec((1, tn), lambda i, j: (0, j)),
        ],
        out_specs=pl.BlockSpec((tm, tn), lambda i, j: (i, j)),
        out_shape=jax.ShapeDtypeStruct((m, n), out_dtype),
        scratch_shapes=[pltpu.VMEM((tm, k), BF16)],
        compiler_params=_params("parallel", "arbitrary"),
        name="rms_matmul",
    )(x, g, w, b)


def _rms_matmul_t_kernel(x_ref, g_ref, wt_ref, o_ref, hn_ref):
    @pl.when(pl.program_id(2) == 0)
    def _():
        hn_ref[...] = _rmsnorm(x_ref[0], g_ref[...]).astype(BF16)

    nt = (((1,), (1,)), ((), ()))
    o_ref[0] = lax.dot_general(wt_ref[...], hn_ref[...], nt, preferred_element_type=F32)


def rms_matmul_t(x, g, wt, row0, rows, tm, tn):
    batch, _, k = x.shape
    n = wt.shape[0]
    assert row0 % tm == 0 and rows % tm == 0 and n % tn == 0
    return pl.pallas_call(
        _rms_matmul_t_kernel,
        grid=(batch, rows // tm, n // tn),
        in_specs=[
            pl.BlockSpec((1, tm, k), lambda b, i, j: (b, row0 // tm + i, 0)),
            pl.BlockSpec((1, k), lambda b, i, j: (0, 0)),
            pl.BlockSpec((tn, k), lambda b, i, j: (j, 0)),
        ],
        out_specs=pl.BlockSpec((1, tn, tm), lambda b, i, j: (b, j, i)),
        out_shape=jax.ShapeDtypeStruct((batch, n, rows), F32),
        scratch_shapes=[pltpu.VMEM((tm, k), BF16)],
        compiler_params=_params("parallel", "parallel", "arbitrary"),
        name="rms_matmul_t",
    )(x, g, wt)


def _rms_glu_kernel(x_ref, g_ref, wv_ref, wg_ref, bv_ref, bg_ref, o_ref, hn_ref):
    @pl.when(pl.program_id(1) == 0)
    def _():
        hn_ref[...] = _rmsnorm(x_ref[...], g_ref[...]).astype(BF16)

    hn = hn_ref[...]
    val = jnp.dot(hn, wv_ref[...], preferred_element_type=F32) + bv_ref[...]
    gate = jnp.dot(hn, wg_ref[...], preferred_element_type=F32) + bg_ref[...]
    o_ref[...] = val * _sigmoid(gate)


def rms_glu(x, g, w, b, tm, tn):
    m, k = x.shape
    c = w.shape[1] // 2
    nb = c // tn
    return pl.pallas_call(
        _rms_glu_kernel,
        grid=(m // tm, nb),
        in_specs=[
            pl.BlockSpec((tm, k), lambda i, j: (i, 0)),
            pl.BlockSpec((1, k), lambda i, j: (0, 0)),
            pl.BlockSpec((k, tn), lambda i, j: (0, j)),
            pl.BlockSpec((k, tn), lambda i, j: (0, j + nb)),
            pl.BlockSpec((1, tn), lambda i, j: (0, j)),
            pl.BlockSpec((1, tn), lambda i, j: (0, j + nb)),
        ],
        out_specs=pl.BlockSpec((tm, tn), lambda i, j: (i, j)),
        out_shape=jax.ShapeDtypeStruct((m, c), F32),
        scratch_shapes=[pltpu.VMEM((tm, k), BF16)],
        compiler_params=_params("parallel", "arbitrary"),
        name="rms_glu",
    )(x, g, w, w, b, b)


PERM_BLOCK = 256


def _qkv_perm_kernel(x_ref, g_ref, w_ref, perm_ref, o_ref, u_ref, hn_ref):
    j = pl.program_id(1)
    n_qkv = 3 * N_BRANCH

    @pl.when(j == 0)
    def _():
        hn_ref[...] = _rmsnorm(x_ref[...], g_ref[...]).astype(BF16)

    def project():
        return jnp.dot(hn_ref[...], w_ref[...], preferred_element_type=F32)

    @pl.when(j == n_qkv)
    def _():
        u_ref[...] = project()

    @pl.when(j < 3)
    def _():
        o_ref[...] = project().astype(BF16)

    for branch in (1, 2):
        dil = A_GROUPS[branch][1]
        rows = CHUNK // dil
        per = PERM_BLOCK // dil

        @pl.when(jnp.logical_and(j // 3 == branch, j < n_qkv))
        def _():
            res_bf = project().astype(BF16)
            for b in range(CHUNK // PERM_BLOCK):
                blk = jnp.dot(perm_ref[branch - 1], res_bf[b * PERM_BLOCK:(b + 1) * PERM_BLOCK],
                              preferred_element_type=F32).astype(BF16)
                for r in range(dil):
                    o_ref[r * rows + b * per:r * rows + (b + 1) * per, :] = blk[r * per:(r + 1) * per]


def qkv_perm(x, g, w_in):
    m, k = x.shape
    tn = A_WIDTH
    n_qkv = 3 * N_BRANCH
    assert w_in.shape[1] == (n_qkv + 1) * tn
    perms = []
    for _, dil in A_GROUPS[1:]:
        p = jnp.arange(PERM_BLOCK)
        src = (p % (PERM_BLOCK // dil)) * dil + p // (PERM_BLOCK // dil)
        perms.append((src[:, None] == jnp.arange(PERM_BLOCK)[None, :]).astype(BF16))

    def w_block(i, j):
        return 0, jnp.where(j < n_qkv, (j % 3) * N_BRANCH + j // 3, n_qkv)

    return pl.pallas_call(
        _qkv_perm_kernel,
        grid=(m // CHUNK, n_qkv + 1),
        in_specs=[
            pl.BlockSpec((CHUNK, k), lambda i, j: (i, 0)),
            pl.BlockSpec((1, k), lambda i, j: (0, 0)),
            pl.BlockSpec((k, tn), w_block),
            pl.BlockSpec((N_BRANCH - 1, PERM_BLOCK, PERM_BLOCK), lambda i, j: (0, 0, 0)),
        ],
        out_specs=[
            pl.BlockSpec((CHUNK, tn), lambda i, j: (i, jnp.minimum(j, n_qkv - 1))),
            pl.BlockSpec((CHUNK, tn), lambda i, j: (i, 0)),
        ],
        out_shape=[jax.ShapeDtypeStruct((m, n_qkv * tn), BF16), jax.ShapeDtypeStruct((m, tn), F32)],
        scratch_shapes=[pltpu.VMEM((CHUNK, k), BF16)],
        compiler_params=_params("parallel", "arbitrary"),
        name="qkv_perm",
    )(x, g, w_in, jnp.stack(perms))


def _band_attn_kernel(*refs):
    qkv_refs = refs[:15]
    bias_ref, o_ref, acc_ref, lse_ref, kp_ref, vp_ref = refs[15:]
    c = pl.program_id(1)
    n_units = CHUNK // BAND
    grp_rows = GROUP * BAND
    head_of_lane = lax.broadcasted_iota(jnp.int32, (1, 1, LANES), 2) // A_HEAD_DIM
    key_is_prev = lax.broadcasted_iota(jnp.int32, (1, 1, 2 * BAND), 2) < BAND

    for branch, (_, dil) in enumerate(A_GROUPS):
        q_ref, kprev_ref, kcur_ref, vprev_ref, vcur_ref = qkv_refs[5 * branch:5 * branch + 5]
        nblk = n_units // dil
        span = nblk * BAND
        if nblk == 1:
            kp_src, vp_src = kprev_ref, vprev_ref
        else:
            kp_src, vp_src = kp_ref.at[branch], vp_ref.at[branch]
            for res in range(dil):
                lo = res * span
                for dst, prev, cur in ((kp_src, kprev_ref, kcur_ref), (vp_src, vprev_ref, vcur_ref)):
                    dst[lo:lo + BAND, :] = prev[lo + span - BAND:lo + span, :]
                    dst[lo + BAND:lo + span, :] = cur[lo:lo + span - BAND, :]

        def group(gi, carry, q_ref=q_ref, kcur_ref=kcur_ref, vcur_ref=vcur_ref, kp_src=kp_src, vp_src=vp_src,
                  nblk=nblk, dil=dil, branch=branch):
            row = pl.multiple_of(gi * grp_rows, grp_rows)

            def blocks(ref):
                return ref[pl.ds(row, grp_rows), :].reshape(GROUP, BAND, LANES)

            q = blocks(q_ref) * jnp.asarray(A_HEAD_DIM ** -0.5, BF16)
            kcat = jnp.concatenate([blocks(kp_src), blocks(kcur_ref)], axis=1)
            vcat = jnp.concatenate([blocks(vp_src), blocks(vcur_ref)], axis=1)
            unit = gi * GROUP + lax.broadcasted_iota(jnp.int32, (GROUP, 1, 1), 0)
            no_prev = jnp.logical_and(unit % nblk == 0, c == 0)
            pen = jnp.where(jnp.logical_and(no_prev, key_is_prev), NEG, 0.0)

            o = jnp.zeros((GROUP, BAND, LANES), F32)
            lse = jnp.zeros((GROUP, BAND, LANES), F32)
            for hh in range(2):
                mine = head_of_lane == hh
                kh = jnp.where(mine, kcat, jnp.zeros_like(kcat))
                vh = jnp.where(mine, vcat, jnp.zeros_like(vcat))
                s = jnp.einsum("uqd,ukd->uqk", q, kh, preferred_element_type=F32) + bias_ref[branch, hh][None] + pen
                mx = jnp.max(s, axis=-1, keepdims=True)
                p = jnp.exp(s - mx)
                den = jnp.sum(p, axis=-1, keepdims=True)
                pn = (p * (1.0 / den)).astype(BF16)
                o = o + jnp.einsum("uqk,ukd->uqd", pn, vh, preferred_element_type=F32)
                lse = jnp.where(mine, mx + jnp.log(den), lse)

            if dil == 1:
                acc_ref[branch, pl.ds(row, grp_rows), :] = o.reshape(grp_rows, LANES)
                lse_ref[branch, pl.ds(row, grp_rows), :] = lse.reshape(grp_rows, LANES)
            else:
                per = max(GROUP // nblk, 1)
                blocks_per = GROUP // per
                for k in range(per):
                    if nblk >= GROUP:
                        start = gi // (nblk // GROUP) + dil * BAND * ((gi % (nblk // GROUP)) * GROUP)
                    else:
                        start = gi * per + k
                    rows = pl.ds(start, blocks_per * BAND, stride=dil)
                    part = slice(k * blocks_per, (k + 1) * blocks_per)
                    acc_ref[branch, rows, :] = o[part].reshape(blocks_per * BAND, LANES)
                    lse_ref[branch, rows, :] = lse[part].reshape(blocks_per * BAND, LANES)
            return carry

        lax.fori_loop(0, n_units // GROUP, group, 0)

    l0, l1, l2 = lse_ref[0], lse_ref[1], lse_ref[2]
    mx = jnp.maximum(jnp.maximum(l0, l1), l2)
    w0, w1, w2 = jnp.exp(l0 - mx), jnp.exp(l1 - mx), jnp.exp(l2 - mx)
    inv = 1.0 / (w0 + w1 + w2)
    o_ref[...] = (acc_ref[0] * (w0 * inv) + acc_ref[1] * (w1 * inv) + acc_ref[2] * (w2 * inv)).astype(o_ref.dtype)


def band_attention(qkv, bias_tbl, batch, seq):
    n_chunk = seq // CHUNK
    n_pair = A_WIDTH // LANES
    blk = (CHUNK, LANES)
    in_specs = []
    for branch in range(N_BRANCH):
        qcol, kcol, vcol = [(3 * branch + kind) * n_pair for kind in range(3)]

        def cur(col):
            return lambda b, c, hp, col=col: (b * n_chunk + c, col + hp)

        def prev(col):
            return lambda b, c, hp, col=col: (b * n_chunk + jnp.maximum(c - 1, 0), col + hp)

        in_specs += [pl.BlockSpec(blk, cur(qcol)), pl.BlockSpec(blk, prev(kcol)), pl.BlockSpec(blk, cur(kcol)),
                     pl.BlockSpec(blk, prev(vcol)), pl.BlockSpec(blk, cur(vcol))]
    in_specs.append(pl.BlockSpec((N_BRANCH, 2, BAND, 2 * BAND), lambda b, c, hp: (0, hp, 0, 0)))
    return pl.pallas_call(
        _band_attn_kernel,
        grid=(batch, n_chunk, n_pair),
        in_specs=in_specs,
        out_specs=pl.BlockSpec(blk, lambda b, c, hp: (b * n_chunk + c, hp)),
        out_shape=jax.ShapeDtypeStruct((batch * seq, A_WIDTH), BF16),
        scratch_shapes=[pltpu.VMEM((N_BRANCH, CHUNK, LANES), F32)] * 2
        + [pltpu.VMEM((N_BRANCH - 1, CHUNK, LANES), BF16)] * 2,
        compiler_params=_params("parallel", "parallel", "arbitrary"),
        name="band_attention",
    )(*([qkv] * 15), bias_tbl)


def _mix_out_kernel(a_ref, u_ref, halo_ref, x_ref, wpool_ref, scale_ref, woa_ref, wop_ref, o_ref, ext_ref,
                    *, tiles_per_seq):
    i = pl.program_id(0)
    tm = a_ref.shape[0]
    halo = halo_ref.shape[0]
    seq_tile = i % tiles_per_seq
    ext_ref[0:halo, :] = jnp.where(seq_tile == 0, 0.0, halo_ref[...])
    ext_ref[halo:halo + tm, :] = u_ref[...]
    pos = seq_tile * tm + lax.broadcasted_iota(jnp.int32, (tm, POOL_GROUP_DIM), 0)
    acc = x_ref[...] + jnp.dot(a_ref[...], woa_ref[...], preferred_element_type=F32)
    for grp, win in enumerate(POOL_WINDOWS):
        cols = slice(grp * POOL_GROUP_DIM, (grp + 1) * POOL_GROUP_DIM)
        tot = ext_ref[halo:halo + tm, cols]
        for back in range(1, win):
            tot = tot + ext_ref[halo - back:halo - back + tm, cols]
        cnt = jnp.minimum(pos + 1, win).astype(F32)
        mix = tot / cnt - ext_ref[halo:halo + tm, cols]
        y = jnp.dot(mix.astype(BF16), wpool_ref[grp], preferred_element_type=F32) * scale_ref[:, cols]
        acc = acc + jnp.dot(y.astype(BF16), wop_ref[cols, :], preferred_element_type=F32)
    o_ref[...] = acc


def mix_out(a, u, x, w_pool, pool_scale, w_out_a, w_out_p, tm, seq):
    m, d = x.shape
    halo = 16
    assert seq % tm == 0 and tm % halo == 0
    kern = functools.partial(_mix_out_kernel, tiles_per_seq=seq // tm)
    return pl.pallas_call(
        kern,
        grid=(m // tm,),
        in_specs=[
            pl.BlockSpec((tm, A_WIDTH), lambda i: (i, 0)),
            pl.BlockSpec((tm, POOL_WIDTH), lambda i: (i, 0)),
            pl.BlockSpec((halo, POOL_WIDTH), lambda i: (jnp.maximum(i * (tm // halo) - 1, 0), 0)),
            pl.BlockSpec((tm, d), lambda i: (i, 0)),
            pl.BlockSpec((len(POOL_WINDOWS), POOL_GROUP_DIM, POOL_GROUP_DIM), lambda i: (0, 0, 0)),
            pl.BlockSpec((1, POOL_WIDTH), lambda i: (0, 0)),
            pl.BlockSpec((A_WIDTH, d), lambda i: (0, 0)),
            pl.BlockSpec((POOL_WIDTH, d), lambda i: (0, 0)),
        ],
        out_specs=pl.BlockSpec((tm, d), lambda i: (i, 0)),
        out_shape=jax.ShapeDtypeStruct((m, d), F32),
        scratch_shapes=[pltpu.VMEM((tm + halo, POOL_WIDTH), F32)],
        compiler_params=_params("parallel"),
        name="mix_out",
    )(a, u, u, x, w_pool, pool_scale, w_out_a, w_out_p)


def _mix_out_sample_kernel(a_ref, mix_ref, x_ref, wpool_ref, scale_ref, woa_ref, wop_ref, o_ref):
    acc = x_ref[...] + jnp.dot(a_ref[...].astype(BF16), woa_ref[...], preferred_element_type=F32)
    for grp in range(len(POOL_WINDOWS)):
        cols = slice(grp * POOL_GROUP_DIM, (grp + 1) * POOL_GROUP_DIM)
        y = jnp.dot(mix_ref[:, cols].astype(BF16), wpool_ref[grp], preferred_element_type=F32) * scale_ref[:, cols]
        acc = acc + jnp.dot(y.astype(BF16), wop_ref[cols, :], preferred_element_type=F32)
    o_ref[...] = acc


def mix_out_sample(a, mix, x, w_pool, pool_scale, w_out_a, w_out_p):
    m, d = x.shape
    return pl.pallas_call(
        _mix_out_sample_kernel,
        out_shape=jax.ShapeDtypeStruct((m, d), F32),
        compiler_params=pltpu.CompilerParams(vmem_limit_bytes=VMEM_LIMIT_BYTES),
        name="mix_out_sample",
    )(a, mix, x, w_pool, pool_scale, w_out_a, w_out_p)


def _top2(logits):
    lane = lax.broadcasted_iota(jnp.int32, logits.shape, 1)
    lg = jnp.where(lane < N_EXPERTS, logits, -jnp.inf)
    m1 = jnp.max(lg, axis=-1, keepdims=True)
    i1 = jnp.min(jnp.where(lg == m1, lane, LANES), axis=-1, keepdims=True)
    rest = jnp.where(lane == i1, -jnp.inf, lg)
    m2 = jnp.max(rest, axis=-1, keepdims=True)
    i2 = jnp.min(jnp.where(rest == m2, lane, LANES), axis=-1, keepdims=True)
    e2 = jnp.exp(m2 - m1)
    return lane, i1, i2, 1.0 / (1.0 + e2), e2 / (1.0 + e2)


def _top2_gates(logits):
    lane, i1, i2, g1, g2 = _top2(logits)
    return jnp.where(lane == i1, g1, 0.0) + jnp.where(lane == i2, g2, 0.0)


ROUTE_E1, ROUTE_E2, ROUTE_R1, ROUTE_R2, ROUTE_G1, ROUTE_G2 = range(6)
EXPERT_TILE = 512


def _router_kernel(x_ref, g_ref, wr_ref, tri_ref, o_ref):
    hn = _rmsnorm(x_ref[...], g_ref[...]).astype(BF16)
    logits = jnp.dot(hn, wr_ref[...], preferred_element_type=F32)
    lane, i1, i2, g1, g2 = _top2(logits)
    chosen = jnp.where(jnp.logical_or(lane == i1, lane == i2), 1.0, 0.0).astype(BF16)
    before = jnp.dot(tri_ref[...], chosen, preferred_element_type=F32)
    r1 = jnp.sum(jnp.where(lane == i1, before, 0.0), axis=-1, keepdims=True)
    r2 = jnp.sum(jnp.where(lane == i2, before, 0.0), axis=-1, keepdims=True)
    rec = jnp.zeros(logits.shape, F32)
    for pos, val in ((ROUTE_E1, i1.astype(F32)), (ROUTE_E2, i2.astype(F32)), (ROUTE_R1, r1), (ROUTE_R2, r2),
                     (ROUTE_G1, g1), (ROUTE_G2, g2)):
        rec = jnp.where(lane == pos, val, rec)
    o_ref[...] = rec


def route_tokens(x, g, w_router, tm):
    m, d = x.shape
    tri = (jnp.arange(tm)[:, None] > jnp.arange(tm)[None, :]).astype(BF16)
    return pl.pallas_call(
        _router_kernel,
        grid=(m // tm,),
        in_specs=[
            pl.BlockSpec((tm, d), lambda i: (i, 0)),
            pl.BlockSpec((1, d), lambda i: (0, 0)),
            pl.BlockSpec((d, LANES), lambda i: (0, 0)),
            pl.BlockSpec((tm, tm), lambda i: (0, 0)),
        ],
        out_specs=pl.BlockSpec((tm, LANES), lambda i: (i, 0)),
        out_shape=jax.ShapeDtypeStruct((m, LANES), F32),
        compiler_params=_params("parallel"),
        name="route_tokens",
    )(x, g, w_router, tri)


SUBLANES = 8


def _slot_copy(src, src_slot, dst, dst_slot, sem):
    rows = lambda s: pl.ds(pl.multiple_of(s * SUBLANES, SUBLANES), SUBLANES)
    return pltpu.make_async_copy(src.at[rows(src_slot)], dst.at[rows(dst_slot)], sem)


def _slab(c, n):
    return pl.ds(c, n, stride=SUBLANES), slice(None)


def _scatter_rows_kernel(dest_ref, x_ref, o_hbm, buf_ref, sem):
    tm = x_ref.shape[0]
    for c in range(x_ref.shape[1] // LANES):
        buf_ref[_slab(c, tm)] = x_ref[:, c * LANES:(c + 1) * LANES]

    def issue(r, carry):
        for k in range(2):
            _slot_copy(buf_ref, r, o_hbm, dest_ref[0, 0, 2 * r + k], sem).start(priority=k)
        return carry

    def drain(r, carry):
        for k in range(2):
            _slot_copy(buf_ref, 0, o_hbm, 0, sem).wait()
        return carry

    lax.fori_loop(0, tm, issue, 0, unroll=8)
    lax.fori_loop(0, tm, drain, 0, unroll=8)


def scatter_rows(x, dest, tm):
    m, d = x.shape
    assert d == SUBLANES * LANES
    dest3 = dest.reshape(m // tm, 1, 2 * tm)
    return pl.pallas_call(
        _scatter_rows_kernel,
        grid=(m // tm,),
        in_specs=[
            pl.BlockSpec((1, 1, 2 * tm), lambda i: (i, 0, 0), memory_space=pltpu.SMEM),
            pl.BlockSpec((tm, d), lambda i: (i, 0)),
        ],
        out_specs=pl.BlockSpec(memory_space=pl.ANY),
        out_shape=jax.ShapeDtypeStruct((2 * m * SUBLANES, LANES), x.dtype),
        scratch_shapes=[pltpu.VMEM((tm * SUBLANES, LANES), x.dtype), pltpu.SemaphoreType.DMA(())],
        compiler_params=_params("arbitrary"),
        name="scatter_rows",
    )(dest3, x)


def _expert_kernel(tile_ref, exp_ref, lo_ref, hi_ref, nvis_ref, x_ref, g_ref, wg_ref, wu_ref, wd_ref, o_ref,
                   hn_ref, acc_ref):
    v = pl.program_id(0)
    j = pl.program_id(1)

    @pl.when(v < nvis_ref[0])
    def _():
        tm, d = hn_ref.shape
        n_slab = d // LANES

        @pl.when(j == 0)
        def _():
            slabs = [x_ref[_slab(c, tm)] for c in range(n_slab)]
            ms = sum(jnp.sum(s * s, axis=-1, keepdims=True) for s in slabs) * (1.0 / d)
            inv = lax.rsqrt(ms + RMS_EPS)
            for c, s in enumerate(slabs):
                cols = slice(c * LANES, (c + 1) * LANES)
                hn_ref[:, cols] = (s * inv * g_ref[:, cols]).astype(BF16)
            acc_ref[...] = jnp.zeros_like(acc_ref)

        hn = hn_ref[...]
        a = jnp.dot(hn, wg_ref[...], preferred_element_type=F32)
        b = jnp.dot(hn, wu_ref[...], preferred_element_type=F32)
        h = (a * _sigmoid(a) * b).astype(BF16)
        acc_ref[...] += jnp.dot(h, wd_ref[...], preferred_element_type=F32)

        @pl.when(j == pl.num_programs(1) - 1)
        def _():
            row = lax.broadcasted_iota(jnp.int32, (tm, 1), 0)
            mine = jnp.logical_and(row >= lo_ref[v], row < hi_ref[v])
            first = jnp.logical_or(v == 0, tile_ref[v] != tile_ref[jnp.maximum(v - 1, 0)])

            @pl.when(first)
            def _():
                for c in range(n_slab):
                    o_ref[_slab(c, tm)] = jnp.where(mine, acc_ref[:, c * LANES:(c + 1) * LANES], 0.0)

            @pl.when(jnp.logical_not(first))
            def _():
                for c in range(n_slab):
                    o_ref[_slab(c, tm)] = jnp.where(mine, acc_ref[:, c * LANES:(c + 1) * LANES], o_ref[_slab(c, tm)])


def expert_swiglu(xs, g, w_gate, w_up, w_down, visits, tm, tf):
    d = w_gate.shape[1]
    m = xs.shape[0] // SUBLANES
    f = w_gate.shape[2]
    n_f = f // tf
    assert m % tm == 0 and f % tf == 0 and xs.shape[1] == LANES and d == SUBLANES * LANES
    tile, expert, lo, hi, n_visit = visits
    slot_tile = pl.BlockSpec((tm * SUBLANES, LANES), lambda v, j, t, e, lo, hi, n: (t[v], 0))

    def f_blk(v, j):
        return j + (v % 2) * (n_f - 1 - 2 * j)

    grid_spec = pltpu.PrefetchScalarGridSpec(
        num_scalar_prefetch=5,
        grid=(tile.shape[0], n_f),
        in_specs=[
            slot_tile,
            pl.BlockSpec((1, d), lambda v, j, t, e, lo, hi, n: (0, 0)),
            pl.BlockSpec((None, d, tf), lambda v, j, t, e, lo, hi, n: (e[v], 0, f_blk(v, j))),
            pl.BlockSpec((None, d, tf), lambda v, j, t, e, lo, hi, n: (e[v], 0, f_blk(v, j))),
            pl.BlockSpec((None, tf, d), lambda v, j, t, e, lo, hi, n: (e[v], f_blk(v, j), 0)),
        ],
        out_specs=slot_tile,
        scratch_shapes=[pltpu.VMEM((tm, d), BF16), pltpu.VMEM((tm, d), F32)],
    )
    return pl.pallas_call(
        _expert_kernel,
        grid_spec=grid_spec,
        out_shape=jax.ShapeDtypeStruct(xs.shape, F32),
        compiler_params=_params("arbitrary", "arbitrary"),
        name="expert_swiglu",
    )(tile, expert, lo, hi, n_visit, xs, g, w_gate, w_up, w_down)


def _combine_kernel(dest_ref, x_ref, rec_ref, gf_ref, ys_hbm, o_ref, y1_ref, y2_ref, sem):
    tm = x_ref.shape[0]
    bufs = (y1_ref, y2_ref)

    def issue(r, carry):
        for k in range(2):
            _slot_copy(ys_hbm, dest_ref[0, 0, 2 * r + k], bufs[k], r, sem).start(priority=k)
        return carry

    def drain(r, carry):
        for k in range(2):
            _slot_copy(ys_hbm, 0, bufs[k], 0, sem).wait()
        return carry

    lax.fori_loop(0, tm, issue, 0, unroll=8)
    lax.fori_loop(0, tm, drain, 0, unroll=8)
    g1 = rec_ref[:, ROUTE_G1:ROUTE_G1 + 1]
    g2 = rec_ref[:, ROUTE_G2:ROUTE_G2 + 1]
    d = x_ref.shape[1]
    outs = []
    for c in range(d // LANES):
        cols = slice(c * LANES, (c + 1) * LANES)
        outs.append(x_ref[:, cols] + (g1 * y1_ref[_slab(c, tm)] + g2 * y2_ref[_slab(c, tm)]))
    ms = sum(jnp.sum(o * o, axis=-1, keepdims=True) for o in outs) * (1.0 / d)
    inv = lax.rsqrt(ms + RMS_EPS)
    for c, o in enumerate(outs):
        cols = slice(c * LANES, (c + 1) * LANES)
        o_ref[:, cols] = o * inv * gf_ref[:, cols]


def combine_rows(x, rec, dest, ys, g_final, tm):
    m, d = x.shape
    dest3 = dest.reshape(m // tm, 1, 2 * tm)
    return pl.pallas_call(
        _combine_kernel,
        grid=(m // tm,),
        in_specs=[
            pl.BlockSpec((1, 1, 2 * tm), lambda i: (i, 0, 0), memory_space=pltpu.SMEM),
            pl.BlockSpec((tm, d), lambda i: (i, 0)),
            pl.BlockSpec((tm, LANES), lambda i: (i, 0)),
            pl.BlockSpec((1, d), lambda i: (0, 0)),
            pl.BlockSpec(memory_space=pl.ANY),
        ],
        out_specs=pl.BlockSpec((tm, d), lambda i: (i, 0)),
        out_shape=jax.ShapeDtypeStruct((m, d), F32),
        scratch_shapes=[pltpu.VMEM((tm * SUBLANES, LANES), F32)] * 2 + [pltpu.SemaphoreType.DMA(())],
        compiler_params=_params("arbitrary"),
        name="combine_rows",
    )(dest3, x, rec, g_final, ys)


def _visit_list(rec, tile_tokens, tm):
    n = rec.shape[0]
    e = rec[:, ROUTE_E1:ROUTE_E2 + 1].astype(jnp.int32)
    rank = rec[:, ROUTE_R1:ROUTE_R2 + 1].astype(jnp.int32)
    onehot = (e[:, :, None] == jnp.arange(N_EXPERTS)[None, None, :]).astype(jnp.int32)
    per_tile = onehot.reshape(n // tile_tokens, tile_tokens * 2, N_EXPERTS).sum(axis=1)
    count = per_tile.sum(axis=0)
    start = jnp.cumsum(count) - count
    tile_base = jnp.cumsum(per_tile, axis=0) - per_tile
    base = (start[None, :] + tile_base)[:, None, :]
    dest = rank + jnp.sum(onehot.reshape(n // tile_tokens, tile_tokens * 2, N_EXPERTS) * base, axis=-1).reshape(n, 2)

    n_tile = 2 * n // tm
    n_visit_max = n_tile + N_EXPERTS
    end = start + count
    first_tile = start // tm
    last_tile = jnp.maximum(end - 1, start) // tm
    n_vis = jnp.where(count > 0, last_tile - first_tile + 1, 0)
    vis_end = jnp.cumsum(n_vis)
    vis_start = vis_end - n_vis
    total = vis_end[-1]
    v = jnp.arange(n_visit_max)
    vc = jnp.minimum(v, total - 1)
    ex = jnp.sum(vc[:, None] >= vis_end[None, :], axis=1)
    tile = first_tile[ex] + vc - vis_start[ex]
    lo = jnp.clip(start[ex] - tile * tm, 0, tm)
    hi = jnp.clip(end[ex] - tile * tm, 0, tm)
    live = v < total
    as_i32 = lambda a: a.astype(jnp.int32)
    visits = (as_i32(tile), as_i32(ex), as_i32(jnp.where(live, lo, 0)), as_i32(jnp.where(live, hi, 0)),
              as_i32(total).reshape(1))
    return dest.astype(jnp.int32), visits


def routed_experts(x, g, w_router, w_gate, w_up, w_down, g_final):
    rec = route_tokens(x, g, w_router, 1024)
    dest, visits = _visit_list(rec, 1024, EXPERT_TILE)
    xs = scatter_rows(x, dest, 512)
    ys = expert_swiglu(xs, g, w_gate, w_up, w_down, visits, EXPERT_TILE, w_gate.shape[2] // 2)
    return combine_rows(x, rec, dest, ys, g_final, 256)


def _swiglu_kernel(*refs, n_expert, final_norm):
    x_ref, g_ref = refs[0], refs[1]
    k = 2
    if n_expert > 1:
        wr_ref = refs[k]
        k += 1
    wg_ref, wu_ref, wd_ref = refs[k:k + 3]
    k += 3
    if final_norm:
        gf_ref = refs[k]
        k += 1
    o_ref, hn_ref, acc_ref = refs[k:k + 3]
    if n_expert > 1:
        gate_ref = refs[k + 3]
    e = pl.program_id(1)
    j = pl.program_id(2)

    @pl.when(jnp.logical_and(e == 0, j == 0))
    def _():
        hn = _rmsnorm(x_ref[...], g_ref[...])
        hn_ref[...] = hn.astype(BF16)
        acc_ref[...] = jnp.zeros_like(acc_ref)
        if n_expert > 1:
            logits = jnp.dot(hn.astype(BF16), wr_ref[...], preferred_element_type=F32)
            gate_ref[...] = _top2_gates(logits)

    hn = hn_ref[...]
    a = jnp.dot(hn, wg_ref[...], preferred_element_type=F32)
    b = jnp.dot(hn, wu_ref[...], preferred_element_type=F32)
    h = (a * _sigmoid(a) * b).astype(BF16)
    y = jnp.dot(h, wd_ref[...], preferred_element_type=F32)
    if n_expert > 1:
        lane = lax.broadcasted_iota(jnp.int32, gate_ref.shape, 1)
        gate = jnp.sum(jnp.where(lane == e, gate_ref[...], 0.0), axis=-1, keepdims=True)
        y = gate * y
    acc_ref[...] += y

    @pl.when(jnp.logical_and(e == n_expert - 1, j == pl.num_programs(2) - 1))
    def _():
        out = x_ref[...] + acc_ref[...]
        if final_norm:
            out = _rmsnorm(out, gf_ref[...])
        o_ref[...] = out


def swiglu_block(x, g, w_gate, w_up, w_down, tm, tf, w_router=None, g_final=None):
    m, d = x.shape
    n_expert, _, f = w_gate.shape
    assert m % tm == 0 and f % tf == 0
    args = [x, g]
    in_specs = [pl.BlockSpec((tm, d), lambda i, e, j: (i, 0)), pl.BlockSpec((1, d), lambda i, e, j: (0, 0))]
    if n_expert > 1:
        args.append(w_router)
        in_specs.append(pl.BlockSpec((d, LANES), lambda i, e, j: (0, 0)))
    args += [w_gate, w_up, w_down]
    in_specs += [
        pl.BlockSpec((None, d, tf), lambda i, e, j: (e, 0, j)),
        pl.BlockSpec((None, d, tf), lambda i, e, j: (e, 0, j)),
        pl.BlockSpec((None, tf, d), lambda i, e, j: (e, j, 0)),
    ]
    if g_final is not None:
        args.append(g_final)
        in_specs.append(pl.BlockSpec((1, d), lambda i, e, j: (0, 0)))
    scratch = [pltpu.VMEM((tm, d), BF16), pltpu.VMEM((tm, d), F32)]
    if n_expert > 1:
        scratch.append(pltpu.VMEM((tm, LANES), F32))
    kern = functools.partial(_swiglu_kernel, n_expert=n_expert, final_norm=g_final is not None)
    return pl.pallas_call(
        kern,
        grid=(m // tm, n_expert, f // tf),
        in_specs=in_specs,
        out_specs=pl.BlockSpec((tm, d), lambda i, e, j: (i, 0)),
        out_shape=jax.ShapeDtypeStruct((m, d), F32),
        scratch_shapes=scratch,
        compiler_params=_params("parallel", "arbitrary", "arbitrary"),
        name="swiglu_block",
    )(*args)


def _ln_silu_pw2(z, lng_ref, lnb_ref, w2_ref, b2_ref):
    mu = jnp.mean(z, axis=-1, keepdims=True)
    zc = z - mu
    var = jnp.mean(zc * zc, axis=-1, keepdims=True)
    zn = zc * lax.rsqrt(var + LN_EPS) * lng_ref[...] + lnb_ref[...]
    act = (zn * _sigmoid(zn)).astype(BF16)
    return jnp.dot(act, w2_ref[...], preferred_element_type=F32) + b2_ref[...]


def _conv_kernel(glu_ref, halo_ref, x_ref, wdw_ref, bdw_ref, lng_ref, lnb_ref, w2_ref, b2_ref, o_ref,
                 ext_ref, z_ref, *, tiles_per_seq, row_chunk):
    i = pl.program_id(0)
    tm = glu_ref.shape[0]
    halo = halo_ref.shape[0]
    n_slab = glu_ref.shape[1] // LANES
    at_start = i % tiles_per_seq == 0
    for cb in range(n_slab):
        cols = slice(cb * LANES, (cb + 1) * LANES)
        ext_ref[cb, 0:halo, :] = jnp.where(at_start, 0.0, _round_bf16(halo_ref[:, cols]))
        ext_ref[cb, halo:halo + tm, :] = _round_bf16(glu_ref[:, cols])
    base = halo - (CONV_K - 1)

    def slab(cb, carry):
        for rc in range(tm // row_chunk):
            r0 = rc * row_chunk
            acc = jnp.zeros((row_chunk, LANES), F32) + bdw_ref[cb]
            for tap in range(CONV_K):
                acc = acc + ext_ref[cb, r0 + base + tap:r0 + base + tap + row_chunk, :] * wdw_ref[cb, tap:tap + 1, :]
            z_ref[cb, r0:r0 + row_chunk, :] = acc
        return carry

    lax.fori_loop(0, n_slab, slab, 0)
    z = jnp.concatenate([z_ref[cb] for cb in range(n_slab)], axis=-1)
    o_ref[...] = x_ref[...] + _ln_silu_pw2(z, lng_ref, lnb_ref, w2_ref, b2_ref)


def conv_block(glu, x, w_dw, b_dw, ln_g, ln_b, w_pw2, b_pw2, tm, seq):
    m, d = x.shape
    halo = 32
    n_slab = d // LANES
    assert seq % tm == 0 and tm % halo == 0
    kern = functools.partial(_conv_kernel, tiles_per_seq=seq // tm, row_chunk=64)
    vec = pl.BlockSpec((1, d), lambda i: (0, 0))
    w_dw_slab = jnp.transpose(w_dw.reshape(w_dw.shape[0], n_slab, LANES), (1, 0, 2))
    b_dw_slab = b_dw.reshape(n_slab, 1, LANES)
    return pl.pallas_call(
        kern,
        grid=(m // tm,),
        in_specs=[
            pl.BlockSpec((tm, d), lambda i: (i, 0)),
            pl.BlockSpec((halo, d), lambda i: (jnp.maximum(i * (tm // halo) - 1, 0), 0)),
            pl.BlockSpec((tm, d), lambda i: (i, 0)),
            pl.BlockSpec(w_dw_slab.shape, lambda i: (0, 0, 0)),
            pl.BlockSpec(b_dw_slab.shape, lambda i: (0, 0, 0)),
            vec, vec,
            pl.BlockSpec((d, d), lambda i: (0, 0)),
            vec,
        ],
        out_specs=pl.BlockSpec((tm, d), lambda i: (i, 0)),
        out_shape=jax.ShapeDtypeStruct((m, d), F32),
        scratch_shapes=[pltpu.VMEM((n_slab, tm + halo, LANES), F32), pltpu.VMEM((n_slab, tm, LANES), F32)],
        compiler_params=_params("parallel"),
        name="conv_block",
    )(glu, glu, x, w_dw_slab, b_dw_slab, ln_g, ln_b, w_pw2, b_pw2)


def _conv_sample_kernel(ext_ref, x_ref, wdw_ref, bdw_ref, lng_ref, lnb_ref, w2_ref, b2_ref, o_ref, z_ref,
                        *, steps):
    n = ext_ref.shape[0]
    z_ref[...] = jnp.zeros_like(z_ref)

    def one(s, carry):
        acc = jnp.zeros((steps, z_ref.shape[2]), F32) + bdw_ref[...]
        for tap in range(CONV_K):
            acc = acc + _round_bf16(ext_ref[s, tap:tap + steps, :]) * wdw_ref[tap:tap + 1, :]
        z_ref[s, 0:steps, :] = acc
        return carry

    lax.fori_loop(0, n, one, 0)
    z = z_ref[...].reshape(n * z_ref.shape[1], z_ref.shape[2])
    y = _ln_silu_pw2(z, lng_ref, lnb_ref, w2_ref, b2_ref)
    o_ref[...] = x_ref[...] + y.reshape(o_ref.shape)


def conv_block_sample(ext, x, w_dw, b_dw, ln_g, ln_b, w_pw2, b_pw2, steps):
    n, _, d = ext.shape
    kern = functools.partial(_conv_sample_kernel, steps=steps)
    return pl.pallas_call(
        kern,
        out_shape=jax.ShapeDtypeStruct((n, 8, d), F32),
        scratch_shapes=[pltpu.VMEM((n, 8, d), F32)],
        compiler_params=pltpu.CompilerParams(vmem_limit_bytes=VMEM_LIMIT_BYTES),
        name="conv_block_sample",
    )(ext, x, w_dw, b_dw, ln_g, ln_b, w_pw2, b_pw2)


def _sample_attn_kernel(q_ref, kn_ref, vn_ref, kvt_ref, c0_ref, c1_ref, c2_ref, b0_ref, b1_ref, b2_ref, bn_ref,
                        ext_ref, a_ref, mix_ref, o0_ref, o1_ref, o2_ref, *, steps):
    caches = (c0_ref, c1_ref, c2_ref)
    biases = (b0_ref, b1_ref, b2_ref)
    new_caches = (o0_ref, o1_ref, o2_ref)
    n_head = q_ref.shape[2]
    lane = lax.broadcasted_iota(jnp.int32, (n_head, A_HEAD_DIM, LANES), 2)
    keep = LANES - steps

    outs, lses = [], []
    for branch in range(N_BRANCH):
        c_ref = caches[branch]
        q = (q_ref[0, branch] * (A_HEAD_DIM ** -0.5)).astype(BF16)
        kt = c_ref[0, 0].astype(BF16)
        vt = c_ref[0, 1].astype(BF16)
        kn = kn_ref[0, branch].astype(BF16)
        vn = vn_ref[0, branch].astype(BF16)
        s_c = jnp.einsum("hqd,hdl->hql", q, kt, preferred_element_type=F32) + biases[branch][...]
        s_n = jnp.einsum("hqd,hkd->hqk", q, kn, preferred_element_type=F32) + bn_ref[branch]
        mx = jnp.maximum(jnp.max(s_c, axis=-1, keepdims=True), jnp.max(s_n, axis=-1, keepdims=True))
        den = (jnp.sum(jnp.exp(s_c - mx), axis=-1, keepdims=True)
               + jnp.sum(jnp.exp(s_n - mx), axis=-1, keepdims=True))
        lse = mx + jnp.log(den)
        p_c = jnp.exp(s_c - lse).astype(BF16)
        p_n = jnp.exp(s_n - lse).astype(BF16)
        outs.append(jnp.einsum("hql,hdl->hqd", p_c, vt, preferred_element_type=F32)
                    + jnp.einsum("hqk,hkd->hqd", p_n, vn, preferred_element_type=F32))
        lses.append(lse)
    mx = jnp.maximum(jnp.maximum(lses[0], lses[1]), lses[2])
    wts = [jnp.exp(lse - mx) for lse in lses]
    inv = 1.0 / (wts[0] + wts[1] + wts[2])
    a_ref[0] = outs[0] * (wts[0] * inv) + outs[1] * (wts[1] * inv) + outs[2] * (wts[2] * inv)

    for branch in range(N_BRANCH):
        c_ref, o_ref = caches[branch], new_caches[branch]
        n_tile = c_ref.shape[-1] // LANES
        for kv in range(2):
            nxt = pltpu.roll(c_ref[0, kv, :, :, 0:LANES], keep, 2)
            for j in range(n_tile):
                cur = nxt
                if j + 1 < n_tile:
                    nxt = pltpu.roll(c_ref[0, kv, :, :, (j + 1) * LANES:(j + 2) * LANES], keep, 2)
                else:
                    nxt = kvt_ref[0, branch, kv]
                o_ref[0, kv, :, :, j * LANES:(j + 1) * LANES] = jnp.where(lane < keep, cur, nxt)

    base = ext_ref.shape[1] - steps
    for t in range(steps):
        for grp, win in enumerate(POOL_WINDOWS):
            cols = slice(grp * POOL_GROUP_DIM, (grp + 1) * POOL_GROUP_DIM)
            last = base + t
            tot = jnp.sum(ext_ref[0, last - win + 1:last + 1, cols], axis=0, keepdims=True)
            mix_ref[0, t:t + 1, cols] = tot / float(win) - ext_ref[0, last:last + 1, cols]


def sample_attention(q, kn, vn, kvt, caches, biases, bias_new, ext_pool, steps):
    n, rows = q.shape[0], q.shape[3]
    hg = A_HEADS // 2
    per_head = (hg, rows, A_HEAD_DIM)
    in_specs = [pl.BlockSpec((1, N_BRANCH) + per_head, lambda i, j: (i, 0, j, 0, 0))] * 3
    in_specs.append(pl.BlockSpec((1, N_BRANCH, 2, hg, A_HEAD_DIM, LANES), lambda i, j: (i, 0, 0, j, 0, 0)))
    cache_specs = [pl.BlockSpec((1, 2, hg, A_HEAD_DIM, c.shape[-1]), lambda i, j: (i, 0, j, 0, 0)) for c in caches]
    in_specs += cache_specs
    in_specs += [pl.BlockSpec((hg, rows, b.shape[-1]), lambda i, j: (j, 0, 0)) for b in biases]
    in_specs += [
        pl.BlockSpec((N_BRANCH, hg, rows, rows), lambda i, j: (0, j, 0, 0)),
        pl.BlockSpec((1,) + ext_pool.shape[1:], lambda i, j: (i, 0, 0)),
    ]
    kern = functools.partial(_sample_attn_kernel, steps=steps)
    return pl.pallas_call(
        kern,
        grid=(n, A_HEADS // hg),
        in_specs=in_specs,
        out_specs=[
            pl.BlockSpec((1,) + per_head, lambda i, j: (i, j, 0, 0)),
            pl.BlockSpec((1, steps, POOL_WIDTH), lambda i, j: (i, 0, 0)),
        ] + cache_specs,
        out_shape=[
            jax.ShapeDtypeStruct((n, A_HEADS, rows, A_HEAD_DIM), F32),
            jax.ShapeDtypeStruct((n, steps, POOL_WIDTH), F32),
        ] + [jax.ShapeDtypeStruct(c.shape, F32) for c in caches],
        compiler_params=_params("parallel", "arbitrary"),
        name="sample_attention",
    )(q, kn, vn, kvt, *caches, *biases, bias_new, ext_pool)


def _t5_bucket(dist):
    max_exact = NUM_BUCKETS // 2
    d = jnp.maximum(dist, 1).astype(F32)
    large = max_exact + (jnp.log(d / max_exact) / math.log(REL_MAX_DIST / max_exact)
                         * (NUM_BUCKETS - max_exact)).astype(jnp.int32)
    return jnp.where(dist < max_exact, dist, jnp.minimum(large, NUM_BUCKETS - 1))


def _slot_bias(rel_bias):
    rb = rel_bias.astype(F32).reshape(NUM_BUCKETS, N_BRANCH, A_HEADS)
    slot = jnp.arange(BAND + 1)
    out = []
    for g, (_, dil) in enumerate(A_GROUPS):
        pick = jax.nn.one_hot(_t5_bucket(slot * dil), NUM_BUCKETS, dtype=F32)
        out.append(jnp.einsum("sb,bh->hs", pick, rb[:, g], precision=lax.Precision.HIGHEST))
    return jnp.stack(out)


def _sliding_rows(v, n_rows, width):
    period = v.shape[-1]
    assert period >= n_rows + width - 1
    flat = jnp.tile(v, (1,) * (v.ndim - 1) + (n_rows + 1,))[..., :(period + 1) * n_rows]
    return flat.reshape(v.shape[:-1] + (n_rows, period + 1))[..., :width]


def _prompt_bias_tables(rel_bias):
    sb = _slot_bias(rel_bias)
    lead = sb.shape[:2]
    diag = jnp.concatenate([jnp.full(lead + (BAND - 1,), NEG, F32), sb[..., ::-1],
                            jnp.full(lead + (BAND,), NEG, F32)], axis=-1)
    return _sliding_rows(diag, BAND, 2 * BAND)[:, :, ::-1]


def _sample_bias_tables(rel_bias, steps, rows):
    sb = _slot_bias(rel_bias)
    cache_tables = []
    for g, (window, dil) in enumerate(A_GROUPS):
        spread = jnp.concatenate([sb[g][..., None], jnp.full(sb[g].shape + (dil - 1,), NEG, F32)], axis=-1)
        by_dist = spread.reshape(A_HEADS, (BAND + 1) * dil)
        need = window + rows
        by_dist = jnp.pad(by_dist, ((0, 0), (0, max(need - by_dist.shape[1], 0))), constant_values=NEG)[:, :need]
        by_dist = jnp.where(jnp.arange(need)[None, :] <= window, by_dist, NEG)
        cache_tables.append(_sliding_rows(by_dist[:, ::-1], rows, window)[:, ::-1])
    t_q = jnp.arange(rows)[:, None]
    t_k = jnp.arange(rows)[None, :]
    new_tables = []
    for g, (_, dil) in enumerate(A_GROUPS):
        diff = t_q - t_k
        ok = (diff >= 0) & (diff % dil == 0) & (t_k < steps)
        pick = jax.nn.one_hot(jnp.where(ok, diff // dil, BAND + 1).reshape(-1), BAND + 1, dtype=F32)
        vals = jnp.einsum("ps,hs->hp", pick, sb[g], precision=lax.Precision.HIGHEST).reshape(A_HEADS, rows, rows)
        new_tables.append(jnp.where(ok[None], vals, NEG))
    return cache_tables, jnp.stack(new_tables)


def kernel(x_prompt, x_sample, cache_kv_w128, cache_kv_w512, cache_kv_w2048, state_pool, state_conv, norm_mix0, w_in0, rel_bias, w_pool, pool_scale, w_out0, norm_ffn0, w_ff_gate, w_ff_up, w_ff_down, norm_mix1, w_pw1, b_pw1, w_dw, b_dw, ln_g, ln_b, w_pw2, b_pw2, norm_ffn1, w_router, we_gate, we_up, we_down, norm_final):
    batch, seq, d = x_prompt.shape
    n_s, steps, _ = x_sample.shape
    caches_in = (cache_kv_w128, cache_kv_w512, cache_kv_w2048)
    assert norm_mix0.shape[0] == 1 and norm_mix1.shape[0] == 1, "two layers: one mixer of each kind"
    assert seq % CHUNK == 0 and steps <= min(dil for _, dil in A_GROUPS[1:])
    for c, (window, _) in zip(caches_in, A_GROUPS):
        assert c.shape[2] == window, "cache must hold a full window"
    qkv_w = N_BRANCH * 3 * A_WIDTH

    w_in = w_in0[0]
    w_in_bf = w_in.astype(BF16)
    w_kv = w_in_bf[:, A_WIDTH * N_BRANCH:qkv_w]
    g_mix0, g_ffn0 = norm_mix0, norm_ffn0
    g_mix1, g_ffn1 = norm_mix1, norm_ffn1
    g_final = norm_final.reshape(1, d)
    wpool_bf = w_pool[0].astype(BF16)
    w_out_a = w_out0[0, :A_WIDTH].astype(BF16)
    w_out_p = w_out0[0, A_WIDTH:].astype(BF16)
    wff = (w_ff_gate.astype(BF16), w_ff_up.astype(BF16), w_ff_down.astype(BF16))
    w_pw1_bf = w_pw1[0].astype(BF16)
    w_pw2_bf = w_pw2[0].astype(BF16)
    w_dw_pad = jnp.pad(w_dw[0], ((0, 32 - CONV_K), (0, 0)))
    wex = (we_gate[0].astype(BF16), we_up[0].astype(BF16), we_down[0].astype(BF16))
    w_router_pad = jnp.pad(w_router[0], ((0, 0), (0, LANES - N_EXPERTS))).astype(BF16)
    zeros = lambda n: jnp.zeros((1, n), F32)

    xp = x_prompt.reshape(batch * seq, d)
    qkv, u_p = qkv_perm(xp, g_mix0, w_in_bf)
    a_p = band_attention(qkv, _prompt_bias_tables(rel_bias), batch, seq)
    h_p = mix_out(a_p, u_p, xp, wpool_bf, pool_scale, w_out_a, w_out_p, 512, seq)
    h_p = swiglu_block(h_p, g_ffn0, *wff, tm=512, tf=1408)
    glu_p = rms_glu(h_p, g_mix1, w_pw1_bf, b_pw1, 1024, 512)
    h_p = conv_block(glu_p, h_p, w_dw_pad, b_dw, ln_g, ln_b, w_pw2_bf, b_pw2, 256, seq)
    y_p = routed_experts(h_p, g_ffn1, w_router_pad, *wex, g_final)

    keep = A_GROUPS[-1][0]
    kv_tail = rms_matmul_t(x_prompt, g_mix0, w_kv.T, seq - keep, keep, 1024, 512)
    kv_tail = kv_tail.reshape(batch, 2, N_BRANCH, A_HEADS, A_HEAD_DIM, keep)
    kv_p = [jnp.transpose(kv_tail[:, :, g, :, :, keep - window:], (0, 4, 1, 2, 3))[None]
            for g, (window, _) in enumerate(A_GROUPS)]
    u_p3 = u_p.reshape(batch, seq, POOL_WIDTH)
    pool_p = u_p3[:, seq - POOL_BUF:][None]
    conv_p = glu_p.reshape(batch, seq, d)[:, seq - (CONV_K - 1):][None]

    m_s = n_s * steps
    xs = x_sample.reshape(m_s, d)
    proj = rms_matmul(xs, g_mix0, w_in_bf, zeros(w_in_bf.shape[1]), m_s, 512, F32)
    qkv_s = proj[:, :qkv_w].reshape(n_s, steps, 3, N_BRANCH, A_HEADS, A_HEAD_DIM)
    u_s = proj[:, qkv_w:].reshape(n_s, steps, POOL_WIDTH)
    rows = 8
    per_head = jnp.transpose(qkv_s, (2, 0, 3, 4, 1, 5))
    per_head = jnp.pad(per_head, ((0, 0),) * 4 + ((0, rows - steps), (0, 0)))
    new_cols = jnp.transpose(qkv_s[:, :, 1:], (0, 3, 2, 4, 5, 1))
    new_cols = jnp.pad(new_cols, ((0, 0),) * 5 + ((LANES - steps, 0),))
    caches_t = [jnp.transpose(c[0], (0, 2, 3, 4, 1)) for c in caches_in]
    ext_pool = jnp.concatenate([state_pool[0], u_s], axis=1)
    cache_bias, new_bias = _sample_bias_tables(rel_bias, steps, rows)
    a_s, mix_s, *caches_out = sample_attention(per_head[0], per_head[1], per_head[2], new_cols, caches_t,
                                               cache_bias, new_bias, ext_pool, steps)
    a_s = jnp.transpose(a_s[:, :, :steps], (0, 2, 1, 3)).reshape(m_s, A_WIDTH)
    h_s = mix_out_sample(a_s, mix_s.reshape(m_s, POOL_WIDTH), xs, wpool_bf, pool_scale, w_out_a, w_out_p)
    h_s = swiglu_block(h_s, g_ffn0, *wff, tm=m_s, tf=1408)
    glu_s = rms_glu(h_s, g_mix1, w_pw1_bf, b_pw1, m_s, 512)
    ext_conv = jnp.concatenate([state_conv[0], glu_s.reshape(n_s, steps, d)], axis=1)
    ext_conv_pad = jnp.pad(ext_conv, ((0, 0), (0, 40 - ext_conv.shape[1]), (0, 0)))
    h_s3 = jnp.pad(h_s.reshape(n_s, steps, d), ((0, 0), (0, 8 - steps), (0, 0)))
    h_s = conv_block_sample(ext_conv_pad, h_s3, w_dw_pad, b_dw, ln_g, ln_b, w_pw2_bf, b_pw2, steps)
    h_s = h_s[:, :steps].reshape(m_s, d)
    y_s = swiglu_block(h_s, g_ffn1, *wex, tm=m_s, tf=896, w_router=w_router_pad, g_final=g_final)

    kv_s = [jnp.transpose(c, (0, 4, 1, 2, 3))[None] for c in caches_out]
    pool_s = ext_pool[:, steps:][None]
    conv_s = ext_conv[:, steps:][None]

    return (y_p.reshape(batch, seq, d), y_s.reshape(n_s, steps, d), kv_p[0], kv_p[1], kv_p[2], pool_p, conv_p,
            kv_s[0], kv_s[1], kv_s[2], pool_s, conv_s)
```

```python
import functools
import math

import jax
import jax.numpy as jnp
from jax import lax
from jax.experimental import pallas as pl
from jax.experimental.pallas import tpu as pltpu

F32 = jnp.float32
BF16 = jnp.bfloat16

A_GROUPS = ((128, 1), (512, 4), (2048, 16))
N_BRANCH = 3
A_HEADS = 8
A_HEAD_DIM = 64
A_WIDTH = 512
BAND = 128
NUM_BUCKETS = 32
REL_MAX_DIST = 2048
POOL_WINDOWS = (2, 4, 8, 16)
POOL_GROUP_DIM = 128
POOL_WIDTH = 512
POOL_BUF = 15
CONV_K = 31
N_EXPERTS = 8
RMS_EPS = 1e-6
LN_EPS = 1e-5
NEG = -1e30

VMEM_LIMIT_BYTES = 56 * 1024 * 1024
LANES = 128
CHUNK = 2048
GROUP = 8


def _params(*sem):
    return pltpu.CompilerParams(dimension_semantics=sem, vmem_limit_bytes=VMEM_LIMIT_BYTES)


def _rmsnorm(x, g):
    return x * lax.rsqrt(jnp.mean(x * x, axis=-1, keepdims=True) + RMS_EPS) * g


def _sigmoid(x):
    return 1.0 / (1.0 + jnp.exp(-x))


def _round_bf16(x):
    return x.astype(BF16).astype(F32)


def _rms_matmul_kernel(x_ref, g_ref, w_ref, b_ref, o_ref, hn_ref):
    @pl.when(pl.program_id(1) == 0)
    def _():
        hn_ref[...] = _rmsnorm(x_ref[...], g_ref[...]).astype(BF16)

    acc = jnp.dot(hn_ref[...], w_ref[...], preferred_element_type=F32)
    o_ref[...] = (acc + b_ref[...]).astype(o_ref.dtype)


def rms_matmul(x, g, w, b, tm, tn, out_dtype):
    m, k = x.shape
    n = w.shape[1]
    assert m % tm == 0 and n % tn == 0
    return pl.pallas_call(
        _rms_matmul_kernel,
        grid=(m // tm, n // tn),
        in_specs=[
            pl.BlockSpec((tm, k), lambda i, j: (i, 0)),
            pl.BlockSpec((1, k), lambda i, j: (0, 0)),
            pl.BlockSpec((k, tn), lambda i, j: (0, j)),
            pl.BlockSpec((1, tn), lambda i, j: (0, j)),
        ],
        out_specs=pl.BlockSpec((tm, tn), lambda i, j: (i, j)),
        out_shape=jax.ShapeDtypeStruct((m, n), out_dtype),
        scratch_shapes=[pltpu.VMEM((tm, k), BF16)],
        compiler_params=_params("parallel", "arbitrary"),
        name="rms_matmul",
    )(x, g, w, b)


def _rms_matmul_t_kernel(x_ref, g_ref, wt_ref, o_ref, hn_ref):
    @pl.when(pl.program_id(2) == 0)
    def _():
        hn_ref[...] = _rmsnorm(x_ref[0], g_ref[...]).astype(BF16)

    nt = (((1,), (1,)), ((), ()))
    o_ref[0] = lax.dot_general(wt_ref[...], hn_ref[...], nt, preferred_element_type=F32)


def rms_matmul_t(x, g, wt, row0, rows, tm, tn):
    batch, _, k = x.shape
    n = wt.shape[0]
    assert row0 % tm == 0 and rows % tm == 0 and n % tn == 0
    return pl.pallas_call(
        _rms_matmul_t_kernel,
        grid=(batch, rows // tm, n // tn),
        in_specs=[
            pl.BlockSpec((1, tm, k), lambda b, i, j: (b, row0 // tm + i, 0)),
            pl.BlockSpec((1, k), lambda b, i, j: (0, 0)),
            pl.BlockSpec((tn, k), lambda b, i, j: (j, 0)),
        ],
        out_specs=pl.BlockSpec((1, tn, tm), lambda b, i, j: (b, j, i)),
        out_shape=jax.ShapeDtypeStruct((batch, n, rows), F32),
        scratch_shapes=[pltpu.VMEM((tm, k), BF16)],
        compiler_params=_params("parallel", "parallel", "arbitrary"),
        name="rms_matmul_t",
    )(x, g, wt)


def _rms_glu_kernel(x_ref, g_ref, wv_ref, wg_ref, bv_ref, bg_ref, o_ref, hn_ref):
    @pl.when(pl.program_id(1) == 0)
    def _():
        hn_ref[...] = _rmsnorm(x_ref[...], g_ref[...]).astype(BF16)

    hn = hn_ref[...]
    val = jnp.dot(hn, wv_ref[...], preferred_element_type=F32) + bv_ref[...]
    gate = jnp.dot(hn, wg_ref[...], preferred_element_type=F32) + bg_ref[...]
    o_ref[...] = val * _sigmoid(gate)


def rms_glu(x, g, w, b, tm, tn):
    m, k = x.shape
    c = w.shape[1] // 2
    nb = c // tn
    return pl.pallas_call(
        _rms_glu_kernel,
        grid=(m // tm, nb),
        in_specs=[
            pl.BlockSpec((tm, k), lambda i, j: (i, 0)),
            pl.BlockSpec((1, k), lambda i, j: (0, 0)),
            pl.BlockSpec((k, tn), lambda i, j: (0, j)),
            pl.BlockSpec((k, tn), lambda i, j: (0, j + nb)),
            pl.BlockSpec((1, tn), lambda i, j: (0, j)),
            pl.BlockSpec((1, tn), lambda i, j: (0, j + nb)),
        ],
        out_specs=pl.BlockSpec((tm, tn), lambda i, j: (i, j)),
        out_shape=jax.ShapeDtypeStruct((m, c), F32),
        scratch_shapes=[pltpu.VMEM((tm, k), BF16)],
        compiler_params=_params("parallel", "arbitrary"),
        name="rms_glu",
    )(x, g, w, w, b, b)


PERM_BLOCK = 256


def _qkv_perm_kernel(x_ref, g_ref, w_ref, perm_ref, o_ref, u_ref, hn_ref):
    j = pl.program_id(1)
    n_qkv = 3 * N_BRANCH

    @pl.when(j == 0)
    def _():
        hn_ref[...] = _rmsnorm(x_ref[...], g_ref[...]).astype(BF16)

    def project():
        return jnp.dot(hn_ref[...], w_ref[...], preferred_element_type=F32)

    @pl.when(j == n_qkv)
    def _():
        u_ref[...] = project()

    @pl.when(j < 3)
    def _():
        o_ref[...] = project().astype(BF16)

    for branch in (1, 2):
        dil = A_GROUPS[branch][1]
        rows = CHUNK // dil
        per = PERM_BLOCK // dil

        @pl.when(jnp.logical_and(j // 3 == branch, j < n_qkv))
        def _():
            res_bf = project().astype(BF16)
            for b in range(CHUNK // PERM_BLOCK):
                blk = jnp.dot(perm_ref[branch - 1], res_bf[b * PERM_BLOCK:(b + 1) * PERM_BLOCK],
                              preferred_element_type=F32).astype(BF16)
                for r in range(dil):
                    o_ref[r * rows + b * per:r * rows + (b + 1) * per, :] = blk[r * per:(r + 1) * per]


def qkv_perm(x, g, w_in):
    m, k = x.shape
    tn = A_WIDTH
    n_qkv = 3 * N_BRANCH
    assert w_in.shape[1] == (n_qkv + 1) * tn
    perms = []
    for _, dil in A_GROUPS[1:]:
        p = jnp.arange(PERM_BLOCK)
        src = (p % (PERM_BLOCK // dil)) * dil + p // (PERM_BLOCK // dil)
        perms.append((src[:, None] == jnp.arange(PERM_BLOCK)[None, :]).astype(BF16))

    def w_block(i, j):
        return 0, jnp.where(j < n_qkv, (j % 3) * N_BRANCH + j // 3, n_qkv)

    return pl.pallas_call(
        _qkv_perm_kernel,
        grid=(m // CHUNK, n_qkv + 1),
        in_specs=[
            pl.BlockSpec((CHUNK, k), lambda i, j: (i, 0)),
            pl.BlockSpec((1, k), lambda i, j: (0, 0)),
            pl.BlockSpec((k, tn), w_block),
            pl.BlockSpec((N_BRANCH - 1, PERM_BLOCK, PERM_BLOCK), lambda i, j: (0, 0, 0)),
        ],
        out_specs=[
            pl.BlockSpec((CHUNK, tn), lambda i, j: (i, jnp.minimum(j, n_qkv - 1))),
            pl.BlockSpec((CHUNK, tn), lambda i, j: (i, 0)),
        ],
        out_shape=[jax.ShapeDtypeStruct((m, n_qkv * tn), BF16), jax.ShapeDtypeStruct((m, tn), F32)],
        scratch_shapes=[pltpu.VMEM((CHUNK, k), BF16)],
        compiler_params=_params("parallel", "arbitrary"),
        name="qkv_perm",
    )(x, g, w_in, jnp.stack(perms))


def _band_attn_kernel(*refs):
    qkv_refs = refs[:15]
    bias_ref, o_ref, acc_ref, lse_ref, kp_ref, vp_ref = refs[15:]
    c = pl.program_id(1)
    n_units = CHUNK // BAND
    grp_rows = GROUP * BAND
    head_of_lane = lax.broadcasted_iota(jnp.int32, (1, 1, LANES), 2) // A_HEAD_DIM
    key_is_prev = lax.broadcasted_iota(jnp.int32, (1, 1, 2 * BAND), 2) < BAND

    for branch, (_, dil) in enumerate(A_GROUPS):
        q_ref, kprev_ref, kcur_ref, vprev_ref, vcur_ref = qkv_refs[5 * branch:5 * branch + 5]
        nblk = n_units // dil
        span = nblk * BAND
        if nblk == 1:
            kp_src, vp_src = kprev_ref, vprev_ref
        else:
            kp_src, vp_src = kp_ref.at[branch], vp_ref.at[branch]
            for res in range(dil):
                lo = res * span
                for dst, prev, cur in ((kp_src, kprev_ref, kcur_ref), (vp_src, vprev_ref, vcur_ref)):
                    dst[lo:lo + BAND, :] = prev[lo + span - BAND:lo + span, :]
                    dst[lo + BAND:lo + span, :] = cur[lo:lo + span - BAND, :]

        def group(gi, carry, seq_start, q_ref=q_ref, kcur_ref=kcur_ref, vcur_ref=vcur_ref, kp_src=kp_src,
                  vp_src=vp_src, nblk=nblk, dil=dil, branch=branch):
            row = pl.multiple_of(gi * grp_rows, grp_rows)

            def blocks(ref):
                return ref[pl.ds(row, grp_rows), :].reshape(GROUP, BAND, LANES)

            q = blocks(q_ref) * jnp.asarray(A_HEAD_DIM ** -0.5, BF16)
            kcat = jnp.concatenate([blocks(kp_src), blocks(kcur_ref)], axis=1)
            vcat = jnp.concatenate([blocks(vp_src), blocks(vcur_ref)], axis=1)
            if seq_start:
                unit = gi * GROUP + lax.broadcasted_iota(jnp.int32, (GROUP, 1, 1), 0)
                pen = jnp.where(jnp.logical_and(unit % nblk == 0, key_is_prev), NEG, 0.0)

            o = jnp.zeros((GROUP, BAND, LANES), F32)
            lse = jnp.zeros((GROUP, BAND, LANES), F32)
            for hh in range(2):
                mine = head_of_lane == hh
                kh = jnp.where(mine, kcat, jnp.zeros_like(kcat))
                vh = jnp.where(mine, vcat, jnp.zeros_like(vcat))
                s = jnp.einsum("uqd,ukd->uqk", q, kh, preferred_element_type=F32) + bias_ref[branch, hh][None]
                if seq_start:
                    s = s + pen
                mx = jnp.max(s, axis=-1, keepdims=True)
                p = jnp.exp(s - mx)
                den = jnp.sum(p, axis=-1, keepdims=True)
                pn = (p * (1.0 / den)).astype(BF16)
                o = o + jnp.einsum("uqk,ukd->uqd", pn, vh, preferred_element_type=F32)
                lse = jnp.where(mine, mx + jnp.log(den), lse)

            if dil == 1:
                acc_ref[branch, pl.ds(row, grp_rows), :] = o.reshape(grp_rows, LANES)
                lse_ref[branch, pl.ds(row, grp_rows), :] = lse.reshape(grp_rows, LANES)
            else:
                per = max(GROUP // nblk, 1)
                blocks_per = GROUP // per
                for k in range(per):
                    if nblk >= GROUP:
                        start = gi // (nblk // GROUP) + dil * BAND * ((gi % (nblk // GROUP)) * GROUP)
                    else:
                        start = gi * per + k
                    rows = pl.ds(start, blocks_per * BAND, stride=dil)
                    part = slice(k * blocks_per, (k + 1) * blocks_per)
                    acc_ref[branch, rows, :] = o[part].reshape(blocks_per * BAND, LANES)
                    lse_ref[branch, rows, :] = lse[part].reshape(blocks_per * BAND, LANES)
            return carry

        for seq_start in (True, False):
            @pl.when(c == 0 if seq_start else c != 0)
            def _(group=group, seq_start=seq_start):
                lax.fori_loop(0, n_units // GROUP, functools.partial(group, seq_start=seq_start), 0)

    l0, l1, l2 = lse_ref[0], lse_ref[1], lse_ref[2]
    mx = jnp.maximum(jnp.maximum(l0, l1), l2)
    w0, w1, w2 = jnp.exp(l0 - mx), jnp.exp(l1 - mx), jnp.exp(l2 - mx)
    inv = 1.0 / (w0 + w1 + w2)
    o_ref[...] = (acc_ref[0] * (w0 * inv) + acc_ref[1] * (w1 * inv) + acc_ref[2] * (w2 * inv)).astype(o_ref.dtype)


def band_attention(qkv, bias_tbl, batch, seq):
    n_chunk = seq // CHUNK
    n_pair = A_WIDTH // LANES
    blk = (CHUNK, LANES)
    in_specs = []
    for branch in range(N_BRANCH):
        qcol, kcol, vcol = [(3 * branch + kind) * n_pair for kind in range(3)]

        def cur(col):
            return lambda b, c, hp, col=col: (b * n_chunk + c, col + hp)

        def prev(col):
            return lambda b, c, hp, col=col: (b * n_chunk + jnp.maximum(c - 1, 0), col + hp)

        in_specs += [pl.BlockSpec(blk, cur(qcol)), pl.BlockSpec(blk, prev(kcol)), pl.BlockSpec(blk, cur(kcol)),
                     pl.BlockSpec(blk, prev(vcol)), pl.BlockSpec(blk, cur(vcol))]
    in_specs.append(pl.BlockSpec((N_BRANCH, 2, BAND, 2 * BAND), lambda b, c, hp: (0, hp, 0, 0)))
    return pl.pallas_call(
        _band_attn_kernel,
        grid=(batch, n_chunk, n_pair),
        in_specs=in_specs,
        out_specs=pl.BlockSpec(blk, lambda b, c, hp: (b * n_chunk + c, hp)),
        out_shape=jax.ShapeDtypeStruct((batch * seq, A_WIDTH), BF16),
        scratch_shapes=[pltpu.VMEM((N_BRANCH, CHUNK, LANES), F32)] * 2
        + [pltpu.VMEM((N_BRANCH - 1, CHUNK, LANES), BF16)] * 2,
        compiler_params=_params("parallel", "parallel", "arbitrary"),
        name="band_attention",
    )(*([qkv] * 15), bias_tbl)


def _mix_out_kernel(a_ref, u_ref, halo_ref, x_ref, wpool_ref, scale_ref, woa_ref, wop_ref, o_ref, ext_ref,
                    *, tiles_per_seq):
    i = pl.program_id(0)
    tm = a_ref.shape[0]
    halo = halo_ref.shape[0]
    seq_tile = i % tiles_per_seq
    ext_ref[0:halo, :] = jnp.where(seq_tile == 0, 0.0, halo_ref[...])
    ext_ref[halo:halo + tm, :] = u_ref[...]
    pos = seq_tile * tm + lax.broadcasted_iota(jnp.int32, (tm, POOL_GROUP_DIM), 0)
    acc = x_ref[...] + jnp.dot(a_ref[...], woa_ref[...], preferred_element_type=F32)
    for grp, win in enumerate(POOL_WINDOWS):
        cols = slice(grp * POOL_GROUP_DIM, (grp + 1) * POOL_GROUP_DIM)
        tot = ext_ref[halo:halo + tm, cols]
        for back in range(1, win):
            tot = tot + ext_ref[halo - back:halo - back + tm, cols]
        cnt = jnp.minimum(pos + 1, win).astype(F32)
        mix = tot / cnt - ext_ref[halo:halo + tm, cols]
        y = jnp.dot(mix.astype(BF16), wpool_ref[grp], preferred_element_type=F32) * scale_ref[:, cols]
        acc = acc + jnp.dot(y.astype(BF16), wop_ref[cols, :], preferred_element_type=F32)
    o_ref[...] = acc


def mix_out(a, u, x, w_pool, pool_scale, w_out_a, w_out_p, tm, seq):
    m, d = x.shape
    halo = 16
    assert seq % tm == 0 and tm % halo == 0
    kern = functools.partial(_mix_out_kernel, tiles_per_seq=seq // tm)
    return pl.pallas_call(
        kern,
        grid=(m // tm,),
        in_specs=[
            pl.BlockSpec((tm, A_WIDTH), lambda i: (i, 0)),
            pl.BlockSpec((tm, POOL_WIDTH), lambda i: (i, 0)),
            pl.BlockSpec((halo, POOL_WIDTH), lambda i: (jnp.maximum(i * (tm // halo) - 1, 0), 0)),
            pl.BlockSpec((tm, d), lambda i: (i, 0)),
            pl.BlockSpec((len(POOL_WINDOWS), POOL_GROUP_DIM, POOL_GROUP_DIM), lambda i: (0, 0, 0)),
            pl.BlockSpec((1, POOL_WIDTH), lambda i: (0, 0)),
            pl.BlockSpec((A_WIDTH, d), lambda i: (0, 0)),
            pl.BlockSpec((POOL_WIDTH, d), lambda i: (0, 0)),
        ],
        out_specs=pl.BlockSpec((tm, d), lambda i: (i, 0)),
        out_shape=jax.ShapeDtypeStruct((m, d), F32),
        scratch_shapes=[pltpu.VMEM((tm + halo, POOL_WIDTH), F32)],
        compiler_params=_params("parallel"),
        name="mix_out",
    )(a, u, u, x, w_pool, pool_scale, w_out_a, w_out_p)


def _mix_out_sample_kernel(a_ref, mix_ref, x_ref, wpool_ref, scale_ref, woa_ref, wop_ref, o_ref):
    acc = x_ref[...] + jnp.dot(a_ref[...].astype(BF16), woa_ref[...], preferred_element_type=F32)
    for grp in range(len(POOL_WINDOWS)):
        cols = slice(grp * POOL_GROUP_DIM, (grp + 1) * POOL_GROUP_DIM)
        y = jnp.dot(mix_ref[:, cols].astype(BF16), wpool_ref[grp], preferred_element_type=F32) * scale_ref[:, cols]
        acc = acc + jnp.dot(y.astype(BF16), wop_ref[cols, :], preferred_element_type=F32)
    o_ref[...] = acc


def mix_out_sample(a, mix, x, w_pool, pool_scale, w_out_a, w_out_p):
    m, d = x.shape
    return pl.pallas_call(
        _mix_out_sample_kernel,
        out_shape=jax.ShapeDtypeStruct((m, d), F32),
        compiler_params=pltpu.CompilerParams(vmem_limit_bytes=VMEM_LIMIT_BYTES),
        name="mix_out_sample",
    )(a, mix, x, w_pool, pool_scale, w_out_a, w_out_p)


def _top2(logits):
    lane = lax.broadcasted_iota(jnp.int32, logits.shape, 1)
    lg = jnp.where(lane < N_EXPERTS, logits, -jnp.inf)
    m1 = jnp.max(lg, axis=-1, keepdims=True)
    i1 = jnp.min(jnp.where(lg == m1, lane, LANES), axis=-1, keepdims=True)
    rest = jnp.where(lane == i1, -jnp.inf, lg)
    m2 = jnp.max(rest, axis=-1, keepdims=True)
    i2 = jnp.min(jnp.where(rest == m2, lane, LANES), axis=-1, keepdims=True)
    e2 = jnp.exp(m2 - m1)
    return lane, i1, i2, 1.0 / (1.0 + e2), e2 / (1.0 + e2)


def _top2_gates(logits):
    lane, i1, i2, g1, g2 = _top2(logits)
    return jnp.where(lane == i1, g1, 0.0) + jnp.where(lane == i2, g2, 0.0)


ROUTE_E1, ROUTE_E2, ROUTE_R1, ROUTE_R2, ROUTE_G1, ROUTE_G2 = range(6)
EXPERT_TILE = 512


def _routing_record(hn, wr_ref, tri_ref):
    logits = jnp.dot(hn, wr_ref[...], preferred_element_type=F32)
    lane, i1, i2, g1, g2 = _top2(logits)
    chosen = jnp.where(jnp.logical_or(lane == i1, lane == i2), 1.0, 0.0).astype(BF16)
    before = jnp.dot(tri_ref[...], chosen, preferred_element_type=F32)
    r1 = jnp.sum(jnp.where(lane == i1, before, 0.0), axis=-1, keepdims=True)
    r2 = jnp.sum(jnp.where(lane == i2, before, 0.0), axis=-1, keepdims=True)
    rec = jnp.zeros(logits.shape, F32)
    for pos, val in ((ROUTE_E1, i1.astype(F32)), (ROUTE_E2, i2.astype(F32)), (ROUTE_R1, r1), (ROUTE_R2, r2),
                     (ROUTE_G1, g1), (ROUTE_G2, g2)):
        rec = jnp.where(lane == pos, val, rec)
    return rec


def _row_copy(src, src_row, dst, dst_row, sem):
    return pltpu.make_async_copy(src.at[pl.ds(src_row, 1)], dst.at[pl.ds(dst_row, 1)], sem)


def _scatter_rows_kernel(dest_ref, x_ref, o_hbm, sem):
    tm = x_ref.shape[0]

    def issue(r, carry):
        for k in range(2):
            _row_copy(x_ref, r, o_hbm, dest_ref[0, 0, 2 * r + k], sem).start(priority=k)
        return carry

    def drain(r, carry):
        for k in range(2):
            _row_copy(x_ref, 0, o_hbm, 0, sem).wait()
        return carry

    lax.fori_loop(0, tm, issue, 0, unroll=8)
    lax.fori_loop(0, tm, drain, 0, unroll=8)


def scatter_rows(x, dest, tm):
    m, d = x.shape
    dest3 = dest.reshape(m // tm, 1, 2 * tm)
    return pl.pallas_call(
        _scatter_rows_kernel,
        grid=(m // tm,),
        in_specs=[
            pl.BlockSpec((1, 1, 2 * tm), lambda i: (i, 0, 0), memory_space=pltpu.SMEM),
            pl.BlockSpec((tm, d), lambda i: (i, 0)),
        ],
        out_specs=pl.BlockSpec(memory_space=pl.ANY),
        out_shape=jax.ShapeDtypeStruct((2 * m, d), x.dtype),
        scratch_shapes=[pltpu.SemaphoreType.DMA(())],
        compiler_params=_params("arbitrary"),
        name="scatter_rows",
    )(dest3, x)


def _expert_kernel(tile_ref, exp_ref, lo_ref, hi_ref, nvis_ref, x_ref, g_ref, wg_ref, wu_ref, wd_ref, o_ref,
                   hn_ref, acc_ref):
    v = pl.program_id(0)
    j = pl.program_id(1)

    @pl.when(v < nvis_ref[0])
    def _():
        @pl.when(j == 0)
        def _():
            hn_ref[...] = _rmsnorm(x_ref[...], g_ref[...]).astype(BF16)
            acc_ref[...] = jnp.zeros_like(acc_ref)

        hn = hn_ref[...]
        a = jnp.dot(hn, wg_ref[...], preferred_element_type=F32)
        b = jnp.dot(hn, wu_ref[...], preferred_element_type=F32)
        h = (a * _sigmoid(a) * b).astype(BF16)
        acc_ref[...] += jnp.dot(h, wd_ref[...], preferred_element_type=F32)

        @pl.when(j == pl.num_programs(1) - 1)
        def _():
            row = lax.broadcasted_iota(jnp.int32, (o_ref.shape[0], 1), 0)
            mine = jnp.logical_and(row >= lo_ref[v], row < hi_ref[v])
            first = jnp.logical_or(v == 0, tile_ref[v] != tile_ref[jnp.maximum(v - 1, 0)])

            @pl.when(first)
            def _():
                o_ref[...] = jnp.where(mine, acc_ref[...], 0.0)

            @pl.when(jnp.logical_not(first))
            def _():
                o_ref[...] = jnp.where(mine, acc_ref[...], o_ref[...])


def expert_swiglu(xs, g, w_gate, w_up, w_down, visits, tm, tf):
    m, d = xs.shape
    f = w_gate.shape[2]
    n_f = f // tf
    assert m % tm == 0 and f % tf == 0
    tile, expert, lo, hi, n_visit = visits

    def f_blk(v, j):
        return j + (v % 2) * (n_f - 1 - 2 * j)

    grid_spec = pltpu.PrefetchScalarGridSpec(
        num_scalar_prefetch=5,
        grid=(tile.shape[0], n_f),
        in_specs=[
            pl.BlockSpec((tm, d), lambda v, j, t, e, lo, hi, n: (t[v], 0)),
            pl.BlockSpec((1, d), lambda v, j, t, e, lo, hi, n: (0, 0)),
            pl.BlockSpec((None, d, tf), lambda v, j, t, e, lo, hi, n: (e[v], 0, f_blk(v, j))),
            pl.BlockSpec((None, d, tf), lambda v, j, t, e, lo, hi, n: (e[v], 0, f_blk(v, j))),
            pl.BlockSpec((None, tf, d), lambda v, j, t, e, lo, hi, n: (e[v], f_blk(v, j), 0)),
        ],
        out_specs=pl.BlockSpec((tm, d), lambda v, j, t, e, lo, hi, n: (t[v], 0)),
        scratch_shapes=[pltpu.VMEM((tm, d), BF16), pltpu.VMEM((tm, d), F32)],
    )
    return pl.pallas_call(
        _expert_kernel,
        grid_spec=grid_spec,
        out_shape=jax.ShapeDtypeStruct((m, d), F32),
        compiler_params=_params("arbitrary", "arbitrary"),
        name="expert_swiglu",
    )(tile, expert, lo, hi, n_visit, xs, g, w_gate, w_up, w_down)


def _combine_kernel(dest_ref, next_ref, x_ref, rec_ref, gf_ref, ys_hbm, o_ref, y_ref, sem):
    i = pl.program_id(0)
    tm = x_ref.shape[0]

    def fetch(idx_ref, slot):
        def issue(r, carry):
            for k in range(2):
                _row_copy(ys_hbm, idx_ref[0, 0, 2 * r + k], y_ref.at[slot, k], r, sem.at[slot]).start(priority=k)
            return carry

        lax.fori_loop(0, tm, issue, 0, unroll=8)

    def finish(slot):
        def drain(r, carry):
            for k in range(2):
                _row_copy(ys_hbm, 0, y_ref.at[slot, k], 0, sem.at[slot]).wait()
            return carry

        lax.fori_loop(0, tm, drain, 0, unroll=8)
        g1 = rec_ref[:, ROUTE_G1:ROUTE_G1 + 1]
        g2 = rec_ref[:, ROUTE_G2:ROUTE_G2 + 1]
        out = x_ref[...] + (g1 * y_ref[slot, 0] + g2 * y_ref[slot, 1])
        o_ref[...] = _rmsnorm(out, gf_ref[...])

    @pl.when(i == 0)
    def _():
        fetch(dest_ref, 0)

    for slot in range(2):
        @pl.when(i % 2 == slot)
        def _(slot=slot):
            @pl.when(i + 1 < pl.num_programs(0))
            def _():
                fetch(next_ref, 1 - slot)

            finish(slot)


def combine_rows(x, rec, dest, ys, g_final, tm):
    m, d = x.shape
    n_tile = m // tm
    dest3 = dest.reshape(n_tile, 1, 2 * tm)
    return pl.pallas_call(
        _combine_kernel,
        grid=(n_tile,),
        in_specs=[
            pl.BlockSpec((1, 1, 2 * tm), lambda i: (i, 0, 0), memory_space=pltpu.SMEM),
            pl.BlockSpec((1, 1, 2 * tm), lambda i: (jnp.minimum(i + 1, n_tile - 1), 0, 0), memory_space=pltpu.SMEM),
            pl.BlockSpec((tm, d), lambda i: (i, 0)),
            pl.BlockSpec((tm, LANES), lambda i: (i, 0)),
            pl.BlockSpec((1, d), lambda i: (0, 0)),
            pl.BlockSpec(memory_space=pl.ANY),
        ],
        out_specs=pl.BlockSpec((tm, d), lambda i: (i, 0)),
        out_shape=jax.ShapeDtypeStruct((m, d), F32),
        scratch_shapes=[pltpu.VMEM((2, 2, tm, d), F32), pltpu.SemaphoreType.DMA((2,))],
        compiler_params=_params("arbitrary"),
        name="combine_rows",
    )(dest3, dest3, x, rec, g_final, ys)


def _visit_list(rec, tile_tokens, tm):
    n = rec.shape[0]
    e = rec[:, ROUTE_E1:ROUTE_E2 + 1].astype(jnp.int32)
    rank = rec[:, ROUTE_R1:ROUTE_R2 + 1].astype(jnp.int32)
    onehot = (e[:, :, None] == jnp.arange(N_EXPERTS)[None, None, :]).astype(jnp.int32)
    per_tile = onehot.reshape(n // tile_tokens, tile_tokens * 2, N_EXPERTS).sum(axis=1)
    count = per_tile.sum(axis=0)
    start = jnp.cumsum(count) - count
    tile_base = jnp.cumsum(per_tile, axis=0) - per_tile
    base = (start[None, :] + tile_base)[:, None, :]
    dest = rank + jnp.sum(onehot.reshape(n // tile_tokens, tile_tokens * 2, N_EXPERTS) * base, axis=-1).reshape(n, 2)

    n_tile = 2 * n // tm
    n_visit_max = n_tile + N_EXPERTS
    end = start + count
    first_tile = start // tm
    last_tile = jnp.maximum(end - 1, start) // tm
    n_vis = jnp.where(count > 0, last_tile - first_tile + 1, 0)
    vis_end = jnp.cumsum(n_vis)
    vis_start = vis_end - n_vis
    total = vis_end[-1]
    v = jnp.arange(n_visit_max)
    vc = jnp.minimum(v, total - 1)
    ex = jnp.sum(vc[:, None] >= vis_end[None, :], axis=1)
    tile = first_tile[ex] + vc - vis_start[ex]
    lo = jnp.clip(start[ex] - tile * tm, 0, tm)
    hi = jnp.clip(end[ex] - tile * tm, 0, tm)
    live = v < total
    as_i32 = lambda a: a.astype(jnp.int32)
    visits = (as_i32(tile), as_i32(ex), as_i32(jnp.where(live, lo, 0)), as_i32(jnp.where(live, hi, 0)),
              as_i32(total).reshape(1))
    return dest.astype(jnp.int32), visits


def routed_experts(x, rec, rank_tile, g, w_gate, w_up, w_down, g_final):
    dest, visits = _visit_list(rec, rank_tile, EXPERT_TILE)
    xs = scatter_rows(x, dest, 512)
    ys = expert_swiglu(xs, g, w_gate, w_up, w_down, visits, EXPERT_TILE, w_gate.shape[2] // 2)
    return combine_rows(x, rec, dest, ys, g_final, 256)


def _swiglu_kernel(*refs, n_expert, final_norm):
    x_ref, g_ref = refs[0], refs[1]
    k = 2
    if n_expert > 1:
        wr_ref = refs[k]
        k += 1
    wg_ref, wu_ref, wd_ref = refs[k:k + 3]
    k += 3
    if final_norm:
        gf_ref = refs[k]
        k += 1
    o_ref, hn_ref, acc_ref = refs[k:k + 3]
    if n_expert > 1:
        gate_ref = refs[k + 3]
    e = pl.program_id(1)
    j = pl.program_id(2)

    @pl.when(jnp.logical_and(e == 0, j == 0))
    def _():
        hn = _rmsnorm(x_ref[...], g_ref[...])
        hn_ref[...] = hn.astype(BF16)
        acc_ref[...] = jnp.zeros_like(acc_ref)
        if n_expert > 1:
            logits = jnp.dot(hn.astype(BF16), wr_ref[...], preferred_element_type=F32)
            gate_ref[...] = _top2_gates(logits)

    hn = hn_ref[...]
    a = jnp.dot(hn, wg_ref[...], preferred_element_type=F32)
    b = jnp.dot(hn, wu_ref[...], preferred_element_type=F32)
    h = (a * _sigmoid(a) * b).astype(BF16)
    y = jnp.dot(h, wd_ref[...], preferred_element_type=F32)
    if n_expert > 1:
        lane = lax.broadcasted_iota(jnp.int32, gate_ref.shape, 1)
        gate = jnp.sum(jnp.where(lane == e, gate_ref[...], 0.0), axis=-1, keepdims=True)
        y = gate * y
    acc_ref[...] += y

    @pl.when(jnp.logical_and(e == n_expert - 1, j == pl.num_programs(2) - 1))
    def _():
        out = x_ref[...] + acc_ref[...]
        if final_norm:
            out = _rmsnorm(out, gf_ref[...])
        o_ref[...] = out


def swiglu_block(x, g, w_gate, w_up, w_down, tm, tf, w_router=None, g_final=None):
    m, d = x.shape
    n_expert, _, f = w_gate.shape
    assert m % tm == 0 and f % tf == 0
    args = [x, g]
    in_specs = [pl.BlockSpec((tm, d), lambda i, e, j: (i, 0)), pl.BlockSpec((1, d), lambda i, e, j: (0, 0))]
    if n_expert > 1:
        args.append(w_router)
        in_specs.append(pl.BlockSpec((d, LANES), lambda i, e, j: (0, 0)))
    args += [w_gate, w_up, w_down]
    in_specs += [
        pl.BlockSpec((None, d, tf), lambda i, e, j: (e, 0, j)),
        pl.BlockSpec((None, d, tf), lambda i, e, j: (e, 0, j)),
        pl.BlockSpec((None, tf, d), lambda i, e, j: (e, j, 0)),
    ]
    if g_final is not None:
        args.append(g_final)
        in_specs.append(pl.BlockSpec((1, d), lambda i, e, j: (0, 0)))
    scratch = [pltpu.VMEM((tm, d), BF16), pltpu.VMEM((tm, d), F32)]
    if n_expert > 1:
        scratch.append(pltpu.VMEM((tm, LANES), F32))
    kern = functools.partial(_swiglu_kernel, n_expert=n_expert, final_norm=g_final is not None)
    return pl.pallas_call(
        kern,
        grid=(m // tm, n_expert, f // tf),
        in_specs=in_specs,
        out_specs=pl.BlockSpec((tm, d), lambda i, e, j: (i, 0)),
        out_shape=jax.ShapeDtypeStruct((m, d), F32),
        scratch_shapes=scratch,
        compiler_params=_params("parallel", "arbitrary", "arbitrary"),
        name="swiglu_block",
    )(*args)


def _ln_silu_pw2(z, lng_ref, lnb_ref, w2_ref, b2_ref):
    mu = jnp.mean(z, axis=-1, keepdims=True)
    zc = z - mu
    var = jnp.mean(zc * zc, axis=-1, keepdims=True)
    zn = zc * lax.rsqrt(var + LN_EPS) * lng_ref[...] + lnb_ref[...]
    act = (zn * _sigmoid(zn)).astype(BF16)
    return jnp.dot(act, w2_ref[...], preferred_element_type=F32) + b2_ref[...]


def _conv_kernel(glu_ref, halo_ref, x_ref, wdw_ref, bdw_ref, lng_ref, lnb_ref, w2_ref, b2_ref, gr_ref, wr_ref, tri_ref,
                 o_ref, rec_ref, ext_ref, z_ref, *, tiles_per_seq, row_chunk):
    i = pl.program_id(0)
    tm = glu_ref.shape[0]
    halo = halo_ref.shape[0]
    n_slab = glu_ref.shape[1] // LANES
    at_start = i % tiles_per_seq == 0
    for cb in range(n_slab):
        cols = slice(cb * LANES, (cb + 1) * LANES)
        ext_ref[cb, 0:halo, :] = jnp.where(at_start, 0.0, _round_bf16(halo_ref[:, cols]))
        ext_ref[cb, halo:halo + tm, :] = _round_bf16(glu_ref[:, cols])
    base = halo - (CONV_K - 1)

    def slab(cb, carry):
        for rc in range(tm // row_chunk):
            r0 = rc * row_chunk
            acc = jnp.zeros((row_chunk, LANES), F32) + bdw_ref[cb]
            for tap in range(CONV_K):
                acc = acc + ext_ref[cb, r0 + base + tap:r0 + base + tap + row_chunk, :] * wdw_ref[cb, tap:tap + 1, :]
            z_ref[cb, r0:r0 + row_chunk, :] = acc
        return carry

    lax.fori_loop(0, n_slab, slab, 0)
    z = jnp.concatenate([z_ref[cb] for cb in range(n_slab)], axis=-1)
    out = x_ref[...] + _ln_silu_pw2(z, lng_ref, lnb_ref, w2_ref, b2_ref)
    o_ref[...] = out
    rec_ref[...] = _routing_record(_rmsnorm(out, gr_ref[...]).astype(BF16), wr_ref, tri_ref)


def conv_block(glu, x, w_dw, b_dw, ln_g, ln_b, w_pw2, b_pw2, g_route, w_router, tm, seq):
    m, d = x.shape
    tri = (jnp.arange(tm)[:, None] > jnp.arange(tm)[None, :]).astype(BF16)
    halo = 32
    n_slab = d // LANES
    assert seq % tm == 0 and tm % halo == 0
    kern = functools.partial(_conv_kernel, tiles_per_seq=seq // tm, row_chunk=64)
    vec = pl.BlockSpec((1, d), lambda i: (0, 0))
    w_dw_slab = jnp.transpose(w_dw.reshape(w_dw.shape[0], n_slab, LANES), (1, 0, 2))
    b_dw_slab = b_dw.reshape(n_slab, 1, LANES)
    return pl.pallas_call(
        kern,
        grid=(m // tm,),
        in_specs=[
            pl.BlockSpec((tm, d), lambda i: (i, 0)),
            pl.BlockSpec((halo, d), lambda i: (jnp.maximum(i * (tm // halo) - 1, 0), 0)),
            pl.BlockSpec((tm, d), lambda i: (i, 0)),
            pl.BlockSpec(w_dw_slab.shape, lambda i: (0, 0, 0)),
            pl.BlockSpec(b_dw_slab.shape, lambda i: (0, 0, 0)),
            vec, vec,
            pl.BlockSpec((d, d), lambda i: (0, 0)),
            vec,
            vec,
            pl.BlockSpec((d, LANES), lambda i: (0, 0)),
            pl.BlockSpec((tm, tm), lambda i: (0, 0)),
        ],
        out_specs=[pl.BlockSpec((tm, d), lambda i: (i, 0)), pl.BlockSpec((tm, LANES), lambda i: (i, 0))],
        out_shape=[jax.ShapeDtypeStruct((m, d), F32), jax.ShapeDtypeStruct((m, LANES), F32)],
        scratch_shapes=[pltpu.VMEM((n_slab, tm + halo, LANES), F32), pltpu.VMEM((n_slab, tm, LANES), F32)],
        compiler_params=_params("parallel"),
        name="conv_block",
    )(glu, glu, x, w_dw_slab, b_dw_slab, ln_g, ln_b, w_pw2, b_pw2, g_route, w_router, tri)


def _conv_sample_kernel(ext_ref, x_ref, wdw_ref, bdw_ref, lng_ref, lnb_ref, w2_ref, b2_ref, o_ref, z_ref,
                        *, steps):
    n = ext_ref.shape[0]
    z_ref[...] = jnp.zeros_like(z_ref)

    def one(s, carry):
        acc = jnp.zeros((steps, z_ref.shape[2]), F32) + bdw_ref[...]
        for tap in range(CONV_K):
            acc = acc + _round_bf16(ext_ref[s, tap:tap + steps, :]) * wdw_ref[tap:tap + 1, :]
        z_ref[s, 0:steps, :] = acc
        return carry

    lax.fori_loop(0, n, one, 0)
    z = z_ref[...].reshape(n * z_ref.shape[1], z_ref.shape[2])
    y = _ln_silu_pw2(z, lng_ref, lnb_ref, w2_ref, b2_ref)
    o_ref[...] = x_ref[...] + y.reshape(o_ref.shape)


def conv_block_sample(ext, x, w_dw, b_dw, ln_g, ln_b, w_pw2, b_pw2, steps):
    n, _, d = ext.shape
    kern = functools.partial(_conv_sample_kernel, steps=steps)
    return pl.pallas_call(
        kern,
        out_shape=jax.ShapeDtypeStruct((n, 8, d), F32),
        scratch_shapes=[pltpu.VMEM((n, 8, d), F32)],
        compiler_params=pltpu.CompilerParams(vmem_limit_bytes=VMEM_LIMIT_BYTES),
        name="conv_block_sample",
    )(ext, x, w_dw, b_dw, ln_g, ln_b, w_pw2, b_pw2)


def _sample_attn_kernel(q_ref, kn_ref, vn_ref, kvt_ref, c0_ref, c1_ref, c2_ref, b0_ref, b1_ref, b2_ref, bn_ref,
                        ext_ref, a_ref, mix_ref, o0_ref, o1_ref, o2_ref, *, steps):
    caches = (c0_ref, c1_ref, c2_ref)
    biases = (b0_ref, b1_ref, b2_ref)
    new_caches = (o0_ref, o1_ref, o2_ref)
    n_head = q_ref.shape[2]
    lane = lax.broadcasted_iota(jnp.int32, (n_head, A_HEAD_DIM, LANES), 2)
    keep = LANES - steps

    outs, lses = [], []
    for branch in range(N_BRANCH):
        c_ref = caches[branch]
        q = (q_ref[0, branch] * (A_HEAD_DIM ** -0.5)).astype(BF16)
        kt = c_ref[0, 0].astype(BF16)
        vt = c_ref[0, 1].astype(BF16)
        kn = kn_ref[0, branch].astype(BF16)
        vn = vn_ref[0, branch].astype(BF16)
        s_c = jnp.einsum("hqd,hdl->hql", q, kt, preferred_element_type=F32) + biases[branch][...]
        s_n = jnp.einsum("hqd,hkd->hqk", q, kn, preferred_element_type=F32) + bn_ref[branch]
        mx = jnp.maximum(jnp.max(s_c, axis=-1, keepdims=True), jnp.max(s_n, axis=-1, keepdims=True))
        den = (jnp.sum(jnp.exp(s_c - mx), axis=-1, keepdims=True)
               + jnp.sum(jnp.exp(s_n - mx), axis=-1, keepdims=True))
        lse = mx + jnp.log(den)
        p_c = jnp.exp(s_c - lse).astype(BF16)
        p_n = jnp.exp(s_n - lse).astype(BF16)
        outs.append(jnp.einsum("hql,hdl->hqd", p_c, vt, preferred_element_type=F32)
                    + jnp.einsum("hqk,hkd->hqd", p_n, vn, preferred_element_type=F32))
        lses.append(lse)
    mx = jnp.maximum(jnp.maximum(lses[0], lses[1]), lses[2])
    wts = [jnp.exp(lse - mx) for lse in lses]
    inv = 1.0 / (wts[0] + wts[1] + wts[2])
    a_ref[0] = outs[0] * (wts[0] * inv) + outs[1] * (wts[1] * inv) + outs[2] * (wts[2] * inv)

    for branch in range(N_BRANCH):
        c_ref, o_ref = caches[branch], new_caches[branch]
        n_tile = c_ref.shape[-1] // LANES
        for kv in range(2):
            nxt = pltpu.roll(c_ref[0, kv, :, :, 0:LANES], keep, 2)
            for j in range(n_tile):
                cur = nxt
                if j + 1 < n_tile:
                    nxt = pltpu.roll(c_ref[0, kv, :, :, (j + 1) * LANES:(j + 2) * LANES], keep, 2)
                else:
                    nxt = kvt_ref[0, branch, kv]
                o_ref[0, kv, :, :, j * LANES:(j + 1) * LANES] = jnp.where(lane < keep, cur, nxt)

    base = ext_ref.shape[1] - steps
    for t in range(steps):
        for grp, win in enumerate(POOL_WINDOWS):
            cols = slice(grp * POOL_GROUP_DIM, (grp + 1) * POOL_GROUP_DIM)
            last = base + t
            tot = jnp.sum(ext_ref[0, last - win + 1:last + 1, cols], axis=0, keepdims=True)
            mix_ref[0, t:t + 1, cols] = tot / float(win) - ext_ref[0, last:last + 1, cols]


def sample_attention(q, kn, vn, kvt, caches, biases, bias_new, ext_pool, steps):
    n, rows = q.shape[0], q.shape[3]
    hg = A_HEADS // 2
    per_head = (hg, rows, A_HEAD_DIM)
    in_specs = [pl.BlockSpec((1, N_BRANCH) + per_head, lambda i, j: (i, 0, j, 0, 0))] * 3
    in_specs.append(pl.BlockSpec((1, N_BRANCH, 2, hg, A_HEAD_DIM, LANES), lambda i, j: (i, 0, 0, j, 0, 0)))
    cache_specs = [pl.BlockSpec((1, 2, hg, A_HEAD_DIM, c.shape[-1]), lambda i, j: (i, 0, j, 0, 0)) for c in caches]
    in_specs += cache_specs
    in_specs += [pl.BlockSpec((hg, rows, b.shape[-1]), lambda i, j: (j, 0, 0)) for b in biases]
    in_specs += [
        pl.BlockSpec((N_BRANCH, hg, rows, rows), lambda i, j: (0, j, 0, 0)),
        pl.BlockSpec((1,) + ext_pool.shape[1:], lambda i, j: (i, 0, 0)),
    ]
    kern = functools.partial(_sample_attn_kernel, steps=steps)
    return pl.pallas_call(
        kern,
        grid=(n, A_HEADS // hg),
        in_specs=in_specs,
        out_specs=[
            pl.BlockSpec((1,) + per_head, lambda i, j: (i, j, 0, 0)),
            pl.BlockSpec((1, steps, POOL_WIDTH), lambda i, j: (i, 0, 0)),
        ] + cache_specs,
        out_shape=[
            jax.ShapeDtypeStruct((n, A_HEADS, rows, A_HEAD_DIM), F32),
            jax.ShapeDtypeStruct((n, steps, POOL_WIDTH), F32),
        ] + [jax.ShapeDtypeStruct(c.shape, F32) for c in caches],
        compiler_params=_params("parallel", "arbitrary"),
        name="sample_attention",
    )(q, kn, vn, kvt, *caches, *biases, bias_new, ext_pool)


def _t5_bucket(dist):
    max_exact = NUM_BUCKETS // 2
    d = jnp.maximum(dist, 1).astype(F32)
    large = max_exact + (jnp.log(d / max_exact) / math.log(REL_MAX_DIST / max_exact)
                         * (NUM_BUCKETS - max_exact)).astype(jnp.int32)
    return jnp.where(dist < max_exact, dist, jnp.minimum(large, NUM_BUCKETS - 1))


def _slot_bias(rel_bias):
    rb = rel_bias.astype(F32).reshape(NUM_BUCKETS, N_BRANCH, A_HEADS)
    slot = jnp.arange(BAND + 1)
    out = []
    for g, (_, dil) in enumerate(A_GROUPS):
        pick = jax.nn.one_hot(_t5_bucket(slot * dil), NUM_BUCKETS, dtype=F32)
        out.append(jnp.einsum("sb,bh->hs", pick, rb[:, g], precision=lax.Precision.HIGHEST))
    return jnp.stack(out)


def _sliding_rows(v, n_rows, width):
    period = v.shape[-1]
    assert period >= n_rows + width - 1
    flat = jnp.tile(v, (1,) * (v.ndim - 1) + (n_rows + 1,))[..., :(period + 1) * n_rows]
    return flat.reshape(v.shape[:-1] + (n_rows, period + 1))[..., :width]


def _prompt_bias_tables(rel_bias):
    sb = _slot_bias(rel_bias)
    lead = sb.shape[:2]
    diag = jnp.concatenate([jnp.full(lead + (BAND - 1,), NEG, F32), sb[..., ::-1],
                            jnp.full(lead + (BAND,), NEG, F32)], axis=-1)
    return _sliding_rows(diag, BAND, 2 * BAND)[:, :, ::-1]


def _sample_bias_tables(rel_bias, steps, rows):
    sb = _slot_bias(rel_bias)
    cache_tables = []
    for g, (window, dil) in enumerate(A_GROUPS):
        spread = jnp.concatenate([sb[g][..., None], jnp.full(sb[g].shape + (dil - 1,), NEG, F32)], axis=-1)
        by_dist = spread.reshape(A_HEADS, (BAND + 1) * dil)
        need = window + rows
        by_dist = jnp.pad(by_dist, ((0, 0), (0, max(need - by_dist.shape[1], 0))), constant_values=NEG)[:, :need]
        by_dist = jnp.where(jnp.arange(need)[None, :] <= window, by_dist, NEG)
        cache_tables.append(_sliding_rows(by_dist[:, ::-1], rows, window)[:, ::-1])
    t_q = jnp.arange(rows)[:, None]
    t_k = jnp.arange(rows)[None, :]
    new_tables = []
    for g, (_, dil) in enumerate(A_GROUPS):
        diff = t_q - t_k
        ok = (diff >= 0) & (diff % dil == 0) & (t_k < steps)
        pick = jax.nn.one_hot(jnp.where(ok, diff // dil, BAND + 1).reshape(-1), BAND + 1, dtype=F32)
        vals = jnp.einsum("ps,hs->hp", pick, sb[g], precision=lax.Precision.HIGHEST).reshape(A_HEADS, rows, rows)
        new_tables.append(jnp.where(ok[None], vals, NEG))
    return cache_tables, jnp.stack(new_tables)


def kernel(x_prompt, x_sample, cache_kv_w128, cache_kv_w512, cache_kv_w2048, state_pool, state_conv, norm_mix0, w_in0, rel_bias, w_pool, pool_scale, w_out0, norm_ffn0, w_ff_gate, w_ff_up, w_ff_down, norm_mix1, w_pw1, b_pw1, w_dw, b_dw, ln_g, ln_b, w_pw2, b_pw2, norm_ffn1, w_router, we_gate, we_up, we_down, norm_final):
    batch, seq, d = x_prompt.shape
    n_s, steps, _ = x_sample.shape
    caches_in = (cache_kv_w128, cache_kv_w512, cache_kv_w2048)
    assert norm_mix0.shape[0] == 1 and norm_mix1.shape[0] == 1, "two layers: one mixer of each kind"
    assert seq % CHUNK == 0 and steps <= min(dil for _, dil in A_GROUPS[1:])
    for c, (window, _) in zip(caches_in, A_GROUPS):
        assert c.shape[2] == window, "cache must hold a full window"
    qkv_w = N_BRANCH * 3 * A_WIDTH

    w_in = w_in0[0]
    w_in_bf = w_in.astype(BF16)
    w_kv = w_in_bf[:, A_WIDTH * N_BRANCH:qkv_w]
    g_mix0, g_ffn0 = norm_mix0, norm_ffn0
    g_mix1, g_ffn1 = norm_mix1, norm_ffn1
    g_final = norm_final.reshape(1, d)
    wpool_bf = w_pool[0].astype(BF16)
    w_out_a = w_out0[0, :A_WIDTH].astype(BF16)
    w_out_p = w_out0[0, A_WIDTH:].astype(BF16)
    wff = (w_ff_gate.astype(BF16), w_ff_up.astype(BF16), w_ff_down.astype(BF16))
    w_pw1_bf = w_pw1[0].astype(BF16)
    w_pw2_bf = w_pw2[0].astype(BF16)
    w_dw_pad = jnp.pad(w_dw[0], ((0, 32 - CONV_K), (0, 0)))
    wex = (we_gate[0].astype(BF16), we_up[0].astype(BF16), we_down[0].astype(BF16))
    w_router_pad = jnp.pad(w_router[0], ((0, 0), (0, LANES - N_EXPERTS))).astype(BF16)
    zeros = lambda n: jnp.zeros((1, n), F32)

    xp = x_prompt.reshape(batch * seq, d)
    qkv, u_p = qkv_perm(xp, g_mix0, w_in_bf)
    a_p = band_attention(qkv, _prompt_bias_tables(rel_bias), batch, seq)
    h_p = mix_out(a_p, u_p, xp, wpool_bf, pool_scale, w_out_a, w_out_p, 512, seq)
    h_p = swiglu_block(h_p, g_ffn0, *wff, tm=512, tf=1408)
    glu_p = rms_glu(h_p, g_mix1, w_pw1_bf, b_pw1, 1024, 512)
    conv_tile = 256
    h_p, rec_p = conv_block(glu_p, h_p, w_dw_pad, b_dw, ln_g, ln_b, w_pw2_bf, b_pw2, g_ffn1, w_router_pad,
                            conv_tile, seq)
    y_p = routed_experts(h_p, rec_p, conv_tile, g_ffn1, *wex, g_final)

    keep = A_GROUPS[-1][0]
    kv_tail = rms_matmul_t(x_prompt, g_mix0, w_kv.T, seq - keep, keep, 1024, 512)
    kv_tail = kv_tail.reshape(batch, 2, N_BRANCH, A_HEADS, A_HEAD_DIM, keep)
    kv_p = [jnp.transpose(kv_tail[:, :, g, :, :, keep - window:], (0, 4, 1, 2, 3))[None]
            for g, (window, _) in enumerate(A_GROUPS)]
    u_p3 = u_p.reshape(batch, seq, POOL_WIDTH)
    pool_p = u_p3[:, seq - POOL_BUF:][None]
    conv_p = glu_p.reshape(batch, seq, d)[:, seq - (CONV_K - 1):][None]

    m_s = n_s * steps
    xs = x_sample.reshape(m_s, d)
    proj = rms_matmul(xs, g_mix0, w_in_bf, zeros(w_in_bf.shape[1]), m_s, 512, F32)
    qkv_s = proj[:, :qkv_w].reshape(n_s, steps, 3, N_BRANCH, A_HEADS, A_HEAD_DIM)
    u_s = proj[:, qkv_w:].reshape(n_s, steps, POOL_WIDTH)
    rows = 8
    per_head = jnp.transpose(qkv_s, (2, 0, 3, 4, 1, 5))
    per_head = jnp.pad(per_head, ((0, 0),) * 4 + ((0, rows - steps), (0, 0)))
    new_cols = jnp.transpose(qkv_s[:, :, 1:], (0, 3, 2, 4, 5, 1))
    new_cols = jnp.pad(new_cols, ((0, 0),) * 5 + ((LANES - steps, 0),))
    caches_t = [jnp.transpose(c[0], (0, 2, 3, 4, 1)) for c in caches_in]
    ext_pool = jnp.concatenate([state_pool[0], u_s], axis=1)
    cache_bias, new_bias = _sample_bias_tables(rel_bias, steps, rows)
    a_s, mix_s, *caches_out = sample_attention(per_head[0], per_head[1], per_head[2], new_cols, caches_t,
                                               cache_bias, new_bias, ext_pool, steps)
    a_s = jnp.transpose(a_s[:, :, :steps], (0, 2, 1, 3)).reshape(m_s, A_WIDTH)
    h_s = mix_out_sample(a_s, mix_s.reshape(m_s, POOL_WIDTH), xs, wpool_bf, pool_scale, w_out_a, w_out_p)
    h_s = swiglu_block(h_s, g_ffn0, *wff, tm=m_s, tf=1408)
    glu_s = rms_glu(h_s, g_mix1, w_pw1_bf, b_pw1, m_s, 512)
    ext_conv = jnp.concatenate([state_conv[0], glu_s.reshape(n_s, steps, d)], axis=1)
    ext_conv_pad = jnp.pad(ext_conv, ((0, 0), (0, 40 - ext_conv.shape[1]), (0, 0)))
    h_s3 = jnp.pad(h_s.reshape(n_s, steps, d), ((0, 0), (0, 8 - steps), (0, 0)))
    h_s = conv_block_sample(ext_conv_pad, h_s3, w_dw_pad, b_dw, ln_g, ln_b, w_pw2_bf, b_pw2, steps)
    h_s = h_s[:, :steps].reshape(m_s, d)
    y_s = swiglu_block(h_s, g_ffn1, *wex, tm=m_s, tf=896, w_router=w_router_pad, g_final=g_final)

    kv_s = [jnp.transpose(c, (0, 4, 1, 2, 3))[None] for c in caches_out]
    pool_s = ext_pool[:, steps:][None]
    conv_s = ext_conv[:, steps:][None]

    return (y_p.reshape(batch, seq, d), y_s.reshape(n_s, steps, d), kv_p[0], kv_p[1], kv_p[2], pool_p, conv_p,
            kv_s[0], kv_s[1], kv_s[2], pool_s, conv_s)
```

```python
import functools
import math

import jax
import jax.numpy as jnp
from jax import lax
from jax.experimental import pallas as pl
from jax.experimental.pallas import tpu as pltpu

F32 = jnp.float32
BF16 = jnp.bfloat16

A_GROUPS = ((128, 1), (512, 4), (2048, 16))
N_BRANCH = 3
A_HEADS = 8
A_HEAD_DIM = 64
A_WIDTH = 512
BAND = 128
NUM_BUCKETS = 32
REL_MAX_DIST = 2048
POOL_WINDOWS = (2, 4, 8, 16)
POOL_GROUP_DIM = 128
POOL_WIDTH = 512
POOL_BUF = 15
CONV_K = 31
N_EXPERTS = 8
RMS_EPS = 1e-6
LN_EPS = 1e-5
NEG = -1e30

VMEM_LIMIT_BYTES = 56 * 1024 * 1024
LANES = 128
CHUNK = 2048
GROUP = 8


def _params(*sem):
    return pltpu.CompilerParams(dimension_semantics=sem, vmem_limit_bytes=VMEM_LIMIT_BYTES)


def _rmsnorm(x, g):
    return x * lax.rsqrt(jnp.mean(x * x, axis=-1, keepdims=True) + RMS_EPS) * g


def _sigmoid(x):
    return 1.0 / (1.0 + jnp.exp(-x))


def _round_bf16(x):
    return x.astype(BF16).astype(F32)


def _rms_matmul_kernel(x_ref, g_ref, w_ref, b_ref, o_ref, hn_ref):
    @pl.when(pl.program_id(1) == 0)
    def _():
        hn_ref[...] = _rmsnorm(x_ref[...], g_ref[...]).astype(BF16)

    acc = jnp.dot(hn_ref[...], w_ref[...], preferred_element_type=F32)
    o_ref[...] = (acc + b_ref[...]).astype(o_ref.dtype)


def rms_matmul(x, g, w, b, tm, tn, out_dtype):
    m, k = x.shape
    n = w.shape[1]
    assert m % tm == 0 and n % tn == 0
    return pl.pallas_call(
        _rms_matmul_kernel,
        grid=(m // tm, n // tn),
        in_specs=[
            pl.BlockSpec((tm, k), lambda i, j: (i, 0)),
            pl.BlockSpec((1, k), lambda i, j: (0, 0)),
            pl.BlockSpec((k, tn), lambda i, j: (0, j)),
            pl.BlockSpec((1, tn), lambda i, j: (0, j)),
        ],
        out_specs=pl.BlockSpec((tm, tn), lambda i, j: (i, j)),
        out_shape=jax.ShapeDtypeStruct((m, n), out_dtype),
        scratch_shapes=[pltpu.VMEM((tm, k), BF16)],
        compiler_params=_params("parallel", "arbitrary"),
        name="rms_matmul",
    )(x, g, w, b)


def _rms_matmul_t_kernel(x_ref, g_ref, wt_ref, o_ref, hn_ref):
    @pl.when(pl.program_id(2) == 0)
    def _():
        hn_ref[...] = _rmsnorm(x_ref[0], g_ref[...]).astype(BF16)

    nt = (((1,), (1,)), ((), ()))
    o_ref[0] = lax.dot_general(wt_ref[...], hn_ref[...], nt, preferred_element_type=F32)


def rms_matmul_t(x, g, wt, row0, rows, tm, tn):
    batch, _, k = x.shape
    n = wt.shape[0]
    assert row0 % tm == 0 and rows % tm == 0 and n % tn == 0
    return pl.pallas_call(
        _rms_matmul_t_kernel,
        grid=(batch, rows // tm, n // tn),
        in_specs=[
            pl.BlockSpec((1, tm, k), lambda b, i, j: (b, row0 // tm + i, 0)),
            pl.BlockSpec((1, k), lambda b, i, j: (0, 0)),
            pl.BlockSpec((tn, k), lambda b, i, j: (j, 0)),
        ],
        out_specs=pl.BlockSpec((1, tn, tm), lambda b, i, j: (b, j, i)),
        out_shape=jax.ShapeDtypeStruct((batch, n, rows), F32),
        scratch_shapes=[pltpu.VMEM((tm, k), BF16)],
        compiler_params=_params("parallel", "parallel", "arbitrary"),
        name="rms_matmul_t",
    )(x, g, wt)


def _rms_glu_kernel(x_ref, g_ref, wv_ref, wg_ref, bv_ref, bg_ref, o_ref, hn_ref):
    @pl.when(pl.program_id(1) == 0)
    def _():
        hn_ref[...] = _rmsnorm(x_ref[...], g_ref[...]).astype(BF16)

    hn = hn_ref[...]
    val = jnp.dot(hn, wv_ref[...], preferred_element_type=F32) + bv_ref[...]
    gate = jnp.dot(hn, wg_ref[...], preferred_element_type=F32) + bg_ref[...]
    o_ref[...] = val * _sigmoid(gate)


def rms_glu(x, g, w, b, tm, tn):
    m, k = x.shape
    c = w.shape[1] // 2
    nb = c // tn
    return pl.pallas_call(
        _rms_glu_kernel,
        grid=(m // tm, nb),
        in_specs=[
            pl.BlockSpec((tm, k), lambda i, j: (i, 0)),
            pl.BlockSpec((1, k), lambda i, j: (0, 0)),
            pl.BlockSpec((k, tn), lambda i, j: (0, j)),
            pl.BlockSpec((k, tn), lambda i, j: (0, j + nb)),
            pl.BlockSpec((1, tn), lambda i, j: (0, j)),
            pl.BlockSpec((1, tn), lambda i, j: (0, j + nb)),
        ],
        out_specs=pl.BlockSpec((tm, tn), lambda i, j: (i, j)),
        out_shape=jax.ShapeDtypeStruct((m, c), F32),
        scratch_shapes=[pltpu.VMEM((tm, k), BF16)],
        compiler_params=_params("parallel", "arbitrary"),
        name="rms_glu",
    )(x, g, w, w, b, b)


PERM_BLOCK = 256


def _qkv_perm_kernel(x_ref, g_ref, w_ref, perm_ref, o_ref, u_ref, hn_ref):
    j = pl.program_id(1)
    n_qkv = 3 * N_BRANCH

    @pl.when(j == 0)
    def _():
        hn_ref[...] = _rmsnorm(x_ref[...], g_ref[...]).astype(BF16)

    def project():
        return jnp.dot(hn_ref[...], w_ref[...], preferred_element_type=F32)

    @pl.when(j == n_qkv)
    def _():
        u_ref[...] = project()

    @pl.when(j < 3)
    def _():
        o_ref[...] = project().astype(BF16)

    for branch in (1, 2):
        dil = A_GROUPS[branch][1]
        rows = CHUNK // dil
        per = PERM_BLOCK // dil

        @pl.when(jnp.logical_and(j // 3 == branch, j < n_qkv))
        def _():
            res_bf = project().astype(BF16)
            for b in range(CHUNK // PERM_BLOCK):
                blk = jnp.dot(perm_ref[branch - 1], res_bf[b * PERM_BLOCK:(b + 1) * PERM_BLOCK],
                              preferred_element_type=F32).astype(BF16)
                for r in range(dil):
                    o_ref[r * rows + b * per:r * rows + (b + 1) * per, :] = blk[r * per:(r + 1) * per]


def qkv_perm(x, g, w_in):
    m, k = x.shape
    tn = A_WIDTH
    n_qkv = 3 * N_BRANCH
    assert w_in.shape[1] == (n_qkv + 1) * tn
    perms = []
    for _, dil in A_GROUPS[1:]:
        p = jnp.arange(PERM_BLOCK)
        src = (p % (PERM_BLOCK // dil)) * dil + p // (PERM_BLOCK // dil)
        perms.append((src[:, None] == jnp.arange(PERM_BLOCK)[None, :]).astype(BF16))

    def w_block(i, j):
        return 0, jnp.where(j < n_qkv, (j % 3) * N_BRANCH + j // 3, n_qkv)

    return pl.pallas_call(
        _qkv_perm_kernel,
        grid=(m // CHUNK, n_qkv + 1),
        in_specs=[
            pl.BlockSpec((CHUNK, k), lambda i, j: (i, 0)),
            pl.BlockSpec((1, k), lambda i, j: (0, 0)),
            pl.BlockSpec((k, tn), w_block),
            pl.BlockSpec((N_BRANCH - 1, PERM_BLOCK, PERM_BLOCK), lambda i, j: (0, 0, 0)),
        ],
        out_specs=[
            pl.BlockSpec((CHUNK, tn), lambda i, j: (i, jnp.minimum(j, n_qkv - 1))),
            pl.BlockSpec((CHUNK, tn), lambda i, j: (i, 0)),
        ],
        out_shape=[jax.ShapeDtypeStruct((m, n_qkv * tn), BF16), jax.ShapeDtypeStruct((m, tn), F32)],
        scratch_shapes=[pltpu.VMEM((CHUNK, k), BF16)],
        compiler_params=_params("parallel", "arbitrary"),
        name="qkv_perm",
    )(x, g, w_in, jnp.stack(perms))


def _band_attn_kernel(*refs):
    qkv_refs = refs[:15]
    bias_ref, o_ref, acc_ref, lse_ref, kp_ref, vp_ref = refs[15:]
    c = pl.program_id(1)
    n_units = CHUNK // BAND
    grp_rows = GROUP * BAND
    head_of_lane = lax.broadcasted_iota(jnp.int32, (1, 1, LANES), 2) // A_HEAD_DIM
    key_is_prev = lax.broadcasted_iota(jnp.int32, (1, 1, 2 * BAND), 2) < BAND

    for branch, (_, dil) in enumerate(A_GROUPS):
        q_ref, kprev_ref, kcur_ref, vprev_ref, vcur_ref = qkv_refs[5 * branch:5 * branch + 5]
        nblk = n_units // dil
        span = nblk * BAND
        if nblk == 1:
            kp_src, vp_src = kprev_ref, vprev_ref
        else:
            kp_src, vp_src = kp_ref.at[branch], vp_ref.at[branch]
            for res in range(dil):
                lo = res * span
                for dst, prev, cur in ((kp_src, kprev_ref, kcur_ref), (vp_src, vprev_ref, vcur_ref)):
                    dst[lo:lo + BAND, :] = prev[lo + span - BAND:lo + span, :]
                    dst[lo + BAND:lo + span, :] = cur[lo:lo + span - BAND, :]

        def group(gi, carry, seq_start, q_ref=q_ref, kcur_ref=kcur_ref, vcur_ref=vcur_ref, kp_src=kp_src,
                  vp_src=vp_src, nblk=nblk, dil=dil, branch=branch):
            row = pl.multiple_of(gi * grp_rows, grp_rows)

            def blocks(ref):
                return ref[pl.ds(row, grp_rows), :].reshape(GROUP, BAND, LANES)

            q = blocks(q_ref) * jnp.asarray(A_HEAD_DIM ** -0.5, BF16)
            kcat = jnp.concatenate([blocks(kp_src), blocks(kcur_ref)], axis=1)
            vcat = jnp.concatenate([blocks(vp_src), blocks(vcur_ref)], axis=1)
            if seq_start:
                unit = gi * GROUP + lax.broadcasted_iota(jnp.int32, (GROUP, 1, 1), 0)
                pen = jnp.where(jnp.logical_and(unit % nblk == 0, key_is_prev), NEG, 0.0)

            o = jnp.zeros((GROUP, BAND, LANES), F32)
            lse = jnp.zeros((GROUP, BAND, LANES), F32)
            for hh in range(2):
                mine = head_of_lane == hh
                kh = jnp.where(mine, kcat, jnp.zeros_like(kcat))
                vh = jnp.where(mine, vcat, jnp.zeros_like(vcat))
                s = jnp.einsum("uqd,ukd->uqk", q, kh, preferred_element_type=F32) + bias_ref[branch, hh][None]
                if seq_start:
                    s = s + pen
                mx = jnp.max(s, axis=-1, keepdims=True)
                p = jnp.exp(s - mx)
                den = jnp.sum(p, axis=-1, keepdims=True)
                pn = (p * (1.0 / den)).astype(BF16)
                o = o + jnp.einsum("uqk,ukd->uqd", pn, vh, preferred_element_type=F32)
                lse = jnp.where(mine, mx + jnp.log(den), lse)

            if dil == 1:
                acc_ref[branch, pl.ds(row, grp_rows), :] = o.reshape(grp_rows, LANES)
                lse_ref[branch, pl.ds(row, grp_rows), :] = lse.reshape(grp_rows, LANES)
            else:
                per = max(GROUP // nblk, 1)
                blocks_per = GROUP // per
                for k in range(per):
                    if nblk >= GROUP:
                        start = gi // (nblk // GROUP) + dil * BAND * ((gi % (nblk // GROUP)) * GROUP)
                    else:
                        start = gi * per + k
                    rows = pl.ds(start, blocks_per * BAND, stride=dil)
                    part = slice(k * blocks_per, (k + 1) * blocks_per)
                    acc_ref[branch, rows, :] = o[part].reshape(blocks_per * BAND, LANES)
                    lse_ref[branch, rows, :] = lse[part].reshape(blocks_per * BAND, LANES)
            return carry

        for seq_start in (True, False):
            @pl.when(c == 0 if seq_start else c != 0)
            def _(group=group, seq_start=seq_start):
                lax.fori_loop(0, n_units // GROUP, functools.partial(group, seq_start=seq_start), 0)

    l0, l1, l2 = lse_ref[0], lse_ref[1], lse_ref[2]
    mx = jnp.maximum(jnp.maximum(l0, l1), l2)
    w0, w1, w2 = jnp.exp(l0 - mx), jnp.exp(l1 - mx), jnp.exp(l2 - mx)
    inv = 1.0 / (w0 + w1 + w2)
    o_ref[...] = (acc_ref[0] * (w0 * inv) + acc_ref[1] * (w1 * inv) + acc_ref[2] * (w2 * inv)).astype(o_ref.dtype)


def band_attention(qkv, bias_tbl, batch, seq):
    n_chunk = seq // CHUNK
    n_pair = A_WIDTH // LANES
    blk = (CHUNK, LANES)
    in_specs = []
    for branch in range(N_BRANCH):
        qcol, kcol, vcol = [(3 * branch + kind) * n_pair for kind in range(3)]

        def cur(col):
            return lambda b, c, hp, col=col: (b * n_chunk + c, col + hp)

        def prev(col):
            return lambda b, c, hp, col=col: (b * n_chunk + jnp.maximum(c - 1, 0), col + hp)

        in_specs += [pl.BlockSpec(blk, cur(qcol)), pl.BlockSpec(blk, prev(kcol)), pl.BlockSpec(blk, cur(kcol)),
                     pl.BlockSpec(blk, prev(vcol)), pl.BlockSpec(blk, cur(vcol))]
    in_specs.append(pl.BlockSpec((N_BRANCH, 2, BAND, 2 * BAND), lambda b, c, hp: (0, hp, 0, 0)))
    return pl.pallas_call(
        _band_attn_kernel,
        grid=(batch, n_chunk, n_pair),
        in_specs=in_specs,
        out_specs=pl.BlockSpec(blk, lambda b, c, hp: (b * n_chunk + c, hp)),
        out_shape=jax.ShapeDtypeStruct((batch * seq, A_WIDTH), BF16),
        scratch_shapes=[pltpu.VMEM((N_BRANCH, CHUNK, LANES), F32)] * 2
        + [pltpu.VMEM((N_BRANCH - 1, CHUNK, LANES), BF16)] * 2,
        compiler_params=_params("parallel", "parallel", "arbitrary"),
        name="band_attention",
    )(*([qkv] * 15), bias_tbl)


def _mix_out_kernel(a_ref, u_ref, halo_ref, x_ref, wpool_ref, scale_ref, woa_ref, wop_ref, o_ref, ext_ref,
                    *, tiles_per_seq):
    i = pl.program_id(0)
    tm = a_ref.shape[0]
    halo = halo_ref.shape[0]
    seq_tile = i % tiles_per_seq
    ext_ref[0:halo, :] = jnp.where(seq_tile == 0, 0.0, halo_ref[...])
    ext_ref[halo:halo + tm, :] = u_ref[...]
    pos = seq_tile * tm + lax.broadcasted_iota(jnp.int32, (tm, POOL_GROUP_DIM), 0)
    acc = x_ref[...] + jnp.dot(a_ref[...], woa_ref[...], preferred_element_type=F32)
    for grp, win in enumerate(POOL_WINDOWS):
        cols = slice(grp * POOL_GROUP_DIM, (grp + 1) * POOL_GROUP_DIM)
        tot = ext_ref[halo:halo + tm, cols]
        for back in range(1, win):
            tot = tot + ext_ref[halo - back:halo - back + tm, cols]
        cnt = jnp.minimum(pos + 1, win).astype(F32)
        mix = tot / cnt - ext_ref[halo:halo + tm, cols]
        y = jnp.dot(mix.astype(BF16), wpool_ref[grp], preferred_element_type=F32) * scale_ref[:, cols]
        acc = acc + jnp.dot(y.astype(BF16), wop_ref[cols, :], preferred_element_type=F32)
    o_ref[...] = acc


def mix_out(a, u, x, w_pool, pool_scale, w_out_a, w_out_p, tm, seq):
    m, d = x.shape
    halo = 16
    assert seq % tm == 0 and tm % halo == 0
    kern = functools.partial(_mix_out_kernel, tiles_per_seq=seq // tm)
    return pl.pallas_call(
        kern,
        grid=(m // tm,),
        in_specs=[
            pl.BlockSpec((tm, A_WIDTH), lambda i: (i, 0)),
            pl.BlockSpec((tm, POOL_WIDTH), lambda i: (i, 0)),
            pl.BlockSpec((halo, POOL_WIDTH), lambda i: (jnp.maximum(i * (tm // halo) - 1, 0), 0)),
            pl.BlockSpec((tm, d), lambda i: (i, 0)),
            pl.BlockSpec((len(POOL_WINDOWS), POOL_GROUP_DIM, POOL_GROUP_DIM), lambda i: (0, 0, 0)),
            pl.BlockSpec((1, POOL_WIDTH), lambda i: (0, 0)),
            pl.BlockSpec((A_WIDTH, d), lambda i: (0, 0)),
            pl.BlockSpec((POOL_WIDTH, d), lambda i: (0, 0)),
        ],
        out_specs=pl.BlockSpec((tm, d), lambda i: (i, 0)),
        out_shape=jax.ShapeDtypeStruct((m, d), F32),
        scratch_shapes=[pltpu.VMEM((tm + halo, POOL_WIDTH), F32)],
        compiler_params=_params("parallel"),
        name="mix_out",
    )(a, u, u, x, w_pool, pool_scale, w_out_a, w_out_p)


def _mix_out_sample_kernel(a_ref, mix_ref, x_ref, wpool_ref, scale_ref, woa_ref, wop_ref, o_ref):
    acc = x_ref[...] + jnp.dot(a_ref[...].astype(BF16), woa_ref[...], preferred_element_type=F32)
    for grp in range(len(POOL_WINDOWS)):
        cols = slice(grp * POOL_GROUP_DIM, (grp + 1) * POOL_GROUP_DIM)
        y = jnp.dot(mix_ref[:, cols].astype(BF16), wpool_ref[grp], preferred_element_type=F32) * scale_ref[:, cols]
        acc = acc + jnp.dot(y.astype(BF16), wop_ref[cols, :], preferred_element_type=F32)
    o_ref[...] = acc


def mix_out_sample(a, mix, x, w_pool, pool_scale, w_out_a, w_out_p):
    m, d = x.shape
    return pl.pallas_call(
        _mix_out_sample_kernel,
        out_shape=jax.ShapeDtypeStruct((m, d), F32),
        compiler_params=pltpu.CompilerParams(vmem_limit_bytes=VMEM_LIMIT_BYTES),
        name="mix_out_sample",
    )(a, mix, x, w_pool, pool_scale, w_out_a, w_out_p)


def _top2(logits):
    lane = lax.broadcasted_iota(jnp.int32, logits.shape, 1)
    lg = jnp.where(lane < N_EXPERTS, logits, -jnp.inf)
    m1 = jnp.max(lg, axis=-1, keepdims=True)
    i1 = jnp.min(jnp.where(lg == m1, lane, LANES), axis=-1, keepdims=True)
    rest = jnp.where(lane == i1, -jnp.inf, lg)
    m2 = jnp.max(rest, axis=-1, keepdims=True)
    i2 = jnp.min(jnp.where(rest == m2, lane, LANES), axis=-1, keepdims=True)
    e2 = jnp.exp(m2 - m1)
    return lane, i1, i2, 1.0 / (1.0 + e2), e2 / (1.0 + e2)


def _top2_gates(logits):
    lane, i1, i2, g1, g2 = _top2(logits)
    return jnp.where(lane == i1, g1, 0.0) + jnp.where(lane == i2, g2, 0.0)


ROUTE_E1, ROUTE_E2, ROUTE_R1, ROUTE_R2, ROUTE_G1, ROUTE_G2 = range(6)
EXPERT_TILE = 512


def _routing_record(hn, wr_ref, tri_ref):
    logits = jnp.dot(hn, wr_ref[...], preferred_element_type=F32)
    lane, i1, i2, g1, g2 = _top2(logits)
    chosen = jnp.where(jnp.logical_or(lane == i1, lane == i2), 1.0, 0.0).astype(BF16)
    before = jnp.dot(tri_ref[...], chosen, preferred_element_type=F32)
    r1 = jnp.sum(jnp.where(lane == i1, before, 0.0), axis=-1, keepdims=True)
    r2 = jnp.sum(jnp.where(lane == i2, before, 0.0), axis=-1, keepdims=True)
    rec = jnp.zeros(logits.shape, F32)
    for pos, val in ((ROUTE_E1, i1.astype(F32)), (ROUTE_E2, i2.astype(F32)), (ROUTE_R1, r1), (ROUTE_R2, r2),
                     (ROUTE_G1, g1), (ROUTE_G2, g2)):
        rec = jnp.where(lane == pos, val, rec)
    return rec


def _row_copy(src, src_row, dst, dst_row, sem):
    return pltpu.make_async_copy(src.at[pl.ds(src_row, 1)], dst.at[pl.ds(dst_row, 1)], sem)


def _scatter_rows_kernel(dest_ref, x_ref, o_hbm, sem):
    tm = x_ref.shape[0]

    def issue(r, carry):
        for k in range(2):
            _row_copy(x_ref, r, o_hbm, dest_ref[0, 0, 2 * r + k], sem).start(priority=k)
        return carry

    def drain(r, carry):
        for k in range(2):
            _row_copy(x_ref, 0, o_hbm, 0, sem).wait()
        return carry

    lax.fori_loop(0, tm, issue, 0, unroll=8)
    lax.fori_loop(0, tm, drain, 0, unroll=8)


def scatter_rows(x, dest, tm):
    m, d = x.shape
    dest3 = dest.reshape(m // tm, 1, 2 * tm)
    return pl.pallas_call(
        _scatter_rows_kernel,
        grid=(m // tm,),
        in_specs=[
            pl.BlockSpec((1, 1, 2 * tm), lambda i: (i, 0, 0), memory_space=pltpu.SMEM),
            pl.BlockSpec((tm, d), lambda i: (i, 0)),
        ],
        out_specs=pl.BlockSpec(memory_space=pl.ANY),
        out_shape=jax.ShapeDtypeStruct((2 * m, d), x.dtype),
        scratch_shapes=[pltpu.SemaphoreType.DMA(())],
        compiler_params=_params("arbitrary"),
        name="scatter_rows",
    )(dest3, x)


def _expert_kernel(tile_ref, exp_ref, lo_ref, hi_ref, nvis_ref, x_ref, g_ref, wg_ref, wu_ref, wd_ref, o_ref,
                   hn_ref, acc_ref):
    v = pl.program_id(0)
    j = pl.program_id(1)

    @pl.when(v < nvis_ref[0])
    def _():
        @pl.when(j == 0)
        def _():
            hn_ref[...] = _rmsnorm(x_ref[...], g_ref[...]).astype(BF16)
            acc_ref[...] = jnp.zeros_like(acc_ref)

        hn = hn_ref[...]
        a = jnp.dot(hn, wg_ref[...], preferred_element_type=F32)
        b = jnp.dot(hn, wu_ref[...], preferred_element_type=F32)
        h = (a * _sigmoid(a) * b).astype(BF16)
        acc_ref[...] += jnp.dot(h, wd_ref[...], preferred_element_type=F32)

        @pl.when(j == pl.num_programs(1) - 1)
        def _():
            row = lax.broadcasted_iota(jnp.int32, (o_ref.shape[0], 1), 0)
            mine = jnp.logical_and(row >= lo_ref[v], row < hi_ref[v])
            first = jnp.logical_or(v == 0, tile_ref[v] != tile_ref[jnp.maximum(v - 1, 0)])

            @pl.when(first)
            def _():
                o_ref[...] = jnp.where(mine, acc_ref[...], 0.0)

            @pl.when(jnp.logical_not(first))
            def _():
                o_ref[...] = jnp.where(mine, acc_ref[...], o_ref[...])


def expert_swiglu(xs, g, w_gate, w_up, w_down, visits, tm, tf):
    m, d = xs.shape
    f = w_gate.shape[2]
    n_f = f // tf
    assert m % tm == 0 and f % tf == 0
    tile, expert, lo, hi, n_visit = visits

    def f_blk(v, j):
        return j + (v % 2) * (n_f - 1 - 2 * j)

    grid_spec = pltpu.PrefetchScalarGridSpec(
        num_scalar_prefetch=5,
        grid=(tile.shape[0], n_f),
        in_specs=[
            pl.BlockSpec((tm, d), lambda v, j, t, e, lo, hi, n: (t[v], 0)),
            pl.BlockSpec((1, d), lambda v, j, t, e, lo, hi, n: (0, 0)),
            pl.BlockSpec((None, d, tf), lambda v, j, t, e, lo, hi, n: (e[v], 0, f_blk(v, j))),
            pl.BlockSpec((None, d, tf), lambda v, j, t, e, lo, hi, n: (e[v], 0, f_blk(v, j))),
            pl.BlockSpec((None, tf, d), lambda v, j, t, e, lo, hi, n: (e[v], f_blk(v, j), 0)),
        ],
        out_specs=pl.BlockSpec((tm, d), lambda v, j, t, e, lo, hi, n: (t[v], 0)),
        scratch_shapes=[pltpu.VMEM((tm, d), BF16), pltpu.VMEM((tm, d), F32)],
    )
    return pl.pallas_call(
        _expert_kernel,
        grid_spec=grid_spec,
        out_shape=jax.ShapeDtypeStruct((m, d), F32),
        compiler_params=_params("arbitrary", "arbitrary"),
        name="expert_swiglu",
    )(tile, expert, lo, hi, n_visit, xs, g, w_gate, w_up, w_down)


def _combine_kernel(dest_ref, next_ref, x_ref, rec_ref, gf_ref, ys_hbm, o_ref, y_ref, sem):
    i = pl.program_id(0)
    tm = x_ref.shape[0]

    def fetch(idx_ref, slot):
        def issue(r, carry):
            for k in range(2):
                _row_copy(ys_hbm, idx_ref[0, 0, 2 * r + k], y_ref.at[slot, k], r, sem.at[slot]).start(priority=k)
            return carry

        lax.fori_loop(0, tm, issue, 0, unroll=8)

    def finish(slot):
        def drain(r, carry):
            for k in range(2):
                _row_copy(ys_hbm, 0, y_ref.at[slot, k], 0, sem.at[slot]).wait()
            return carry

        lax.fori_loop(0, tm, drain, 0, unroll=8)
        g1 = rec_ref[:, ROUTE_G1:ROUTE_G1 + 1]
        g2 = rec_ref[:, ROUTE_G2:ROUTE_G2 + 1]
        out = x_ref[...] + (g1 * y_ref[slot, 0] + g2 * y_ref[slot, 1])
        o_ref[...] = _rmsnorm(out, gf_ref[...])

    @pl.when(i == 0)
    def _():
        fetch(dest_ref, 0)

    for slot in range(2):
        @pl.when(i % 2 == slot)
        def _(slot=slot):
            @pl.when(i + 1 < pl.num_programs(0))
            def _():
                fetch(next_ref, 1 - slot)

            finish(slot)


def combine_rows(x, rec, dest, ys, g_final, tm):
    m, d = x.shape
    n_tile = m // tm
    dest3 = dest.reshape(n_tile, 1, 2 * tm)
    return pl.pallas_call(
        _combine_kernel,
        grid=(n_tile,),
        in_specs=[
            pl.BlockSpec((1, 1, 2 * tm), lambda i: (i, 0, 0), memory_space=pltpu.SMEM),
            pl.BlockSpec((1, 1, 2 * tm), lambda i: (jnp.minimum(i + 1, n_tile - 1), 0, 0), memory_space=pltpu.SMEM),
            pl.BlockSpec((tm, d), lambda i: (i, 0)),
            pl.BlockSpec((tm, LANES), lambda i: (i, 0)),
            pl.BlockSpec((1, d), lambda i: (0, 0)),
            pl.BlockSpec(memory_space=pl.ANY),
        ],
        out_specs=pl.BlockSpec((tm, d), lambda i: (i, 0)),
        out_shape=jax.ShapeDtypeStruct((m, d), F32),
        scratch_shapes=[pltpu.VMEM((2, 2, tm, d), F32), pltpu.SemaphoreType.DMA((2,))],
        compiler_params=_params("arbitrary"),
        name="combine_rows",
    )(dest3, dest3, x, rec, g_final, ys)


def _visit_list(rec, tile_tokens, tm):
    n = rec.shape[0]
    e = rec[:, ROUTE_E1:ROUTE_E2 + 1].astype(jnp.int32)
    rank = rec[:, ROUTE_R1:ROUTE_R2 + 1].astype(jnp.int32)
    onehot = (e[:, :, None] == jnp.arange(N_EXPERTS)[None, None, :]).astype(jnp.int32)
    per_tile = onehot.reshape(n // tile_tokens, tile_tokens * 2, N_EXPERTS).sum(axis=1)
    count = per_tile.sum(axis=0)
    start = jnp.cumsum(count) - count
    tile_base = jnp.cumsum(per_tile, axis=0) - per_tile
    base = (start[None, :] + tile_base)[:, None, :]
    dest = rank + jnp.sum(onehot.reshape(n // tile_tokens, tile_tokens * 2, N_EXPERTS) * base, axis=-1).reshape(n, 2)

    n_tile = 2 * n // tm
    n_visit_max = n_tile + N_EXPERTS
    end = start + count
    first_tile = start // tm
    last_tile = jnp.maximum(end - 1, start) // tm
    n_vis = jnp.where(count > 0, last_tile - first_tile + 1, 0)
    vis_end = jnp.cumsum(n_vis)
    vis_start = vis_end - n_vis
    total = vis_end[-1]
    v = jnp.arange(n_visit_max)
    vc = jnp.minimum(v, total - 1)
    ex = jnp.sum(vc[:, None] >= vis_end[None, :], axis=1)
    tile = first_tile[ex] + vc - vis_start[ex]
    lo = jnp.clip(start[ex] - tile * tm, 0, tm)
    hi = jnp.clip(end[ex] - tile * tm, 0, tm)
    live = v < total
    as_i32 = lambda a: a.astype(jnp.int32)
    visits = (as_i32(tile), as_i32(ex), as_i32(jnp.where(live, lo, 0)), as_i32(jnp.where(live, hi, 0)),
              as_i32(total).reshape(1))
    return dest.astype(jnp.int32), visits


def routed_experts(x, rec, rank_tile, g, w_gate, w_up, w_down, g_final):
    dest, visits = _visit_list(rec, rank_tile, EXPERT_TILE)
    xs = scatter_rows(x, dest, 512)
    ys = expert_swiglu(xs, g, w_gate, w_up, w_down, visits, EXPERT_TILE, w_gate.shape[2] // 2)
    return combine_rows(x, rec, dest, ys, g_final, 256)


def _swiglu_kernel(*refs, n_expert, final_norm):
    x_ref, g_ref = refs[0], refs[1]
    k = 2
    if n_expert > 1:
        wr_ref = refs[k]
        k += 1
    wg_ref, wu_ref, wd_ref = refs[k:k + 3]
    k += 3
    if final_norm:
        gf_ref = refs[k]
        k += 1
    o_ref, hn_ref, acc_ref = refs[k:k + 3]
    if n_expert > 1:
        gate_ref = refs[k + 3]
    e = pl.program_id(1)
    j = pl.program_id(2)

    @pl.when(jnp.logical_and(e == 0, j == 0))
    def _():
        hn = _rmsnorm(x_ref[...], g_ref[...])
        hn_ref[...] = hn.astype(BF16)
        acc_ref[...] = jnp.zeros_like(acc_ref)
        if n_expert > 1:
            logits = jnp.dot(hn.astype(BF16), wr_ref[...], preferred_element_type=F32)
            gate_ref[...] = _top2_gates(logits)

    hn = hn_ref[...]
    a = jnp.dot(hn, wg_ref[...], preferred_element_type=F32)
    b = jnp.dot(hn, wu_ref[...], preferred_element_type=F32)
    h = (a * _sigmoid(a) * b).astype(BF16)
    y = jnp.dot(h, wd_ref[...], preferred_element_type=F32)
    if n_expert > 1:
        lane = lax.broadcasted_iota(jnp.int32, gate_ref.shape, 1)
        gate = jnp.sum(jnp.where(lane == e, gate_ref[...], 0.0), axis=-1, keepdims=True)
        y = gate * y
    acc_ref[...] += y

    @pl.when(jnp.logical_and(e == n_expert - 1, j == pl.num_programs(2) - 1))
    def _():
        out = x_ref[...] + acc_ref[...]
        if final_norm:
            out = _rmsnorm(out, gf_ref[...])
        o_ref[...] = out


def swiglu_block(x, g, w_gate, w_up, w_down, tm, tf, w_router=None, g_final=None):
    m, d = x.shape
    n_expert, _, f = w_gate.shape
    assert m % tm == 0 and f % tf == 0
    args = [x, g]
    in_specs = [pl.BlockSpec((tm, d), lambda i, e, j: (i, 0)), pl.BlockSpec((1, d), lambda i, e, j: (0, 0))]
    if n_expert > 1:
        args.append(w_router)
        in_specs.append(pl.BlockSpec((d, LANES), lambda i, e, j: (0, 0)))
    args += [w_gate, w_up, w_down]
    in_specs += [
        pl.BlockSpec((None, d, tf), lambda i, e, j: (e, 0, j)),
        pl.BlockSpec((None, d, tf), lambda i, e, j: (e, 0, j)),
        pl.BlockSpec((None, tf, d), lambda i, e, j: (e, j, 0)),
    ]
    if g_final is not None:
        args.append(g_final)
        in_specs.append(pl.BlockSpec((1, d), lambda i, e, j: (0, 0)))
    scratch = [pltpu.VMEM((tm, d), BF16), pltpu.VMEM((tm, d), F32)]
    if n_expert > 1:
        scratch.append(pltpu.VMEM((tm, LANES), F32))
    kern = functools.partial(_swiglu_kernel, n_expert=n_expert, final_norm=g_final is not None)
    return pl.pallas_call(
        kern,
        grid=(m // tm, n_expert, f // tf),
        in_specs=in_specs,
        out_specs=pl.BlockSpec((tm, d), lambda i, e, j: (i, 0)),
        out_shape=jax.ShapeDtypeStruct((m, d), F32),
        scratch_shapes=scratch,
        compiler_params=_params("parallel", "arbitrary", "arbitrary"),
        name="swiglu_block",
    )(*args)


def _ln_silu_pw2(z, lng_ref, lnb_ref, w2_ref, b2_ref):
    mu = jnp.mean(z, axis=-1, keepdims=True)
    zc = z - mu
    var = jnp.mean(zc * zc, axis=-1, keepdims=True)
    zn = zc * lax.rsqrt(var + LN_EPS) * lng_ref[...] + lnb_ref[...]
    act = (zn * _sigmoid(zn)).astype(BF16)
    return jnp.dot(act, w2_ref[...], preferred_element_type=F32) + b2_ref[...]


def _conv_kernel(glu_ref, halo_ref, x_ref, wdw_ref, bdw_ref, lng_ref, lnb_ref, w2_ref, b2_ref, gr_ref, wr_ref, tri_ref,
                 o_ref, rec_ref, ext_ref, z_ref, *, tiles_per_seq, row_chunk):
    i = pl.program_id(0)
    tm = glu_ref.shape[0]
    halo = halo_ref.shape[0]
    n_slab = glu_ref.shape[1] // LANES
    at_start = i % tiles_per_seq == 0
    for cb in range(n_slab):
        cols = slice(cb * LANES, (cb + 1) * LANES)
        ext_ref[cb, 0:halo, :] = jnp.where(at_start, 0.0, _round_bf16(halo_ref[:, cols]))
        ext_ref[cb, halo:halo + tm, :] = _round_bf16(glu_ref[:, cols])
    base = halo - (CONV_K - 1)

    def slab(cb, carry):
        for rc in range(tm // row_chunk):
            r0 = rc * row_chunk
            acc = jnp.zeros((row_chunk, LANES), F32) + bdw_ref[cb]
            for tap in range(CONV_K):
                acc = acc + ext_ref[cb, r0 + base + tap:r0 + base + tap + row_chunk, :] * wdw_ref[cb, tap:tap + 1, :]
            z_ref[cb, r0:r0 + row_chunk, :] = acc
        return carry

    lax.fori_loop(0, n_slab, slab, 0)
    z = jnp.concatenate([z_ref[cb] for cb in range(n_slab)], axis=-1)
    out = x_ref[...] + _ln_silu_pw2(z, lng_ref, lnb_ref, w2_ref, b2_ref)
    o_ref[...] = out
    rec_ref[...] = _routing_record(_rmsnorm(out, gr_ref[...]).astype(BF16), wr_ref, tri_ref)


def conv_block(glu, x, w_dw, b_dw, ln_g, ln_b, w_pw2, b_pw2, g_route, w_router, tm, seq):
    m, d = x.shape
    tri = (jnp.arange(tm)[:, None] > jnp.arange(tm)[None, :]).astype(BF16)
    halo = 32
    n_slab = d // LANES
    assert seq % tm == 0 and tm % halo == 0
    kern = functools.partial(_conv_kernel, tiles_per_seq=seq // tm, row_chunk=64)
    vec = pl.BlockSpec((1, d), lambda i: (0, 0))
    w_dw_slab = jnp.transpose(w_dw.reshape(w_dw.shape[0], n_slab, LANES), (1, 0, 2))
    b_dw_slab = b_dw.reshape(n_slab, 1, LANES)
    return pl.pallas_call(
        kern,
        grid=(m // tm,),
        in_specs=[
            pl.BlockSpec((tm, d), lambda i: (i, 0)),
            pl.BlockSpec((halo, d), lambda i: (jnp.maximum(i * (tm // halo) - 1, 0), 0)),
            pl.BlockSpec((tm, d), lambda i: (i, 0)),
            pl.BlockSpec(w_dw_slab.shape, lambda i: (0, 0, 0)),
            pl.BlockSpec(b_dw_slab.shape, lambda i: (0, 0, 0)),
            vec, vec,
            pl.BlockSpec((d, d), lambda i: (0, 0)),
            vec,
            vec,
            pl.BlockSpec((d, LANES), lambda i: (0, 0)),
            pl.BlockSpec((tm, tm), lambda i: (0, 0)),
        ],
        out_specs=[pl.BlockSpec((tm, d), lambda i: (i, 0)), pl.BlockSpec((tm, LANES), lambda i: (i, 0))],
        out_shape=[jax.ShapeDtypeStruct((m, d), F32), jax.ShapeDtypeStruct((m, LANES), F32)],
        scratch_shapes=[pltpu.VMEM((n_slab, tm + halo, LANES), F32), pltpu.VMEM((n_slab, tm, LANES), F32)],
        compiler_params=_params("parallel"),
        name="conv_block",
    )(glu, glu, x, w_dw_slab, b_dw_slab, ln_g, ln_b, w_pw2, b_pw2, g_route, w_router, tri)


def _conv_sample_kernel(ext_ref, x_ref, wdw_ref, bdw_ref, lng_ref, lnb_ref, w2_ref, b2_ref, o_ref, z_ref,
                        *, steps):
    n = ext_ref.shape[0]
    z_ref[...] = jnp.zeros_like(z_ref)

    def one(s, carry):
        acc = jnp.zeros((steps, z_ref.shape[2]), F32) + bdw_ref[...]
        for tap in range(CONV_K):
            acc = acc + _round_bf16(ext_ref[s, tap:tap + steps, :]) * wdw_ref[tap:tap + 1, :]
        z_ref[s, 0:steps, :] = acc
        return carry

    lax.fori_loop(0, n, one, 0)
    z = z_ref[...].reshape(n * z_ref.shape[1], z_ref.shape[2])
    y = _ln_silu_pw2(z, lng_ref, lnb_ref, w2_ref, b2_ref)
    o_ref[...] = x_ref[...] + y.reshape(o_ref.shape)


def conv_block_sample(ext, x, w_dw, b_dw, ln_g, ln_b, w_pw2, b_pw2, steps):
    n, _, d = ext.shape
    kern = functools.partial(_conv_sample_kernel, steps=steps)
    return pl.pallas_call(
        kern,
        out_shape=jax.ShapeDtypeStruct((n, 8, d), F32),
        scratch_shapes=[pltpu.VMEM((n, 8, d), F32)],
        compiler_params=pltpu.CompilerParams(vmem_limit_bytes=VMEM_LIMIT_BYTES),
        name="conv_block_sample",
    )(ext, x, w_dw, b_dw, ln_g, ln_b, w_pw2, b_pw2)


def _sample_attn_kernel(q_ref, kn_ref, vn_ref, kvt_ref, c0_ref, c1_ref, c2_ref, b0_ref, b1_ref, b2_ref, bn_ref,
                        ext_ref, a_ref, mix_ref, o0_ref, o1_ref, o2_ref, *, steps):
    caches = (c0_ref, c1_ref, c2_ref)
    biases = (b0_ref, b1_ref, b2_ref)
    new_caches = (o0_ref, o1_ref, o2_ref)
    n_head = q_ref.shape[2]
    lane = lax.broadcasted_iota(jnp.int32, (n_head, A_HEAD_DIM, LANES), 2)
    keep = LANES - steps

    outs, lses = [], []
    for branch in range(N_BRANCH):
        c_ref = caches[branch]
        q = (q_ref[0, branch] * (A_HEAD_DIM ** -0.5)).astype(BF16)
        kt = c_ref[0, 0].astype(BF16)
        vt = c_ref[0, 1].astype(BF16)
        kn = kn_ref[0, branch].astype(BF16)
        vn = vn_ref[0, branch].astype(BF16)
        s_c = jnp.einsum("hqd,hdl->hql", q, kt, preferred_element_type=F32) + biases[branch][...]
        s_n = jnp.einsum("hqd,hkd->hqk", q, kn, preferred_element_type=F32) + bn_ref[branch]
        mx = jnp.maximum(jnp.max(s_c, axis=-1, keepdims=True), jnp.max(s_n, axis=-1, keepdims=True))
        den = (jnp.sum(jnp.exp(s_c - mx), axis=-1, keepdims=True)
               + jnp.sum(jnp.exp(s_n - mx), axis=-1, keepdims=True))
        lse = mx + jnp.log(den)
        p_c = jnp.exp(s_c - lse).astype(BF16)
        p_n = jnp.exp(s_n - lse).astype(BF16)
        outs.append(jnp.einsum("hql,hdl->hqd", p_c, vt, preferred_element_type=F32)
                    + jnp.einsum("hqk,hkd->hqd", p_n, vn, preferred_element_type=F32))
        lses.append(lse)
    mx = jnp.maximum(jnp.maximum(lses[0], lses[1]), lses[2])
    wts = [jnp.exp(lse - mx) for lse in lses]
    inv = 1.0 / (wts[0] + wts[1] + wts[2])
    a_ref[0] = outs[0] * (wts[0] * inv) + outs[1] * (wts[1] * inv) + outs[2] * (wts[2] * inv)

    for branch in range(N_BRANCH):
        c_ref, o_ref = caches[branch], new_caches[branch]
        n_tile = c_ref.shape[-1] // LANES
        for kv in range(2):
            nxt = pltpu.roll(c_ref[0, kv, :, :, 0:LANES], keep, 2)
            for j in range(n_tile):
                cur = nxt
                if j + 1 < n_tile:
                    nxt = pltpu.roll(c_ref[0, kv, :, :, (j + 1) * LANES:(j + 2) * LANES], keep, 2)
                else:
                    nxt = kvt_ref[0, branch, kv]
                o_ref[0, kv, :, :, j * LANES:(j + 1) * LANES] = jnp.where(lane < keep, cur, nxt)

    base = ext_ref.shape[1] - steps
    for t in range(steps):
        for grp, win in enumerate(POOL_WINDOWS):
            cols = slice(grp * POOL_GROUP_DIM, (grp + 1) * POOL_GROUP_DIM)
            last = base + t
            tot = jnp.sum(ext_ref[0, last - win + 1:last + 1, cols], axis=0, keepdims=True)
            mix_ref[0, t:t + 1, cols] = tot / float(win) - ext_ref[0, last:last + 1, cols]


def sample_attention(q, kn, vn, kvt, caches, biases, bias_new, ext_pool, steps):
    n, rows = q.shape[0], q.shape[3]
    hg = A_HEADS // 2
    per_head = (hg, rows, A_HEAD_DIM)
    in_specs = [pl.BlockSpec((1, N_BRANCH) + per_head, lambda i, j: (i, 0, j, 0, 0))] * 3
    in_specs.append(pl.BlockSpec((1, N_BRANCH, 2, hg, A_HEAD_DIM, LANES), lambda i, j: (i, 0, 0, j, 0, 0)))
    cache_specs = [pl.BlockSpec((1, 2, hg, A_HEAD_DIM, c.shape[-1]), lambda i, j: (i, 0, j, 0, 0)) for c in caches]
    in_specs += cache_specs
    in_specs += [pl.BlockSpec((hg, rows, b.shape[-1]), lambda i, j: (j, 0, 0)) for b in biases]
    in_specs += [
        pl.BlockSpec((N_BRANCH, hg, rows, rows), lambda i, j: (0, j, 0, 0)),
        pl.BlockSpec((1,) + ext_pool.shape[1:], lambda i, j: (i, 0, 0)),
    ]
    kern = functools.partial(_sample_attn_kernel, steps=steps)
    return pl.pallas_call(
        kern,
        grid=(n, A_HEADS // hg),
        in_specs=in_specs,
        out_specs=[
            pl.BlockSpec((1,) + per_head, lambda i, j: (i, j, 0, 0)),
            pl.BlockSpec((1, steps, POOL_WIDTH), lambda i, j: (i, 0, 0)),
        ] + cache_specs,
        out_shape=[
            jax.ShapeDtypeStruct((n, A_HEADS, rows, A_HEAD_DIM), F32),
            jax.ShapeDtypeStruct((n, steps, POOL_WIDTH), F32),
        ] + [jax.ShapeDtypeStruct(c.shape, F32) for c in caches],
        compiler_params=_params("parallel", "arbitrary"),
        name="sample_attention",
    )(q, kn, vn, kvt, *caches, *biases, bias_new, ext_pool)


def _t5_bucket(dist):
    max_exact = NUM_BUCKETS // 2
    d = jnp.maximum(dist, 1).astype(F32)
    large = max_exact + (jnp.log(d / max_exact) / math.log(REL_MAX_DIST / max_exact)
                         * (NUM_BUCKETS - max_exact)).astype(jnp.int32)
    return jnp.where(dist < max_exact, dist, jnp.minimum(large, NUM_BUCKETS - 1))


def _slot_bias(rel_bias):
    rb = rel_bias.astype(F32).reshape(NUM_BUCKETS, N_BRANCH, A_HEADS)
    slot = jnp.arange(BAND + 1)
    out = []
    for g, (_, dil) in enumerate(A_GROUPS):
        pick = jax.nn.one_hot(_t5_bucket(slot * dil), NUM_BUCKETS, dtype=F32)
        out.append(jnp.einsum("sb,bh->hs", pick, rb[:, g], precision=lax.Precision.HIGHEST))
    return jnp.stack(out)


def _sliding_rows(v, n_rows, width):
    period = v.shape[-1]
    assert period >= n_rows + width - 1
    flat = jnp.tile(v, (1,) * (v.ndim - 1) + (n_rows + 1,))[..., :(period + 1) * n_rows]
    return flat.reshape(v.shape[:-1] + (n_rows, period + 1))[..., :width]


def _prompt_bias_tables(rel_bias):
    sb = _slot_bias(rel_bias)
    lead = sb.shape[:2]
    diag = jnp.concatenate([jnp.full(lead + (BAND - 1,), NEG, F32), sb[..., ::-1],
                            jnp.full(lead + (BAND,), NEG, F32)], axis=-1)
    return _sliding_rows(diag, BAND, 2 * BAND)[:, :, ::-1]


def _sample_bias_tables(rel_bias, steps, rows):
    sb = _slot_bias(rel_bias)
    cache_tables = []
    for g, (window, dil) in enumerate(A_GROUPS):
        spread = jnp.concatenate([sb[g][..., None], jnp.full(sb[g].shape + (dil - 1,), NEG, F32)], axis=-1)
        by_dist = spread.reshape(A_HEADS, (BAND + 1) * dil)
        need = window + rows
        by_dist = jnp.pad(by_dist, ((0, 0), (0, max(need - by_dist.shape[1], 0))), constant_values=NEG)[:, :need]
        by_dist = jnp.where(jnp.arange(need)[None, :] <= window, by_dist, NEG)
        cache_tables.append(_sliding_rows(by_dist[:, ::-1], rows, window)[:, ::-1])
    t_q = jnp.arange(rows)[:, None]
    t_k = jnp.arange(rows)[None, :]
    new_tables = []
    for g, (_, dil) in enumerate(A_GROUPS):
        diff = t_q - t_k
        ok = (diff >= 0) & (diff % dil == 0) & (t_k < steps)
        pick = jax.nn.one_hot(jnp.where(ok, diff // dil, BAND + 1).reshape(-1), BAND + 1, dtype=F32)
        vals = jnp.einsum("ps,hs->hp", pick, sb[g], precision=lax.Precision.HIGHEST).reshape(A_HEADS, rows, rows)
        new_tables.append(jnp.where(ok[None], vals, NEG))
    return cache_tables, jnp.stack(new_tables)


def kernel(x_prompt, x_sample, cache_kv_w128, cache_kv_w512, cache_kv_w2048, state_pool, state_conv, norm_mix0, w_in0, rel_bias, w_pool, pool_scale, w_out0, norm_ffn0, w_ff_gate, w_ff_up, w_ff_down, norm_mix1, w_pw1, b_pw1, w_dw, b_dw, ln_g, ln_b, w_pw2, b_pw2, norm_ffn1, w_router, we_gate, we_up, we_down, norm_final):
    batch, seq, d = x_prompt.shape
    n_s, steps, _ = x_sample.shape
    caches_in = (cache_kv_w128, cache_kv_w512, cache_kv_w2048)
    assert norm_mix0.shape[0] == 1 and norm_mix1.shape[0] == 1, "two layers: one mixer of each kind"
    assert seq % CHUNK == 0 and steps <= min(dil for _, dil in A_GROUPS[1:])
    for c, (window, _) in zip(caches_in, A_GROUPS):
        assert c.shape[2] == window, "cache must hold a full window"
    qkv_w = N_BRANCH * 3 * A_WIDTH

    w_in = w_in0[0]
    w_in_bf = w_in.astype(BF16)
    w_kv = w_in_bf[:, A_WIDTH * N_BRANCH:qkv_w]
    g_mix0, g_ffn0 = norm_mix0, norm_ffn0
    g_mix1, g_ffn1 = norm_mix1, norm_ffn1
    g_final = norm_final.reshape(1, d)
    wpool_bf = w_pool[0].astype(BF16)
    w_out_a = w_out0[0, :A_WIDTH].astype(BF16)
    w_out_p = w_out0[0, A_WIDTH:].astype(BF16)
    wff = (w_ff_gate.astype(BF16), w_ff_up.astype(BF16), w_ff_down.astype(BF16))
    w_pw1_bf = w_pw1[0].astype(BF16)
    w_pw2_bf = w_pw2[0].astype(BF16)
    w_dw_pad = jnp.pad(w_dw[0], ((0, 32 - CONV_K), (0, 0)))
    wex = (we_gate[0].astype(BF16), we_up[0].astype(BF16), we_down[0].astype(BF16))
    w_router_pad = jnp.pad(w_router[0], ((0, 0), (0, LANES - N_EXPERTS))).astype(BF16)
    zeros = lambda n: jnp.zeros((1, n), F32)

    xp = x_prompt.reshape(batch * seq, d)
    qkv, u_p = qkv_perm(xp, g_mix0, w_in_bf)
    a_p = band_attention(qkv, _prompt_bias_tables(rel_bias), batch, seq)
    h_p = mix_out(a_p, u_p, xp, wpool_bf, pool_scale, w_out_a, w_out_p, 512, seq)
    h_p = swiglu_block(h_p, g_ffn0, *wff, tm=512, tf=1408)
    glu_p = rms_glu(h_p, g_mix1, w_pw1_bf, b_pw1, 1024, 512)
    conv_tile = 512
    h_p, rec_p = conv_block(glu_p, h_p, w_dw_pad, b_dw, ln_g, ln_b, w_pw2_bf, b_pw2, g_ffn1, w_router_pad,
                            conv_tile, seq)
    y_p = routed_experts(h_p, rec_p, conv_tile, g_ffn1, *wex, g_final)

    keep = A_GROUPS[-1][0]
    kv_tail = rms_matmul_t(x_prompt, g_mix0, w_kv.T, seq - keep, keep, 1024, 512)
    kv_tail = kv_tail.reshape(batch, 2, N_BRANCH, A_HEADS, A_HEAD_DIM, keep)
    kv_p = [jnp.transpose(kv_tail[:, :, g, :, :, keep - window:], (0, 4, 1, 2, 3))[None]
            for g, (window, _) in enumerate(A_GROUPS)]
    u_p3 = u_p.reshape(batch, seq, POOL_WIDTH)
    pool_p = u_p3[:, seq - POOL_BUF:][None]
    conv_p = glu_p.reshape(batch, seq, d)[:, seq - (CONV_K - 1):][None]

    m_s = n_s * steps
    xs = x_sample.reshape(m_s, d)
    proj = rms_matmul(xs, g_mix0, w_in_bf, zeros(w_in_bf.shape[1]), m_s, 512, F32)
    qkv_s = proj[:, :qkv_w].reshape(n_s, steps, 3, N_BRANCH, A_HEADS, A_HEAD_DIM)
    u_s = proj[:, qkv_w:].reshape(n_s, steps, POOL_WIDTH)
    rows = 8
    per_head = jnp.transpose(qkv_s, (2, 0, 3, 4, 1, 5))
    per_head = jnp.pad(per_head, ((0, 0),) * 4 + ((0, rows - steps), (0, 0)))
    new_cols = jnp.transpose(qkv_s[:, :, 1:], (0, 3, 2, 4, 5, 1))
    new_cols = jnp.pad(new_cols, ((0, 0),) * 5 + ((LANES - steps, 0),))
    caches_t = [jnp.transpose(c[0], (0, 2, 3, 4, 1)) for c in caches_in]
    ext_pool = jnp.concatenate([state_pool[0], u_s], axis=1)
    cache_bias, new_bias = _sample_bias_tables(rel_bias, steps, rows)
    a_s, mix_s, *caches_out = sample_attention(per_head[0], per_head[1], per_head[2], new_cols, caches_t,
                                               cache_bias, new_bias, ext_pool, steps)
    a_s = jnp.transpose(a_s[:, :, :steps], (0, 2, 1, 3)).reshape(m_s, A_WIDTH)
    h_s = mix_out_sample(a_s, mix_s.reshape(m_s, POOL_WIDTH), xs, wpool_bf, pool_scale, w_out_a, w_out_p)
    h_s = swiglu_block(h_s, g_ffn0, *wff, tm=m_s, tf=1408)
    glu_s = rms_glu(h_s, g_mix1, w_pw1_bf, b_pw1, m_s, 512)
    ext_conv = jnp.concatenate([state_conv[0], glu_s.reshape(n_s, steps, d)], axis=1)
    ext_conv_pad = jnp.pad(ext_conv, ((0, 0), (0, 40 - ext_conv.shape[1]), (0, 0)))
    h_s3 = jnp.pad(h_s.reshape(n_s, steps, d), ((0, 0), (0, 8 - steps), (0, 0)))
    h_s = conv_block_sample(ext_conv_pad, h_s3, w_dw_pad, b_dw, ln_g, ln_b, w_pw2_bf, b_pw2, steps)
    h_s = h_s[:, :steps].reshape(m_s, d)
    y_s = swiglu_block(h_s, g_ffn1, *wex, tm=m_s, tf=896, w_router=w_router_pad, g_final=g_final)

    kv_s = [jnp.transpose(c, (0, 4, 1, 2, 3))[None] for c in caches_out]
    pool_s = ext_pool[:, steps:][None]
    conv_s = ext_conv[:, steps:][None]

    return (y_p.reshape(batch, seq, d), y_s.reshape(n_s, steps, d), kv_p[0], kv_p[1], kv_p[2], pool_p, conv_p,
            kv_s[0], kv_s[1], kv_s[2], pool_s, conv_s)
```

```python
import functools
import math

import jax
import jax.numpy as jnp
from jax import lax
from jax.experimental import pallas as pl
from jax.experimental.pallas import tpu as pltpu

F32 = jnp.float32
BF16 = jnp.bfloat16

A_GROUPS = ((128, 1), (512, 4), (2048, 16))
N_BRANCH = 3
A_HEADS = 8
A_HEAD_DIM = 64
A_WIDTH = 512
BAND = 128
NUM_BUCKETS = 32
REL_MAX_DIST = 2048
POOL_WINDOWS = (2, 4, 8, 16)
POOL_GROUP_DIM = 128
POOL_WIDTH = 512
POOL_BUF = 15
CONV_K = 31
N_EXPERTS = 8
RMS_EPS = 1e-6
LN_EPS = 1e-5
NEG = -1e30

VMEM_LIMIT_BYTES = 56 * 1024 * 1024
LANES = 128
CHUNK = 2048
GROUP = 8


def _params(*sem):
    return pltpu.CompilerParams(dimension_semantics=sem, vmem_limit_bytes=VMEM_LIMIT_BYTES)


def _rmsnorm(x, g):
    return x * lax.rsqrt(jnp.mean(x * x, axis=-1, keepdims=True) + RMS_EPS) * g


def _sigmoid(x):
    return 1.0 / (1.0 + jnp.exp(-x))


def _round_bf16(x):
    return x.astype(BF16).astype(F32)


def _rms_matmul_kernel(x_ref, g_ref, w_ref, b_ref, o_ref, hn_ref):
    @pl.when(pl.program_id(1) == 0)
    def _():
        hn_ref[...] = _rmsnorm(x_ref[...], g_ref[...]).astype(BF16)

    acc = jnp.dot(hn_ref[...], w_ref[...], preferred_element_type=F32)
    o_ref[...] = (acc + b_ref[...]).astype(o_ref.dtype)


def rms_matmul(x, g, w, b, tm, tn, out_dtype):
    m, k = x.shape
    n = w.shape[1]
    assert m % tm == 0 and n % tn == 0
    return pl.pallas_call(
        _rms_matmul_kernel,
        grid=(m // tm, n // tn),
        in_specs=[
            pl.BlockSpec((tm, k), lambda i, j: (i, 0)),
            pl.BlockSpec((1, k), lambda i, j: (0, 0)),
            pl.BlockSpec((k, tn), lambda i, j: (0, j)),
            pl.BlockSpec((1, tn), lambda i, j: (0, j)),
        ],
        out_specs=pl.BlockSpec((tm, tn), lambda i, j: (i, j)),
        out_shape=jax.ShapeDtypeStruct((m, n), out_dtype),
        scratch_shapes=[pltpu.VMEM((tm, k), BF16)],
        compiler_params=_params("parallel", "arbitrary"),
        name="rms_matmul",
    )(x, g, w, b)


def _rms_matmul_t_kernel(x_ref, g_ref, wt_ref, o_ref, hn_ref):
    @pl.when(pl.program_id(2) == 0)
    def _():
        hn_ref[...] = _rmsnorm(x_ref[0], g_ref[...]).astype(BF16)

    nt = (((1,), (1,)), ((), ()))
    o_ref[0] = lax.dot_general(wt_ref[...], hn_ref[...], nt, preferred_element_type=F32)


def rms_matmul_t(x, g, wt, row0, rows, tm, tn):
    batch, _, k = x.shape
    n = wt.shape[0]
    assert row0 % tm == 0 and rows % tm == 0 and n % tn == 0
    return pl.pallas_call(
        _rms_matmul_t_kernel,
        grid=(batch, rows // tm, n // tn),
        in_specs=[
            pl.BlockSpec((1, tm, k), lambda b, i, j: (b, row0 // tm + i, 0)),
            pl.BlockSpec((1, k), lambda b, i, j: (0, 0)),
            pl.BlockSpec((tn, k), lambda b, i, j: (j, 0)),
        ],
        out_specs=pl.BlockSpec((1, tn, tm), lambda b, i, j: (b, j, i)),
        out_shape=jax.ShapeDtypeStruct((batch, n, rows), F32),
        scratch_shapes=[pltpu.VMEM((tm, k), BF16)],
        compiler_params=_params("parallel", "parallel", "arbitrary"),
        name="rms_matmul_t",
    )(x, g, wt)


def _rms_glu_kernel(x_ref, g_ref, wv_ref, wg_ref, bv_ref, bg_ref, o_ref, hn_ref):
    @pl.when(pl.program_id(1) == 0)
    def _():
        hn_ref[...] = _rmsnorm(x_ref[...], g_ref[...]).astype(BF16)

    hn = hn_ref[...]
    val = jnp.dot(hn, wv_ref[...], preferred_element_type=F32) + bv_ref[...]
    gate = jnp.dot(hn, wg_ref[...], preferred_element_type=F32) + bg_ref[...]
    o_ref[...] = val * _sigmoid(gate)


def rms_glu(x, g, w, b, tm, tn):
    m, k = x.shape
    c = w.shape[1] // 2
    nb = c // tn
    return pl.pallas_call(
        _rms_glu_kernel,
        grid=(m // tm, nb),
        in_specs=[
            pl.BlockSpec((tm, k), lambda i, j: (i, 0)),
            pl.BlockSpec((1, k), lambda i, j: (0, 0)),
            pl.BlockSpec((k, tn), lambda i, j: (0, j)),
            pl.BlockSpec((k, tn), lambda i, j: (0, j + nb)),
            pl.BlockSpec((1, tn), lambda i, j: (0, j)),
            pl.BlockSpec((1, tn), lambda i, j: (0, j + nb)),
        ],
        out_specs=pl.BlockSpec((tm, tn), lambda i, j: (i, j)),
        out_shape=jax.ShapeDtypeStruct((m, c), F32),
        scratch_shapes=[pltpu.VMEM((tm, k), BF16)],
        compiler_params=_params("parallel", "arbitrary"),
        name="rms_glu",
    )(x, g, w, w, b, b)


PERM_BLOCK = 256


def _qkv_perm_kernel(x_ref, g_ref, w_ref, perm_ref, o_ref, u_ref, hn_ref):
    j = pl.program_id(1)
    n_qkv = 3 * N_BRANCH

    @pl.when(j == 0)
    def _():
        hn_ref[...] = _rmsnorm(x_ref[...], g_ref[...]).astype(BF16)

    def project():
        return jnp.dot(hn_ref[...], w_ref[...], preferred_element_type=F32)

    @pl.when(j == n_qkv)
    def _():
        u_ref[...] = project()

    @pl.when(j < 3)
    def _():
        o_ref[...] = project().astype(BF16)

    for branch in (1, 2):
        dil = A_GROUPS[branch][1]
        rows = CHUNK // dil
        per = PERM_BLOCK // dil

        @pl.when(jnp.logical_and(j // 3 == branch, j < n_qkv))
        def _():
            res_bf = project().astype(BF16)
            for b in range(CHUNK // PERM_BLOCK):
                blk = jnp.dot(perm_ref[branch - 1], res_bf[b * PERM_BLOCK:(b + 1) * PERM_BLOCK],
                              preferred_element_type=F32).astype(BF16)
                for r in range(dil):
                    o_ref[r * rows + b * per:r * rows + (b + 1) * per, :] = blk[r * per:(r + 1) * per]


def qkv_perm(x, g, w_in):
    m, k = x.shape
    tn = A_WIDTH
    n_qkv = 3 * N_BRANCH
    assert w_in.shape[1] == (n_qkv + 1) * tn
    perms = []
    for _, dil in A_GROUPS[1:]:
        p = jnp.arange(PERM_BLOCK)
        src = (p % (PERM_BLOCK // dil)) * dil + p // (PERM_BLOCK // dil)
        perms.append((src[:, None] == jnp.arange(PERM_BLOCK)[None, :]).astype(BF16))

    def w_block(i, j):
        return 0, jnp.where(j < n_qkv, (j % 3) * N_BRANCH + j // 3, n_qkv)

    return pl.pallas_call(
        _qkv_perm_kernel,
        grid=(m // CHUNK, n_qkv + 1),
        in_specs=[
            pl.BlockSpec((CHUNK, k), lambda i, j: (i, 0)),
            pl.BlockSpec((1, k), lambda i, j: (0, 0)),
            pl.BlockSpec((k, tn), w_block),
            pl.BlockSpec((N_BRANCH - 1, PERM_BLOCK, PERM_BLOCK), lambda i, j: (0, 0, 0)),
        ],
        out_specs=[
            pl.BlockSpec((CHUNK, tn), lambda i, j: (i, jnp.minimum(j, n_qkv - 1))),
            pl.BlockSpec((CHUNK, tn), lambda i, j: (i, 0)),
        ],
        out_shape=[jax.ShapeDtypeStruct((m, n_qkv * tn), BF16), jax.ShapeDtypeStruct((m, tn), F32)],
        scratch_shapes=[pltpu.VMEM((CHUNK, k), BF16)],
        compiler_params=_params("parallel", "arbitrary"),
        name="qkv_perm",
    )(x, g, w_in, jnp.stack(perms))


def _band_attn_kernel(*refs):
    qkv_refs = refs[:15]
    bias_ref, o_ref, acc_ref, lse_ref, kp_ref, vp_ref = refs[15:]
    c = pl.program_id(1)
    n_units = CHUNK // BAND
    grp_rows = GROUP * BAND
    head_of_lane = lax.broadcasted_iota(jnp.int32, (1, 1, LANES), 2) // A_HEAD_DIM
    key_is_prev = lax.broadcasted_iota(jnp.int32, (1, 1, 2 * BAND), 2) < BAND

    for branch, (_, dil) in enumerate(A_GROUPS):
        q_ref, kprev_ref, kcur_ref, vprev_ref, vcur_ref = qkv_refs[5 * branch:5 * branch + 5]
        nblk = n_units // dil
        span = nblk * BAND
        if nblk == 1:
            kp_src, vp_src = kprev_ref, vprev_ref
        else:
            kp_src, vp_src = kp_ref.at[branch], vp_ref.at[branch]
            for res in range(dil):
                lo = res * span
                for dst, prev, cur in ((kp_src, kprev_ref, kcur_ref), (vp_src, vprev_ref, vcur_ref)):
                    dst[lo:lo + BAND, :] = prev[lo + span - BAND:lo + span, :]
                    dst[lo + BAND:lo + span, :] = cur[lo:lo + span - BAND, :]

        def group(gi, carry, seq_start, q_ref=q_ref, kcur_ref=kcur_ref, vcur_ref=vcur_ref, kp_src=kp_src,
                  vp_src=vp_src, nblk=nblk, dil=dil, branch=branch):
            row = pl.multiple_of(gi * grp_rows, grp_rows)

            def blocks(ref):
                return ref[pl.ds(row, grp_rows), :].reshape(GROUP, BAND, LANES)

            q = blocks(q_ref) * jnp.asarray(A_HEAD_DIM ** -0.5, BF16)
            kcat = jnp.concatenate([blocks(kp_src), blocks(kcur_ref)], axis=1)
            vcat = jnp.concatenate([blocks(vp_src), blocks(vcur_ref)], axis=1)
            if seq_start:
                unit = gi * GROUP + lax.broadcasted_iota(jnp.int32, (GROUP, 1, 1), 0)
                pen = jnp.where(jnp.logical_and(unit % nblk == 0, key_is_prev), NEG, 0.0)

            o = jnp.zeros((GROUP, BAND, LANES), F32)
            lse = jnp.zeros((GROUP, BAND, LANES), F32)
            for hh in range(2):
                mine = head_of_lane == hh
                kh = jnp.where(mine, kcat, jnp.zeros_like(kcat))
                vh = jnp.where(mine, vcat, jnp.zeros_like(vcat))
                s = jnp.einsum("uqd,ukd->uqk", q, kh, preferred_element_type=F32) + bias_ref[branch, hh][None]
                if seq_start:
                    s = s + pen
                mx = jnp.max(s, axis=-1, keepdims=True)
                p = jnp.exp(s - mx)
                den = jnp.sum(p, axis=-1, keepdims=True)
                pn = (p * (1.0 / den)).astype(BF16)
                o = o + jnp.einsum("uqk,ukd->uqd", pn, vh, preferred_element_type=F32)
                lse = jnp.where(mine, mx + jnp.log(den), lse)

            if dil == 1:
                acc_ref[branch, pl.ds(row, grp_rows), :] = o.reshape(grp_rows, LANES)
                lse_ref[branch, pl.ds(row, grp_rows), :] = lse.reshape(grp_rows, LANES)
            else:
                per = max(GROUP // nblk, 1)
                blocks_per = GROUP // per
                for k in range(per):
                    if nblk >= GROUP:
                        start = gi // (nblk // GROUP) + dil * BAND * ((gi % (nblk // GROUP)) * GROUP)
                    else:
                        start = gi * per + k
                    rows = pl.ds(start, blocks_per * BAND, stride=dil)
                    part = slice(k * blocks_per, (k + 1) * blocks_per)
                    acc_ref[branch, rows, :] = o[part].reshape(blocks_per * BAND, LANES)
                    lse_ref[branch, rows, :] = lse[part].reshape(blocks_per * BAND, LANES)
            return carry

        for seq_start in (True, False):
            @pl.when(c == 0 if seq_start else c != 0)
            def _(group=group, seq_start=seq_start):
                lax.fori_loop(0, n_units // GROUP, functools.partial(group, seq_start=seq_start), 0)

    l0, l1, l2 = lse_ref[0], lse_ref[1], lse_ref[2]
    mx = jnp.maximum(jnp.maximum(l0, l1), l2)
    w0, w1, w2 = jnp.exp(l0 - mx), jnp.exp(l1 - mx), jnp.exp(l2 - mx)
    inv = 1.0 / (w0 + w1 + w2)
    o_ref[...] = (acc_ref[0] * (w0 * inv) + acc_ref[1] * (w1 * inv) + acc_ref[2] * (w2 * inv)).astype(o_ref.dtype)


def band_attention(qkv, bias_tbl, batch, seq):
    n_chunk = seq // CHUNK
    n_pair = A_WIDTH // LANES
    blk = (CHUNK, LANES)
    in_specs = []
    for branch in range(N_BRANCH):
        qcol, kcol, vcol = [(3 * branch + kind) * n_pair for kind in range(3)]

        def cur(col):
            return lambda b, c, hp, col=col: (b * n_chunk + c, col + hp)

        def prev(col):
            return lambda b, c, hp, col=col: (b * n_chunk + jnp.maximum(c - 1, 0), col + hp)

        in_specs += [pl.BlockSpec(blk, cur(qcol)), pl.BlockSpec(blk, prev(kcol)), pl.BlockSpec(blk, cur(kcol)),
                     pl.BlockSpec(blk, prev(vcol)), pl.BlockSpec(blk, cur(vcol))]
    in_specs.append(pl.BlockSpec((N_BRANCH, 2, BAND, 2 * BAND), lambda b, c, hp: (0, hp, 0, 0)))
    return pl.pallas_call(
        _band_attn_kernel,
        grid=(batch, n_chunk, n_pair),
        in_specs=in_specs,
        out_specs=pl.BlockSpec(blk, lambda b, c, hp: (b * n_chunk + c, hp)),
        out_shape=jax.ShapeDtypeStruct((batch * seq, A_WIDTH), BF16),
        scratch_shapes=[pltpu.VMEM((N_BRANCH, CHUNK, LANES), F32)] * 2
        + [pltpu.VMEM((N_BRANCH - 1, CHUNK, LANES), BF16)] * 2,
        compiler_params=_params("parallel", "parallel", "arbitrary"),
        name="band_attention",
    )(*([qkv] * 15), bias_tbl)


def _mix_out_kernel(a_ref, u_ref, halo_ref, x_ref, wpool_ref, scale_ref, woa_ref, wop_ref, o_ref, ext_ref,
                    *, tiles_per_seq):
    i = pl.program_id(0)
    tm = a_ref.shape[0]
    halo = halo_ref.shape[0]
    seq_tile = i % tiles_per_seq
    ext_ref[0:halo, :] = jnp.where(seq_tile == 0, 0.0, halo_ref[...])
    ext_ref[halo:halo + tm, :] = u_ref[...]
    pos = seq_tile * tm + lax.broadcasted_iota(jnp.int32, (tm, POOL_GROUP_DIM), 0)
    acc = x_ref[...] + jnp.dot(a_ref[...], woa_ref[...], preferred_element_type=F32)
    for grp, win in enumerate(POOL_WINDOWS):
        cols = slice(grp * POOL_GROUP_DIM, (grp + 1) * POOL_GROUP_DIM)
        tot = ext_ref[halo:halo + tm, cols]
        for back in range(1, win):
            tot = tot + ext_ref[halo - back:halo - back + tm, cols]
        cnt = jnp.minimum(pos + 1, win).astype(F32)
        mix = tot / cnt - ext_ref[halo:halo + tm, cols]
        y = jnp.dot(mix.astype(BF16), wpool_ref[grp], preferred_element_type=F32) * scale_ref[:, cols]
        acc = acc + jnp.dot(y.astype(BF16), wop_ref[cols, :], preferred_element_type=F32)
    o_ref[...] = acc


def mix_out(a, u, x, w_pool, pool_scale, w_out_a, w_out_p, tm, seq):
    m, d = x.shape
    halo = 16
    assert seq % tm == 0 and tm % halo == 0
    kern = functools.partial(_mix_out_kernel, tiles_per_seq=seq // tm)
    return pl.pallas_call(
        kern,
        grid=(m // tm,),
        in_specs=[
            pl.BlockSpec((tm, A_WIDTH), lambda i: (i, 0)),
            pl.BlockSpec((tm, POOL_WIDTH), lambda i: (i, 0)),
            pl.BlockSpec((halo, POOL_WIDTH), lambda i: (jnp.maximum(i * (tm // halo) - 1, 0), 0)),
            pl.BlockSpec((tm, d), lambda i: (i, 0)),
            pl.BlockSpec((len(POOL_WINDOWS), POOL_GROUP_DIM, POOL_GROUP_DIM), lambda i: (0, 0, 0)),
            pl.BlockSpec((1, POOL_WIDTH), lambda i: (0, 0)),
            pl.BlockSpec((A_WIDTH, d), lambda i: (0, 0)),
            pl.BlockSpec((POOL_WIDTH, d), lambda i: (0, 0)),
        ],
        out_specs=pl.BlockSpec((tm, d), lambda i: (i, 0)),
        out_shape=jax.ShapeDtypeStruct((m, d), F32),
        scratch_shapes=[pltpu.VMEM((tm + halo, POOL_WIDTH), F32)],
        compiler_params=_params("parallel"),
        name="mix_out",
    )(a, u, u, x, w_pool, pool_scale, w_out_a, w_out_p)


def _mix_out_sample_kernel(a_ref, mix_ref, x_ref, wpool_ref, scale_ref, woa_ref, wop_ref, o_ref):
    acc = x_ref[...] + jnp.dot(a_ref[...].astype(BF16), woa_ref[...], preferred_element_type=F32)
    for grp in range(len(POOL_WINDOWS)):
        cols = slice(grp * POOL_GROUP_DIM, (grp + 1) * POOL_GROUP_DIM)
        y = jnp.dot(mix_ref[:, cols].astype(BF16), wpool_ref[grp], preferred_element_type=F32) * scale_ref[:, cols]
        acc = acc + jnp.dot(y.astype(BF16), wop_ref[cols, :], preferred_element_type=F32)
    o_ref[...] = acc


def mix_out_sample(a, mix, x, w_pool, pool_scale, w_out_a, w_out_p):
    m, d = x.shape
    return pl.pallas_call(
        _mix_out_sample_kernel,
        out_shape=jax.ShapeDtypeStruct((m, d), F32),
        compiler_params=pltpu.CompilerParams(vmem_limit_bytes=VMEM_LIMIT_BYTES),
        name="mix_out_sample",
    )(a, mix, x, w_pool, pool_scale, w_out_a, w_out_p)


def _top2(logits):
    lane = lax.broadcasted_iota(jnp.int32, logits.shape, 1)
    lg = jnp.where(lane < N_EXPERTS, logits, -jnp.inf)
    m1 = jnp.max(lg, axis=-1, keepdims=True)
    i1 = jnp.min(jnp.where(lg == m1, lane, LANES), axis=-1, keepdims=True)
    rest = jnp.where(lane == i1, -jnp.inf, lg)
    m2 = jnp.max(rest, axis=-1, keepdims=True)
    i2 = jnp.min(jnp.where(rest == m2, lane, LANES), axis=-1, keepdims=True)
    e2 = jnp.exp(m2 - m1)
    return lane, i1, i2, 1.0 / (1.0 + e2), e2 / (1.0 + e2)


def _top2_gates(logits):
    lane, i1, i2, g1, g2 = _top2(logits)
    return jnp.where(lane == i1, g1, 0.0) + jnp.where(lane == i2, g2, 0.0)


ROUTE_E1, ROUTE_E2, ROUTE_R1, ROUTE_R2, ROUTE_G1, ROUTE_G2 = range(6)
EXPERT_TILE = 512


def _routing_record(hn, wr_ref, tri_ref):
    logits = jnp.dot(hn, wr_ref[...], preferred_element_type=F32)
    lane, i1, i2, g1, g2 = _top2(logits)
    chosen = jnp.where(jnp.logical_or(lane == i1, lane == i2), 1.0, 0.0).astype(BF16)
    before = jnp.dot(tri_ref[...], chosen, preferred_element_type=F32)
    r1 = jnp.sum(jnp.where(lane == i1, before, 0.0), axis=-1, keepdims=True)
    r2 = jnp.sum(jnp.where(lane == i2, before, 0.0), axis=-1, keepdims=True)
    rec = jnp.zeros(logits.shape, F32)
    for pos, val in ((ROUTE_E1, i1.astype(F32)), (ROUTE_E2, i2.astype(F32)), (ROUTE_R1, r1), (ROUTE_R2, r2),
                     (ROUTE_G1, g1), (ROUTE_G2, g2)):
        rec = jnp.where(lane == pos, val, rec)
    return rec


def _row_copy(src, src_row, dst, dst_row, sem):
    return pltpu.make_async_copy(src.at[pl.ds(src_row, 1)], dst.at[pl.ds(dst_row, 1)], sem)


def _scatter_rows_kernel(dest_ref, x_ref, o_hbm, sem):
    tm = x_ref.shape[0]

    def issue(r, carry):
        for k in range(2):
            _row_copy(x_ref, r, o_hbm, dest_ref[0, 0, 2 * r + k], sem).start(priority=k)
        return carry

    def drain(r, carry):
        for k in range(2):
            _row_copy(x_ref, 0, o_hbm, 0, sem).wait()
        return carry

    lax.fori_loop(0, tm, issue, 0, unroll=8)
    lax.fori_loop(0, tm, drain, 0, unroll=8)


def scatter_rows(x, dest, tm):
    m, d = x.shape
    dest3 = dest.reshape(m // tm, 1, 2 * tm)
    return pl.pallas_call(
        _scatter_rows_kernel,
        grid=(m // tm,),
        in_specs=[
            pl.BlockSpec((1, 1, 2 * tm), lambda i: (i, 0, 0), memory_space=pltpu.SMEM),
            pl.BlockSpec((tm, d), lambda i: (i, 0)),
        ],
        out_specs=pl.BlockSpec(memory_space=pl.ANY),
        out_shape=jax.ShapeDtypeStruct((2 * m, d), x.dtype),
        scratch_shapes=[pltpu.SemaphoreType.DMA(())],
        compiler_params=_params("arbitrary"),
        name="scatter_rows",
    )(dest3, x)


def _expert_kernel(tile_ref, exp_ref, lo_ref, hi_ref, nvis_ref, x_ref, g_ref, wg_ref, wu_ref, wd_ref, o_ref,
                   hn_ref, acc_ref):
    v = pl.program_id(0)
    j = pl.program_id(1)

    @pl.when(v < nvis_ref[0])
    def _():
        @pl.when(j == 0)
        def _():
            hn_ref[...] = _rmsnorm(x_ref[...], g_ref[...]).astype(BF16)
            acc_ref[...] = jnp.zeros_like(acc_ref)

        hn = hn_ref[...]
        a = jnp.dot(hn, wg_ref[...], preferred_element_type=F32)
        b = jnp.dot(hn, wu_ref[...], preferred_element_type=F32)
        h = (a * _sigmoid(a) * b).astype(BF16)
        acc_ref[...] += jnp.dot(h, wd_ref[...], preferred_element_type=F32)

        @pl.when(j == pl.num_programs(1) - 1)
        def _():
            row = lax.broadcasted_iota(jnp.int32, (o_ref.shape[0], 1), 0)
            mine = jnp.logical_and(row >= lo_ref[v], row < hi_ref[v])
            first = jnp.logical_or(v == 0, tile_ref[v] != tile_ref[jnp.maximum(v - 1, 0)])

            @pl.when(first)
            def _():
                o_ref[...] = jnp.where(mine, acc_ref[...], 0.0)

            @pl.when(jnp.logical_not(first))
            def _():
                o_ref[...] = jnp.where(mine, acc_ref[...], o_ref[...])


def expert_swiglu(xs, g, w_gate, w_up, w_down, visits, tm, tf):
    m, d = xs.shape
    f = w_gate.shape[2]
    n_f = f // tf
    assert m % tm == 0 and f % tf == 0
    tile, expert, lo, hi, n_visit = visits

    def f_blk(v, j):
        return j + (v % 2) * (n_f - 1 - 2 * j)

    grid_spec = pltpu.PrefetchScalarGridSpec(
        num_scalar_prefetch=5,
        grid=(tile.shape[0], n_f),
        in_specs=[
            pl.BlockSpec((tm, d), lambda v, j, t, e, lo, hi, n: (t[v], 0)),
            pl.BlockSpec((1, d), lambda v, j, t, e, lo, hi, n: (0, 0)),
            pl.BlockSpec((None, d, tf), lambda v, j, t, e, lo, hi, n: (e[v], 0, f_blk(v, j))),
            pl.BlockSpec((None, d, tf), lambda v, j, t, e, lo, hi, n: (e[v], 0, f_blk(v, j))),
            pl.BlockSpec((None, tf, d), lambda v, j, t, e, lo, hi, n: (e[v], f_blk(v, j), 0)),
        ],
        out_specs=pl.BlockSpec((tm, d), lambda v, j, t, e, lo, hi, n: (t[v], 0)),
        scratch_shapes=[pltpu.VMEM((tm, d), BF16), pltpu.VMEM((tm, d), F32)],
    )
    return pl.pallas_call(
        _expert_kernel,
        grid_spec=grid_spec,
        out_shape=jax.ShapeDtypeStruct((m, d), F32),
        compiler_params=_params("arbitrary", "arbitrary"),
        name="expert_swiglu",
    )(tile, expert, lo, hi, n_visit, xs, g, w_gate, w_up, w_down)


def _combine_kernel(dest_ref, next_ref, x_ref, rec_ref, gf_ref, ys_hbm, o_ref, y_ref, sem):
    i = pl.program_id(0)
    tm = x_ref.shape[0]

    def fetch(idx_ref, slot):
        def issue(r, carry):
            for k in range(2):
                _row_copy(ys_hbm, idx_ref[0, 0, 2 * r + k], y_ref.at[slot, k], r, sem.at[slot]).start(priority=k)
            return carry

        lax.fori_loop(0, tm, issue, 0, unroll=8)

    def finish(slot):
        def drain(r, carry):
            for k in range(2):
                _row_copy(ys_hbm, 0, y_ref.at[slot, k], 0, sem.at[slot]).wait()
            return carry

        lax.fori_loop(0, tm, drain, 0, unroll=8)
        g1 = rec_ref[:, ROUTE_G1:ROUTE_G1 + 1]
        g2 = rec_ref[:, ROUTE_G2:ROUTE_G2 + 1]
        out = x_ref[...] + (g1 * y_ref[slot, 0] + g2 * y_ref[slot, 1])
        o_ref[...] = _rmsnorm(out, gf_ref[...])

    @pl.when(i == 0)
    def _():
        fetch(dest_ref, 0)

    for slot in range(2):
        @pl.when(i % 2 == slot)
        def _(slot=slot):
            @pl.when(i + 1 < pl.num_programs(0))
            def _():
                fetch(next_ref, 1 - slot)

            finish(slot)


def combine_rows(x, rec, dest, ys, g_final, tm):
    m, d = x.shape
    n_tile = m // tm
    dest3 = dest.reshape(n_tile, 1, 2 * tm)
    return pl.pallas_call(
        _combine_kernel,
        grid=(n_tile,),
        in_specs=[
            pl.BlockSpec((1, 1, 2 * tm), lambda i: (i, 0, 0), memory_space=pltpu.SMEM),
            pl.BlockSpec((1, 1, 2 * tm), lambda i: (jnp.minimum(i + 1, n_tile - 1), 0, 0), memory_space=pltpu.SMEM),
            pl.BlockSpec((tm, d), lambda i: (i, 0)),
            pl.BlockSpec((tm, LANES), lambda i: (i, 0)),
            pl.BlockSpec((1, d), lambda i: (0, 0)),
            pl.BlockSpec(memory_space=pl.ANY),
        ],
        out_specs=pl.BlockSpec((tm, d), lambda i: (i, 0)),
        out_shape=jax.ShapeDtypeStruct((m, d), F32),
        scratch_shapes=[pltpu.VMEM((2, 2, tm, d), F32), pltpu.SemaphoreType.DMA((2,))],
        compiler_params=_params("arbitrary"),
        name="combine_rows",
    )(dest3, dest3, x, rec, g_final, ys)


def _visit_list(rec, tile_tokens, tm):
    n = rec.shape[0]
    e = rec[:, ROUTE_E1:ROUTE_E2 + 1].astype(jnp.int32)
    rank = rec[:, ROUTE_R1:ROUTE_R2 + 1].astype(jnp.int32)
    onehot = (e[:, :, None] == jnp.arange(N_EXPERTS)[None, None, :]).astype(jnp.int32)
    per_tile = onehot.reshape(n // tile_tokens, tile_tokens * 2, N_EXPERTS).sum(axis=1)
    count = per_tile.sum(axis=0)
    start = jnp.cumsum(count) - count
    tile_base = jnp.cumsum(per_tile, axis=0) - per_tile
    base = (start[None, :] + tile_base)[:, None, :]
    dest = rank + jnp.sum(onehot.reshape(n // tile_tokens, tile_tokens * 2, N_EXPERTS) * base, axis=-1).reshape(n, 2)

    n_tile = 2 * n // tm
    n_visit_max = n_tile + N_EXPERTS
    end = start + count
    first_tile = start // tm
    last_tile = jnp.maximum(end - 1, start) // tm
    n_vis = jnp.where(count > 0, last_tile - first_tile + 1, 0)
    vis_end = jnp.cumsum(n_vis)
    vis_start = vis_end - n_vis
    total = vis_end[-1]
    v = jnp.arange(n_visit_max)
    vc = jnp.minimum(v, total - 1)
    ex = jnp.sum(vc[:, None] >= vis_end[None, :], axis=1)
    tile = first_tile[ex] + vc - vis_start[ex]
    lo = jnp.clip(start[ex] - tile * tm, 0, tm)
    hi = jnp.clip(end[ex] - tile * tm, 0, tm)
    live = v < total
    as_i32 = lambda a: a.astype(jnp.int32)
    visits = (as_i32(tile), as_i32(ex), as_i32(jnp.where(live, lo, 0)), as_i32(jnp.where(live, hi, 0)),
              as_i32(total).reshape(1))
    return dest.astype(jnp.int32), visits


def routed_experts(x, rec, rank_tile, g, w_gate, w_up, w_down, g_final):
    dest, visits = _visit_list(rec, rank_tile, EXPERT_TILE)
    xs = scatter_rows(x, dest, 512)
    ys = expert_swiglu(xs, g, w_gate, w_up, w_down, visits, EXPERT_TILE, w_gate.shape[2] // 2)
    return combine_rows(x, rec, dest, ys, g_final, 256)


def _swiglu_kernel(*refs, n_expert, final_norm):
    x_ref, g_ref = refs[0], refs[1]
    k = 2
    if n_expert > 1:
        wr_ref = refs[k]
        k += 1
    wg_ref, wu_ref, wd_ref = refs[k:k + 3]
    k += 3
    if final_norm:
        gf_ref = refs[k]
        k += 1
    o_ref, hn_ref, acc_ref = refs[k:k + 3]
    if n_expert > 1:
        gate_ref = refs[k + 3]
    e = pl.program_id(1)
    j = pl.program_id(2)

    @pl.when(jnp.logical_and(e == 0, j == 0))
    def _():
        hn = _rmsnorm(x_ref[...], g_ref[...])
        hn_ref[...] = hn.astype(BF16)
        acc_ref[...] = jnp.zeros_like(acc_ref)
        if n_expert > 1:
            logits = jnp.dot(hn.astype(BF16), wr_ref[...], preferred_element_type=F32)
            gate_ref[...] = _top2_gates(logits)

    hn = hn_ref[...]
    a = jnp.dot(hn, wg_ref[...], preferred_element_type=F32)
    b = jnp.dot(hn, wu_ref[...], preferred_element_type=F32)
    h = (a * _sigmoid(a) * b).astype(BF16)
    y = jnp.dot(h, wd_ref[...], preferred_element_type=F32)
    if n_expert > 1:
        lane = lax.broadcasted_iota(jnp.int32, gate_ref.shape, 1)
        gate = jnp.sum(jnp.where(lane == e, gate_ref[...], 0.0), axis=-1, keepdims=True)
        y = gate * y
    acc_ref[...] += y

    @pl.when(jnp.logical_and(e == n_expert - 1, j == pl.num_programs(2) - 1))
    def _():
        out = x_ref[...] + acc_ref[...]
        if final_norm:
            out = _rmsnorm(out, gf_ref[...])
        o_ref[...] = out


def swiglu_block(x, g, w_gate, w_up, w_down, tm, tf, w_router=None, g_final=None):
    m, d = x.shape
    n_expert, _, f = w_gate.shape
    assert m % tm == 0 and f % tf == 0
    args = [x, g]
    in_specs = [pl.BlockSpec((tm, d), lambda i, e, j: (i, 0)), pl.BlockSpec((1, d), lambda i, e, j: (0, 0))]
    if n_expert > 1:
        args.append(w_router)
        in_specs.append(pl.BlockSpec((d, LANES), lambda i, e, j: (0, 0)))
    args += [w_gate, w_up, w_down]
    in_specs += [
        pl.BlockSpec((None, d, tf), lambda i, e, j: (e, 0, j)),
        pl.BlockSpec((None, d, tf), lambda i, e, j: (e, 0, j)),
        pl.BlockSpec((None, tf, d), lambda i, e, j: (e, j, 0)),
    ]
    if g_final is not None:
        args.append(g_final)
        in_specs.append(pl.BlockSpec((1, d), lambda i, e, j: (0, 0)))
    scratch = [pltpu.VMEM((tm, d), BF16), pltpu.VMEM((tm, d), F32)]
    if n_expert > 1:
        scratch.append(pltpu.VMEM((tm, LANES), F32))
    kern = functools.partial(_swiglu_kernel, n_expert=n_expert, final_norm=g_final is not None)
    return pl.pallas_call(
        kern,
        grid=(m // tm, n_expert, f // tf),
        in_specs=in_specs,
        out_specs=pl.BlockSpec((tm, d), lambda i, e, j: (i, 0)),
        out_shape=jax.ShapeDtypeStruct((m, d), F32),
        scratch_shapes=scratch,
        compiler_params=_params("parallel", "arbitrary", "arbitrary"),
        name="swiglu_block",
    )(*args)


def _ln_silu_pw2(z, lng_ref, lnb_ref, w2_ref, b2_ref):
    mu = jnp.mean(z, axis=-1, keepdims=True)
    zc = z - mu
    var = jnp.mean(zc * zc, axis=-1, keepdims=True)
    zn = zc * lax.rsqrt(var + LN_EPS) * lng_ref[...] + lnb_ref[...]
    act = (zn * _sigmoid(zn)).astype(BF16)
    return jnp.dot(act, w2_ref[...], preferred_element_type=F32) + b2_ref[...]


def _conv_kernel(glu_ref, halo_ref, x_ref, wdw_ref, bdw_ref, lng_ref, lnb_ref, w2_ref, b2_ref, gr_ref, wr_ref, tri_ref,
                 o_ref, rec_ref, ext_ref, z_ref, *, tiles_per_seq, row_chunk):
    i = pl.program_id(0)
    tm = glu_ref.shape[0]
    halo = halo_ref.shape[0]
    n_slab = glu_ref.shape[1] // LANES
    at_start = i % tiles_per_seq == 0
    for cb in range(n_slab):
        cols = slice(cb * LANES, (cb + 1) * LANES)
        ext_ref[cb, 0:halo, :] = jnp.where(at_start, 0.0, _round_bf16(halo_ref[:, cols]))
        ext_ref[cb, halo:halo + tm, :] = _round_bf16(glu_ref[:, cols])
    base = halo - (CONV_K - 1)

    def slab(cb, carry):
        for rc in range(tm // row_chunk):
            r0 = rc * row_chunk
            acc = jnp.zeros((row_chunk, LANES), F32) + bdw_ref[cb]
            for tap in range(CONV_K):
                acc = acc + ext_ref[cb, r0 + base + tap:r0 + base + tap + row_chunk, :] * wdw_ref[cb, tap:tap + 1, :]
            z_ref[cb, r0:r0 + row_chunk, :] = acc
        return carry

    lax.fori_loop(0, n_slab, slab, 0)
    z = jnp.concatenate([z_ref[cb] for cb in range(n_slab)], axis=-1)
    out = x_ref[...] + _ln_silu_pw2(z, lng_ref, lnb_ref, w2_ref, b2_ref)
    o_ref[...] = out
    rec_ref[...] = _routing_record(_rmsnorm(out, gr_ref[...]).astype(BF16), wr_ref, tri_ref)


def conv_block(glu, x, w_dw, b_dw, ln_g, ln_b, w_pw2, b_pw2, g_route, w_router, tm, seq):
    m, d = x.shape
    tri = (jnp.arange(tm)[:, None] > jnp.arange(tm)[None, :]).astype(BF16)
    halo = 32
    n_slab = d // LANES
    assert seq % tm == 0 and tm % halo == 0
    kern = functools.partial(_conv_kernel, tiles_per_seq=seq // tm, row_chunk=64)
    vec = pl.BlockSpec((1, d), lambda i: (0, 0))
    w_dw_slab = jnp.transpose(w_dw.reshape(w_dw.shape[0], n_slab, LANES), (1, 0, 2))
    b_dw_slab = b_dw.reshape(n_slab, 1, LANES)
    return pl.pallas_call(
        kern,
        grid=(m // tm,),
        in_specs=[
            pl.BlockSpec((tm, d), lambda i: (i, 0)),
            pl.BlockSpec((halo, d), lambda i: (jnp.maximum(i * (tm // halo) - 1, 0), 0)),
            pl.BlockSpec((tm, d), lambda i: (i, 0)),
            pl.BlockSpec(w_dw_slab.shape, lambda i: (0, 0, 0)),
            pl.BlockSpec(b_dw_slab.shape, lambda i: (0, 0, 0)),
            vec, vec,
            pl.BlockSpec((d, d), lambda i: (0, 0)),
            vec,
            vec,
            pl.BlockSpec((d, LANES), lambda i: (0, 0)),
            pl.BlockSpec((tm, tm), lambda i: (0, 0)),
        ],
        out_specs=[pl.BlockSpec((tm, d), lambda i: (i, 0)), pl.BlockSpec((tm, LANES), lambda i: (i, 0))],
        out_shape=[jax.ShapeDtypeStruct((m, d), F32), jax.ShapeDtypeStruct((m, LANES), F32)],
        scratch_shapes=[pltpu.VMEM((n_slab, tm + halo, LANES), F32), pltpu.VMEM((n_slab, tm, LANES), F32)],
        compiler_params=_params("parallel"),
        name="conv_block",
    )(glu, glu, x, w_dw_slab, b_dw_slab, ln_g, ln_b, w_pw2, b_pw2, g_route, w_router, tri)


def _conv_sample_kernel(ext_ref, x_ref, wdw_ref, bdw_ref, lng_ref, lnb_ref, w2_ref, b2_ref, o_ref, z_ref,
                        *, steps):
    n = ext_ref.shape[0]
    z_ref[...] = jnp.zeros_like(z_ref)

    def one(s, carry):
        acc = jnp.zeros((steps, z_ref.shape[2]), F32) + bdw_ref[...]
        for tap in range(CONV_K):
            acc = acc + _round_bf16(ext_ref[s, tap:tap + steps, :]) * wdw_ref[tap:tap + 1, :]
        z_ref[s, 0:steps, :] = acc
        return carry

    lax.fori_loop(0, n, one, 0)
    z = z_ref[...].reshape(n * z_ref.shape[1], z_ref.shape[2])
    y = _ln_silu_pw2(z, lng_ref, lnb_ref, w2_ref, b2_ref)
    o_ref[...] = x_ref[...] + y.reshape(o_ref.shape)


def conv_block_sample(ext, x, w_dw, b_dw, ln_g, ln_b, w_pw2, b_pw2, steps):
    n, _, d = ext.shape
    kern = functools.partial(_conv_sample_kernel, steps=steps)
    return pl.pallas_call(
        kern,
        out_shape=jax.ShapeDtypeStruct((n, 8, d), F32),
        scratch_shapes=[pltpu.VMEM((n, 8, d), F32)],
        compiler_params=pltpu.CompilerParams(vmem_limit_bytes=VMEM_LIMIT_BYTES),
        name="conv_block_sample",
    )(ext, x, w_dw, b_dw, ln_g, ln_b, w_pw2, b_pw2)


def _sample_attn_kernel(q_ref, kn_ref, vn_ref, kvt_ref, c0_ref, c1_ref, c2_ref, b0_ref, b1_ref, b2_ref, bn_ref,
                        ext_ref, a_ref, mix_ref, o0_ref, o1_ref, o2_ref, *, steps):
    caches = (c0_ref, c1_ref, c2_ref)
    biases = (b0_ref, b1_ref, b2_ref)
    new_caches = (o0_ref, o1_ref, o2_ref)
    n_head = q_ref.shape[2]
    lane = lax.broadcasted_iota(jnp.int32, (n_head, A_HEAD_DIM, LANES), 2)
    keep = LANES - steps

    outs, lses = [], []
    for branch in range(N_BRANCH):
        c_ref = caches[branch]
        q = (q_ref[0, branch] * (A_HEAD_DIM ** -0.5)).astype(BF16)
        kt = c_ref[0, 0].astype(BF16)
        vt = c_ref[0, 1].astype(BF16)
        kn = kn_ref[0, branch].astype(BF16)
        vn = vn_ref[0, branch].astype(BF16)
        s_c = jnp.einsum("hqd,hdl->hql", q, kt, preferred_element_type=F32) + biases[branch][...]
        s_n = jnp.einsum("hqd,hkd->hqk", q, kn, preferred_element_type=F32) + bn_ref[branch]
        mx = jnp.maximum(jnp.max(s_c, axis=-1, keepdims=True), jnp.max(s_n, axis=-1, keepdims=True))
        den = (jnp.sum(jnp.exp(s_c - mx), axis=-1, keepdims=True)
               + jnp.sum(jnp.exp(s_n - mx), axis=-1, keepdims=True))
        lse = mx + jnp.log(den)
        p_c = jnp.exp(s_c - lse).astype(BF16)
        p_n = jnp.exp(s_n - lse).astype(BF16)
        outs.append(jnp.einsum("hql,hdl->hqd", p_c, vt, preferred_element_type=F32)
                    + jnp.einsum("hqk,hkd->hqd", p_n, vn, preferred_element_type=F32))
        lses.append(lse)
    mx = jnp.maximum(jnp.maximum(lses[0], lses[1]), lses[2])
    wts = [jnp.exp(lse - mx) for lse in lses]
    inv = 1.0 / (wts[0] + wts[1] + wts[2])
    a_ref[0] = outs[0] * (wts[0] * inv) + outs[1] * (wts[1] * inv) + outs[2] * (wts[2] * inv)

    for branch in range(N_BRANCH):
        c_ref, o_ref = caches[branch], new_caches[branch]
        n_tile = c_ref.shape[-1] // LANES
        for kv in range(2):
            nxt = pltpu.roll(c_ref[0, kv, :, :, 0:LANES], keep, 2)
            for j in range(n_tile):
                cur = nxt
                if j + 1 < n_tile:
                    nxt = pltpu.roll(c_ref[0, kv, :, :, (j + 1) * LANES:(j + 2) * LANES], keep, 2)
                else:
                    nxt = kvt_ref[0, branch, kv]
                o_ref[0, kv, :, :, j * LANES:(j + 1) * LANES] = jnp.where(lane < keep, cur, nxt)

    base = ext_ref.shape[1] - steps
    for t in range(steps):
        for grp, win in enumerate(POOL_WINDOWS):
            cols = slice(grp * POOL_GROUP_DIM, (grp + 1) * POOL_GROUP_DIM)
            last = base + t
            tot = jnp.sum(ext_ref[0, last - win + 1:last + 1, cols], axis=0, keepdims=True)
            mix_ref[0, t:t + 1, cols] = tot / float(win) - ext_ref[0, last:last + 1, cols]


def sample_attention(q, kn, vn, kvt, caches, biases, bias_new, ext_pool, steps):
    n, rows = q.shape[0], q.shape[3]
    hg = A_HEADS // 2
    per_head = (hg, rows, A_HEAD_DIM)
    in_specs = [pl.BlockSpec((1, N_BRANCH) + per_head, lambda i, j: (i, 0, j, 0, 0))] * 3
    in_specs.append(pl.BlockSpec((1, N_BRANCH, 2, hg, A_HEAD_DIM, LANES), lambda i, j: (i, 0, 0, j, 0, 0)))
    cache_specs = [pl.BlockSpec((1, 2, hg, A_HEAD_DIM, c.shape[-1]), lambda i, j: (i, 0, j, 0, 0)) for c in caches]
    in_specs += cache_specs
    in_specs += [pl.BlockSpec((hg, rows, b.shape[-1]), lambda i, j: (j, 0, 0)) for b in biases]
    in_specs += [
        pl.BlockSpec((N_BRANCH, hg, rows, rows), lambda i, j: (0, j, 0, 0)),
        pl.BlockSpec((1,) + ext_pool.shape[1:], lambda i, j: (i, 0, 0)),
    ]
    kern = functools.partial(_sample_attn_kernel, steps=steps)
    return pl.pallas_call(
        kern,
        grid=(n, A_HEADS // hg),
        in_specs=in_specs,
        out_specs=[
            pl.BlockSpec((1,) + per_head, lambda i, j: (i, j, 0, 0)),
            pl.BlockSpec((1, steps, POOL_WIDTH), lambda i, j: (i, 0, 0)),
        ] + cache_specs,
        out_shape=[
            jax.ShapeDtypeStruct((n, A_HEADS, rows, A_HEAD_DIM), F32),
            jax.ShapeDtypeStruct((n, steps, POOL_WIDTH), F32),
        ] + [jax.ShapeDtypeStruct(c.shape, F32) for c in caches],
        compiler_params=_params("parallel", "arbitrary"),
        name="sample_attention",
    )(q, kn, vn, kvt, *caches, *biases, bias_new, ext_pool)


def _t5_bucket(dist):
    max_exact = NUM_BUCKETS // 2
    d = jnp.maximum(dist, 1).astype(F32)
    large = max_exact + (jnp.log(d / max_exact) / math.log(REL_MAX_DIST / max_exact)
                         * (NUM_BUCKETS - max_exact)).astype(jnp.int32)
    return jnp.where(dist < max_exact, dist, jnp.minimum(large, NUM_BUCKETS - 1))


def _slot_bias(rel_bias):
    rb = rel_bias.astype(F32).reshape(NUM_BUCKETS, N_BRANCH, A_HEADS)
    slot = jnp.arange(BAND + 1)
    out = []
    for g, (_, dil) in enumerate(A_GROUPS):
        pick = jax.nn.one_hot(_t5_bucket(slot * dil), NUM_BUCKETS, dtype=F32)
        out.append(jnp.einsum("sb,bh->hs", pick, rb[:, g], precision=lax.Precision.HIGHEST))
    return jnp.stack(out)


def _sliding_rows(v, n_rows, width):
    period = v.shape[-1]
    assert period >= n_rows + width - 1
    flat = jnp.tile(v, (1,) * (v.ndim - 1) + (n_rows + 1,))[..., :(period + 1) * n_rows]
    return flat.reshape(v.shape[:-1] + (n_rows, period + 1))[..., :width]


def _prompt_bias_tables(rel_bias):
    sb = _slot_bias(rel_bias)
    lead = sb.shape[:2]
    diag = jnp.concatenate([jnp.full(lead + (BAND - 1,), NEG, F32), sb[..., ::-1],
                            jnp.full(lead + (BAND,), NEG, F32)], axis=-1)
    return _sliding_rows(diag, BAND, 2 * BAND)[:, :, ::-1]


def _sample_bias_tables(rel_bias, steps, rows):
    sb = _slot_bias(rel_bias)
    cache_tables = []
    for g, (window, dil) in enumerate(A_GROUPS):
        spread = jnp.concatenate([sb[g][..., None], jnp.full(sb[g].shape + (dil - 1,), NEG, F32)], axis=-1)
        by_dist = spread.reshape(A_HEADS, (BAND + 1) * dil)
        need = window + rows
        by_dist = jnp.pad(by_dist, ((0, 0), (0, max(need - by_dist.shape[1], 0))), constant_values=NEG)[:, :need]
        by_dist = jnp.where(jnp.arange(need)[None, :] <= window, by_dist, NEG)
        cache_tables.append(_sliding_rows(by_dist[:, ::-1], rows, window)[:, ::-1])
    t_q = jnp.arange(rows)[:, None]
    t_k = jnp.arange(rows)[None, :]
    new_tables = []
    for g, (_, dil) in enumerate(A_GROUPS):
        diff = t_q - t_k
        ok = (diff >= 0) & (diff % dil == 0) & (t_k < steps)
        pick = jax.nn.one_hot(jnp.where(ok, diff // dil, BAND + 1).reshape(-1), BAND + 1, dtype=F32)
        vals = jnp.einsum("ps,hs->hp", pick, sb[g], precision=lax.Precision.HIGHEST).reshape(A_HEADS, rows, rows)
        new_tables.append(jnp.where(ok[None], vals, NEG))
    return cache_tables, jnp.stack(new_tables)


def kernel(x_prompt, x_sample, cache_kv_w128, cache_kv_w512, cache_kv_w2048, state_pool, state_conv, norm_mix0, w_in0, rel_bias, w_pool, pool_scale, w_out0, norm_ffn0, w_ff_gate, w_ff_up, w_ff_down, norm_mix1, w_pw1, b_pw1, w_dw, b_dw, ln_g, ln_b, w_pw2, b_pw2, norm_ffn1, w_router, we_gate, we_up, we_down, norm_final):
    batch, seq, d = x_prompt.shape
    n_s, steps, _ = x_sample.shape
    caches_in = (cache_kv_w128, cache_kv_w512, cache_kv_w2048)
    assert norm_mix0.shape[0] == 1 and norm_mix1.shape[0] == 1, "two layers: one mixer of each kind"
    assert seq % CHUNK == 0 and steps <= min(dil for _, dil in A_GROUPS[1:])
    for c, (window, _) in zip(caches_in, A_GROUPS):
        assert c.shape[2] == window, "cache must hold a full window"
    qkv_w = N_BRANCH * 3 * A_WIDTH

    w_in = w_in0[0]
    w_in_bf = w_in.astype(BF16)
    w_kv = w_in_bf[:, A_WIDTH * N_BRANCH:qkv_w]
    g_mix0, g_ffn0 = norm_mix0, norm_ffn0
    g_mix1, g_ffn1 = norm_mix1, norm_ffn1
    g_final = norm_final.reshape(1, d)
    wpool_bf = w_pool[0].astype(BF16)
    w_out_a = w_out0[0, :A_WIDTH].astype(BF16)
    w_out_p = w_out0[0, A_WIDTH:].astype(BF16)
    wff = (w_ff_gate.astype(BF16), w_ff_up.astype(BF16), w_ff_down.astype(BF16))
    w_pw1_bf = w_pw1[0].astype(BF16)
    w_pw2_bf = w_pw2[0].astype(BF16)
    w_dw_pad = jnp.pad(w_dw[0], ((0, 32 - CONV_K), (0, 0)))
    wex = (we_gate[0].astype(BF16), we_up[0].astype(BF16), we_down[0].astype(BF16))
    w_router_pad = jnp.pad(w_router[0], ((0, 0), (0, LANES - N_EXPERTS))).astype(BF16)
    zeros = lambda n: jnp.zeros((1, n), F32)

    xp = x_prompt.reshape(batch * seq, d)
    qkv, u_p = qkv_perm(xp, g_mix0, w_in_bf)
    a_p = band_attention(qkv, _prompt_bias_tables(rel_bias), batch, seq)
    h_p = mix_out(a_p, u_p, xp, wpool_bf, pool_scale, w_out_a, w_out_p, 512, seq)
    h_p = swiglu_block(h_p, g_ffn0, *wff, tm=512, tf=1408)
    glu_p = rms_glu(h_p, g_mix1, w_pw1_bf, b_pw1, 512, 1024)
    conv_tile = 512
    h_p, rec_p = conv_block(glu_p, h_p, w_dw_pad, b_dw, ln_g, ln_b, w_pw2_bf, b_pw2, g_ffn1, w_router_pad,
                            conv_tile, seq)
    y_p = routed_experts(h_p, rec_p, conv_tile, g_ffn1, *wex, g_final)

    keep = A_GROUPS[-1][0]
    kv_tail = rms_matmul_t(x_prompt, g_mix0, w_kv.T, seq - keep, keep, 1024, 512)
    kv_tail = kv_tail.reshape(batch, 2, N_BRANCH, A_HEADS, A_HEAD_DIM, keep)
    kv_p = [jnp.transpose(kv_tail[:, :, g, :, :, keep - window:], (0, 4, 1, 2, 3))[None]
            for g, (window, _) in enumerate(A_GROUPS)]
    u_p3 = u_p.reshape(batch, seq, POOL_WIDTH)
    pool_p = u_p3[:, seq - POOL_BUF:][None]
    conv_p = glu_p.reshape(batch, seq, d)[:, seq - (CONV_K - 1):][None]

    m_s = n_s * steps
    xs = x_sample.reshape(m_s, d)
    proj = rms_matmul(xs, g_mix0, w_in_bf, zeros(w_in_bf.shape[1]), m_s, 512, F32)
    qkv_s = proj[:, :qkv_w].reshape(n_s, steps, 3, N_BRANCH, A_HEADS, A_HEAD_DIM)
    u_s = proj[:, qkv_w:].reshape(n_s, steps, POOL_WIDTH)
    rows = 8
    per_head = jnp.transpose(qkv_s, (2, 0, 3, 4, 1, 5))
    per_head = jnp.pad(per_head, ((0, 0),) * 4 + ((0, rows - steps), (0, 0)))
    new_cols = jnp.transpose(qkv_s[:, :, 1:], (0, 3, 2, 4, 5, 1))
    new_cols = jnp.pad(new_cols, ((0, 0),) * 5 + ((LANES - steps, 0),))
    caches_t = [jnp.transpose(c[0], (0, 2, 3, 4, 1)) for c in caches_in]
    ext_pool = jnp.concatenate([state_pool[0], u_s], axis=1)
    cache_bias, new_bias = _sample_bias_tables(rel_bias, steps, rows)
    a_s, mix_s, *caches_out = sample_attention(per_head[0], per_head[1], per_head[2], new_cols, caches_t,
                                               cache_bias, new_bias, ext_pool, steps)
    a_s = jnp.transpose(a_s[:, :, :steps], (0, 2, 1, 3)).reshape(m_s, A_WIDTH)
    h_s = mix_out_sample(a_s, mix_s.reshape(m_s, POOL_WIDTH), xs, wpool_bf, pool_scale, w_out_a, w_out_p)
    h_s = swiglu_block(h_s, g_ffn0, *wff, tm=m_s, tf=1408)
    glu_s = rms_glu(h_s, g_mix1, w_pw1_bf, b_pw1, m_s, 512)
    ext_conv = jnp.concatenate([state_conv[0], glu_s.reshape(n_s, steps, d)], axis=1)
    ext_conv_pad = jnp.pad(ext_conv, ((0, 0), (0, 40 - ext_conv.shape[1]), (0, 0)))
    h_s3 = jnp.pad(h_s.reshape(n_s, steps, d), ((0, 0), (0, 8 - steps), (0, 0)))
    h_s = conv_block_sample(ext_conv_pad, h_s3, w_dw_pad, b_dw, ln_g, ln_b, w_pw2_bf, b_pw2, steps)
    h_s = h_s[:, :steps].reshape(m_s, d)
    y_s = swiglu_block(h_s, g_ffn1, *wex, tm=m_s, tf=896, w_router=w_router_pad, g_final=g_final)

    kv_s = [jnp.transpose(c, (0, 4, 1, 2, 3))[None] for c in caches_out]
    pool_s = ext_pool[:, steps:][None]
    conv_s = ext_conv[:, steps:][None]

    return (y_p.reshape(batch, seq, d), y_s.reshape(n_s, steps, d), kv_p[0], kv_p[1], kv_p[2], pool_p, conv_p,
            kv_s[0], kv_s[1], kv_s[2], pool_s, conv_s)
```
